```python
import jax, jax.numpy as jnp
from jax import lax
import numpy as np

D_MODEL = 1024
BATCH = 16
SEQ = 256
DEPTH = 2
DEC_BATCH = 4
DEC_SEQ = 2048
PAST_LEN = 256

GRID_W = 64
N_HEADS = 8
KV_HEADS = 2
HEAD_DIM = 64
Q_GROUP = N_HEADS // KV_HEADS
ATTN_WIDTH = N_HEADS * HEAD_DIM
KV_WIDTH = KV_HEADS * HEAD_DIM
WINDOW = 128
BLOCK = 128
AXIS_ROT = HEAD_DIM // 2
ROPE_BASE = 10000.0
LRU_WIDTH = 512
LRU_BLOCKS = 8
LRU_BW = LRU_WIDTH // LRU_BLOCKS
LRU_C = 8.0
CONV_W = 4
CONV_LEFT = 2
FOURIER_WIDTH = 512
FOURIER_GROUPS = 4
FOURIER_GW = FOURIER_WIDTH // FOURIER_GROUPS
D_FF = 2816
N_BRANCH = 3
N_SUB = 3
EPS = 1e-6
NEG = -1e30
IN_WIDTH = ATTN_WIDTH + 2 * KV_WIDTH + 2 * LRU_WIDTH + FOURIER_WIDTH
SPLITS = [ATTN_WIDTH,
          ATTN_WIDTH + KV_WIDTH,
          ATTN_WIDTH + 2 * KV_WIDTH,
          ATTN_WIDTH + 2 * KV_WIDTH + LRU_WIDTH,
          ATTN_WIDTH + 2 * KV_WIDTH + 2 * LRU_WIDTH]

kernel_name = "hybrid_diffusion_prefix_step"

f32 = jnp.float32


def rmsnorm(x, g):
    xf = x.astype(f32)
    y = xf * lax.rsqrt(jnp.mean(xf * xf, axis=-1, keepdims=True) + EPS)
    return (y * g.astype(f32)).astype(x.dtype)


def swiglu(h, wg, wu, wd):
    return (jax.nn.silu(h @ wg) * (h @ wu)) @ wd


def axial_rope_tables(n_tokens):
    rows = n_tokens // GRID_W
    row = jnp.repeat(jnp.arange(rows), GRID_W).astype(f32)
    col = jnp.tile(jnp.arange(GRID_W), rows).astype(f32)
    inv = ROPE_BASE ** (-jnp.arange(0, AXIS_ROT, 2, dtype=f32) / AXIS_ROT)
    ang = jnp.stack([row[:, None] * inv, col[:, None] * inv], axis=1)
    return jnp.cos(ang), jnp.sin(ang)


def apply_rope(x, cos, sin):
    B, L, H, _ = x.shape
    xr = x.astype(f32).reshape(B, L, H, 2, 2, AXIS_ROT // 2)
    x1, x2 = xr[..., 0, :], xr[..., 1, :]
    c, s = cos[:, None], sin[:, None]
    out = jnp.stack([x1 * c - x2 * s, x1 * s + x2 * c], axis=-2)
    return out.reshape(B, L, H, HEAD_DIM).astype(x.dtype)


def sink_attend(q, k, v, bias, sink):
    s = jnp.einsum('bqhgd,bkhd->bhgqk', q.astype(f32), k.astype(f32)) * (HEAD_DIM ** -0.5)
    if bias is not None:
        s = s + bias
    sk = sink.astype(f32)[None, :, :, None, None]
    m = jnp.maximum(jnp.max(s, axis=-1, keepdims=True), sk)
    p = jnp.exp(s - m)
    denom = jnp.sum(p, axis=-1, keepdims=True) + jnp.exp(sk - m)
    o = jnp.einsum('bhgqk,bkhd->bqhgd', p / denom, v.astype(f32))
    return o.astype(q.dtype)


def context_attention(q, k, v, sink):
    B, L = q.shape[:2]
    nb = L // BLOCK
    qb = q.reshape(B, nb, BLOCK, KV_HEADS, Q_GROUP, HEAD_DIM).swapaxes(0, 1)
    ob = lax.map(lambda qi: sink_attend(qi, k, v, None, sink), qb)
    return ob.swapaxes(0, 1).reshape(B, L, ATTN_WIDTH)


def latent_attention(q, k, v, k_ctx, v_ctx, sink):
    B, L = q.shape[:2]
    nb = L // BLOCK
    pad = ((0, 0), (BLOCK, BLOCK), (0, 0), (0, 0))
    kp, vp = jnp.pad(k, pad), jnp.pad(v, pad)
    qb = q.reshape(B, nb, BLOCK, KV_HEADS, Q_GROUP, HEAD_DIM).swapaxes(0, 1)
    a_idx = jnp.arange(BLOCK)[:, None]
    b_idx = jnp.arange(3 * BLOCK)[None, :]
    band_ok = jnp.abs(b_idx - BLOCK - a_idx) <= WINDOW
    ctx_bias = jnp.zeros((BLOCK, k_ctx.shape[1]), f32)

    def one_block(args):
        qi, n = args
        start = n * BLOCK
        kb = lax.dynamic_slice_in_dim(kp, start, 3 * BLOCK, axis=1)
        vb = lax.dynamic_slice_in_dim(vp, start, 3 * BLOCK, axis=1)
        ppos = start + b_idx
        ok = band_ok & (ppos >= BLOCK) & (ppos < L + BLOCK)
        bias = jnp.concatenate([jnp.where(ok, 0.0, NEG).astype(f32), ctx_bias], axis=1)
        keys = jnp.concatenate([kb, k_ctx.astype(kb.dtype)], axis=1)
        vals = jnp.concatenate([vb, v_ctx.astype(vb.dtype)], axis=1)
        return sink_attend(qi, keys, vals, bias, sink)

    ob = lax.map(one_block, (qb, jnp.arange(nb)))
    return ob.swapaxes(0, 1).reshape(B, L, ATTN_WIDTH)


def short_conv(x, w, bias):
    L = x.shape[1]
    xp = jnp.pad(x, ((0, 0), (CONV_LEFT, CONV_W - 1 - CONV_LEFT), (0, 0)))
    out = bias
    for j in range(CONV_W):
        out = out + xp[:, j:j + L] * w[j]
    return out


def block_diag(x, w):
    B, L, _ = x.shape
    y = jnp.einsum('blnc,ncd->blnd', x.reshape(B, L, LRU_BLOCKS, LRU_BW), w.astype(f32))
    return y.reshape(B, L, LRU_WIDTH)


def rglru_coeffs(x, wa, ba, wi, bi, lam):
    xf = x.astype(f32)
    r = jax.nn.sigmoid(block_diag(xf, wa) + ba.astype(f32))
    i = jax.nn.sigmoid(block_diag(xf, wi) + bi.astype(f32))
    log_a = -LRU_C * r * jax.nn.softplus(-lam.astype(f32))
    a = jnp.exp(log_a)
    b = jnp.sqrt(-jnp.expm1(2.0 * log_a)) * (i * xf)
    return a, b


def linear_scan(a, b, h0, reverse):
    edge = -1 if reverse else 0
    b = b.at[:, edge].add(a[:, edge] * h0)

    def combine(e1, e2):
        a1, b1 = e1
        a2, b2 = e2
        return a1 * a2, a2 * b1 + b2

    _, h = lax.associative_scan(combine, (a, b), reverse=reverse, axis=1)
    return h


def bidir_rglru(x, wa, ba, wi, bi, lam, h0):
    af, bf = rglru_coeffs(x, wa[0], ba[0], wi[0], bi[0], lam[0])
    h_fwd = linear_scan(af, bf, h0[:, 0].astype(f32), False)
    ab, bb = rglru_coeffs(x, wa[1], ba[1], wi[1], bi[1], lam[1])
    h_bwd = linear_scan(ab, bb, h0[:, 1].astype(f32), True)
    return h_fwd, h_bwd


def fourier_mix(x):
    B, L, _ = x.shape
    xf = x.astype(f32).reshape(B, L, FOURIER_GROUPS, FOURIER_GW)
    y = jnp.fft.fft2(xf, axes=(1, 3), norm="ortho").real
    return y.reshape(B, L, FOURIER_WIDTH).astype(x.dtype)


def token_mixer(h, P, rope, ctx):
    B, L, _ = h.shape
    q, k, v, xr, yr, xf = jnp.split(h @ P["w_in"], SPLITS, axis=-1)
    q = q.reshape(B, L, N_HEADS, HEAD_DIM)
    k = k.reshape(B, L, KV_HEADS, HEAD_DIM)
    v = v.reshape(B, L, KV_HEADS, HEAD_DIM)
    sink = P["attn_sink"].reshape(KV_HEADS, Q_GROUP)
    xc = short_conv(xr, P["conv_w"], P["conv_b"])
    if ctx is None:
        attn = context_attention(q, k, v, sink)
        h0 = jnp.zeros((B, 2, LRU_WIDTH), f32)
    else:
        k_ctx, v_ctx, h0 = ctx
        cos, sin = rope
        q = apply_rope(q, cos, sin)
        k = apply_rope(k, cos, sin)
        attn = latent_attention(q, k, v, k_ctx, v_ctx, sink)
    h_fwd, h_bwd = bidir_rglru(xc, P["lru_wa"], P["lru_ba"], P["lru_wi"], P["lru_bi"], P["lru_lambda"], h0)
    rec = ((h_fwd + h_bwd) * jax.nn.gelu(yr.astype(f32))).astype(h.dtype)
    four = fourier_mix(xf)
    g = jax.nn.sigmoid(h @ P["w_branch_gate"] + P["b_branch_gate"])
    ga, gr, gf = jnp.split(g, N_BRANCH, axis=-1)
    merged = (ga * (attn @ P["w_attn_out"]) + gr * (rec @ P["w_lru_out"])
              + gf * (four @ P["w_fourier_out"]))
    out = merged @ P["w_o"]
    if ctx is None:
        return out, (k, v, jnp.stack([h_fwd[:, -1], h_bwd[:, 0]], axis=1))
    return out, None


def trunk_layer(x, cond, P, rope, ctx):
    M = cond.shape[0]
    mod = (jax.nn.silu(cond) @ P["w_ada"] + P["b_ada"]).reshape(M, 1, N_SUB, 3, D_MODEL)

    def modnorm(z, i):
        return rmsnorm(z, P["norm_w"][i]) * (1.0 + mod[:, :, i, 1]) + mod[:, :, i, 0]

    x = x + 0.5 * mod[:, :, 0, 2] * swiglu(modnorm(x, 0), P["ffn1_wg"], P["ffn1_wu"], P["ffn1_wd"])
    mix, ctx_out = token_mixer(modnorm(x, 1), P, rope, ctx)
    x = x + mod[:, :, 1, 2] * mix
    x = x + 0.5 * mod[:, :, 2, 2] * swiglu(modnorm(x, 2), P["ffn2_wg"], P["ffn2_wu"], P["ffn2_wd"])
    return x, ctx_out


def setup_inputs(seed: int = 0) -> dict:
    key = jax.random.key(seed)
    keys = iter(jax.random.split(key, 48))

    def nrm(shape, scale):
        return jax.random.normal(next(keys), shape, f32) * scale

    D = D_MODEL
    lo, hi = 0.9 ** (1.0 / LRU_C), 0.999 ** (1.0 / LRU_C)
    a0 = jax.random.uniform(next(keys), (DEPTH, 2, LRU_WIDTH), f32, minval=lo, maxval=hi)
    return {
        "x_prompt": nrm((BATCH, SEQ, D), 1.0),
        "x_sample": nrm((DEC_BATCH, DEC_SEQ, D), 1.0),
        "c": nrm((DEC_BATCH, D), 1.0),
        "cache_k": nrm((DEC_BATCH, DEPTH, PAST_LEN, KV_HEADS, HEAD_DIM), 1.0),
        "cache_v": nrm((DEC_BATCH, DEPTH, PAST_LEN, KV_HEADS, HEAD_DIM), 1.0),
        "state_lru": nrm((DEC_BATCH, DEPTH, 2, LRU_WIDTH), 0.5),
        "c_ctx": nrm((D,), 1.0),
        "w_ada": nrm((DEPTH, D, N_SUB * 3 * D), 0.5 * D ** -0.5),
        "b_ada": nrm((DEPTH, N_SUB * 3 * D), 0.01),
        "norm_w": 1.0 + nrm((DEPTH, N_SUB, D), 0.01),
        "final_norm_w": 1.0 + nrm((D,), 0.01),
        "ffn1_wg": nrm((DEPTH, D, D_FF), D ** -0.5),
        "ffn1_wu": nrm((DEPTH, D, D_FF), D ** -0.5),
        "ffn1_wd": nrm((DEPTH, D_FF, D), D_FF ** -0.5),
        "ffn2_wg": nrm((DEPTH, D, D_FF), D ** -0.5),
        "ffn2_wu": nrm((DEPTH, D, D_FF), D ** -0.5),
        "ffn2_wd": nrm((DEPTH, D_FF, D), D_FF ** -0.5),
        "w_in": nrm((DEPTH, D, IN_WIDTH), D ** -0.5),
        "w_branch_gate": nrm((DEPTH, D, N_BRANCH * D), D ** -0.5),
        "b_branch_gate": nrm((DEPTH, N_BRANCH * D), 0.01),
        "attn_sink": nrm((DEPTH, N_HEADS), 0.5),
        "w_attn_out": nrm((DEPTH, ATTN_WIDTH, D), ATTN_WIDTH ** -0.5),
        "conv_w": nrm((DEPTH, CONV_W, LRU_WIDTH), CONV_W ** -0.5),
        "conv_b": nrm((DEPTH, LRU_WIDTH), 0.01),
        "lru_wa": nrm((DEPTH, 2, LRU_BLOCKS, LRU_BW, LRU_BW), LRU_BW ** -0.5),
        "lru_ba": nrm((DEPTH, 2, LRU_WIDTH), 0.01),
        "lru_wi": nrm((DEPTH, 2, LRU_BLOCKS, LRU_BW, LRU_BW), LRU_BW ** -0.5),
        "lru_bi": nrm((DEPTH, 2, LRU_WIDTH), 0.01),
        "lru_lambda": jnp.log(a0) - jnp.log1p(-a0),
        "w_lru_out": nrm((DEPTH, LRU_WIDTH, D), LRU_WIDTH ** -0.5),
        "w_fourier_out": nrm((DEPTH, FOURIER_WIDTH, D), FOURIER_WIDTH ** -0.5),
        "w_o": nrm((DEPTH, D, D), D ** -0.5),
    }


def reference(x_prompt, x_sample, c, cache_k, cache_v, state_lru, c_ctx,
              w_ada, b_ada, norm_w, final_norm_w,
              ffn1_wg, ffn1_wu, ffn1_wd, ffn2_wg, ffn2_wu, ffn2_wd,
              w_in, w_branch_gate, b_branch_gate, attn_sink, w_attn_out,
              conv_w, conv_b, lru_wa, lru_ba, lru_wi, lru_bi, lru_lambda,
              w_lru_out, w_fourier_out, w_o):
    rope = axial_rope_tables(x_sample.shape[1])
    cond_ctx = c_ctx[None, :]
    xp, xs = x_prompt, x_sample
    ks, vs, ss = [], [], []
    for l in range(DEPTH):
        P = dict(w_ada=w_ada[l], b_ada=b_ada[l], norm_w=norm_w[l],
                 ffn1_wg=ffn1_wg[l], ffn1_wu=ffn1_wu[l], ffn1_wd=ffn1_wd[l],
                 ffn2_wg=ffn2_wg[l], ffn2_wu=ffn2_wu[l], ffn2_wd=ffn2_wd[l],
                 w_in=w_in[l], w_branch_gate=w_branch_gate[l], b_branch_gate=b_branch_gate[l],
                 attn_sink=attn_sink[l], w_attn_out=w_attn_out[l],
                 conv_w=conv_w[l], conv_b=conv_b[l],
                 lru_wa=lru_wa[l], lru_ba=lru_ba[l], lru_wi=lru_wi[l], lru_bi=lru_bi[l],
                 lru_lambda=lru_lambda[l], w_lru_out=w_lru_out[l],
                 w_fourier_out=w_fourier_out[l], w_o=w_o[l])
        xp, (k_l, v_l, s_l) = trunk_layer(xp, cond_ctx, P, None, None)
        ks.append(k_l)
        vs.append(v_l)
        ss.append(s_l)
        xs, _ = trunk_layer(xs, c, P, rope, (cache_k[:, l], cache_v[:, l], state_lru[:, l]))
    y_prompt = rmsnorm(xp, final_norm_w)
    y_sample = rmsnorm(xs, final_norm_w)
    new_cache_k = jnp.stack(ks, axis=1)
    new_cache_v = jnp.stack(vs, axis=1)
    new_state_lru = jnp.stack(ss, axis=1)
    return (y_prompt, y_sample, new_cache_k, new_cache_v, new_state_lru)
```

```python
import functools
import math

import numpy as np
import jax
import jax.numpy as jnp
from jax import lax
from jax.experimental import pallas as pl
from jax.experimental.pallas import tpu as pltpu

f32 = jnp.float32
bf16 = jnp.bfloat16

D_MODEL = 1024
DEPTH = 2
GRID_W = 64
N_HEADS = 8
KV_HEADS = 2
HEAD_DIM = 64
Q_GROUP = N_HEADS // KV_HEADS
ATTN_WIDTH = N_HEADS * HEAD_DIM
KV_WIDTH = KV_HEADS * HEAD_DIM
WINDOW = 128
BLOCK = 128
AXIS_ROT = HEAD_DIM // 2
ROPE_BASE = 10000.0
LRU_WIDTH = 512
LRU_BLOCKS = 8
LRU_BW = LRU_WIDTH // LRU_BLOCKS
LRU_C = 8.0
CONV_W = 4
CONV_LEFT = 2
FOURIER_WIDTH = 512
FOURIER_GROUPS = 4
FOURIER_GW = FOURIER_WIDTH // FOURIER_GROUPS
D_FF = 2816
N_BRANCH = 3
N_SUB = 3
EPS = 1e-6
NEG = -1e30
IN_WIDTH = ATTN_WIDTH + 2 * KV_WIDTH + 2 * LRU_WIDTH + FOURIER_WIDTH

V7X_VMEM_BYTES = 64 * 1024 * 1024
SUBLANES = 8
LANES = 128
COND_ROWS = 8
LRU_HALF = LRU_WIDTH // 2
LRU_SEGMENTS = SUBLANES
PAD_ROWS = SUBLANES

_MIB = 1024 * 1024


def _cparams(sem, vmem_mib):
    return pltpu.CompilerParams(dimension_semantics=sem,
                                vmem_limit_bytes=min(vmem_mib * _MIB, V7X_VMEM_BYTES - 8 * _MIB))


def _dot(a, b):
    return jnp.dot(a, b, preferred_element_type=f32)


def _sigmoid(x):
    return jax.nn.sigmoid(x)


def _modnorm(x, nw, shift, scale):
    y = x * lax.rsqrt(jnp.mean(x * x, axis=-1, keepdims=True) + EPS)
    return (y * nw) * (1.0 + scale) + shift


def _ada_kernel(cond_ref, w_ref, b_ref, o_ref):
    c = cond_ref[...]
    s = (c * _sigmoid(c)).astype(bf16)
    o_ref[...] = _dot(s, w_ref[...].astype(bf16)) + b_ref[...]


def _ada_call(cond, w_ada, b_ada):
    n_out = w_ada.shape[-1]
    tn = 1024
    return pl.pallas_call(
        _ada_kernel,
        out_shape=jax.ShapeDtypeStruct((DEPTH, COND_ROWS, n_out), f32),
        grid=(DEPTH, n_out // tn),
        in_specs=[
            pl.BlockSpec((COND_ROWS, D_MODEL), lambda l, j: (0, 0)),
            pl.BlockSpec((None, D_MODEL, tn), lambda l, j: (l, 0, j)),
            pl.BlockSpec((None, 1, tn), lambda l, j: (l, 0, j)),
        ],
        out_specs=pl.BlockSpec((None, COND_ROWS, tn), lambda l, j: (l, 0, j)),
        compiler_params=_cparams(("parallel", "parallel"), 24),
        name="adaln",
    )(cond, w_ada, b_ada.reshape(DEPTH, 1, n_out))


def _mod_spec(layer, slot, row_fn):
    return pl.BlockSpec((None, None, None, 1, D_MODEL),
                        lambda i, *_: (layer, slot, row_fn(i), 0, 0))


def _vec_spec(layer, sub):
    return pl.BlockSpec((None, None, 1, D_MODEL), lambda i, *_: (layer, sub, 0, 0))


def _ffn_kernel(*refs, n_ff, final):
    if final:
        (x_ref, sh_ref, sc_ref, gt_ref, nw_ref, wg_ref, wu_ref, wd_ref, fw_ref,
         o_ref, h_ref, acc_ref) = refs
    else:
        (x_ref, sh_ref, sc_ref, gt_ref, nw_ref, wg_ref, wu_ref, wd_ref,
         o_ref, h_ref, acc_ref) = refs
    j = pl.program_id(1)

    @pl.when(j == 0)
    def _():
        h_ref[...] = _modnorm(x_ref[...], nw_ref[...], sh_ref[...], sc_ref[...]).astype(bf16)
        acc_ref[...] = jnp.zeros_like(acc_ref)

    h = h_ref[...]
    g = _dot(h, wg_ref[...].astype(bf16))
    u = _dot(h, wu_ref[...].astype(bf16))
    a = ((g * _sigmoid(g)) * u).astype(bf16)
    acc_ref[...] += _dot(a, wd_ref[...].astype(bf16))

    @pl.when(j == n_ff - 1)
    def _():
        y = x_ref[...] + (0.5 * gt_ref[...]) * acc_ref[...]
        if final:
            y = y * lax.rsqrt(jnp.mean(y * y, axis=-1, keepdims=True) + EPS) * fw_ref[...]
        o_ref[...] = y


def _ffn_call(x, mod5, norm_w4, wg, wu, wd, final_w, *, layer, sub, row_fn, tm, name):
    n_tok = x.shape[0]
    tf = 256
    n_ff = D_FF // tf
    final = final_w is not None
    in_specs = [
        pl.BlockSpec((tm, D_MODEL), lambda i, j: (i, 0)),
        _mod_spec(layer, sub * 3 + 0, row_fn),
        _mod_spec(layer, sub * 3 + 1, row_fn),
        _mod_spec(layer, sub * 3 + 2, row_fn),
        _vec_spec(layer, sub),
        pl.BlockSpec((None, D_MODEL, tf), lambda i, j: (layer, 0, j)),
        pl.BlockSpec((None, D_MODEL, tf), lambda i, j: (layer, 0, j)),
        pl.BlockSpec((None, tf, D_MODEL), lambda i, j: (layer, j, 0)),
    ]
    args = [x, mod5, mod5, mod5, norm_w4, wg, wu, wd]
    if final:
        in_specs.append(pl.BlockSpec((1, D_MODEL), lambda i, j: (0, 0)))
        args.append(final_w)
    return pl.pallas_call(
        functools.partial(_ffn_kernel, n_ff=n_ff, final=final),
        out_shape=jax.ShapeDtypeStruct((n_tok, D_MODEL), f32),
        grid=(n_tok // tm, n_ff),
        in_specs=in_specs,
        out_specs=pl.BlockSpec((tm, D_MODEL), lambda i, j: (i, 0)),
        scratch_shapes=[pltpu.VMEM((tm, D_MODEL), bf16), pltpu.VMEM((tm, D_MODEL), f32)],
        compiler_params=_cparams(("parallel", "arbitrary"), 48),
        name=name,
    )(*args)


def _swap16(x):
    n = x.shape[-1]
    lane = lax.broadcasted_iota(jnp.int32, x.shape, x.ndim - 1)
    first = (lane % AXIS_ROT) < (AXIS_ROT // 2)
    return jnp.where(first, pltpu.roll(x, n - AXIS_ROT // 2, x.ndim - 1),
                     pltpu.roll(x, AXIS_ROT // 2, x.ndim - 1))


def _inproj_kernel(*refs, rope):
    if rope:
        (x_ref, sh_ref, sc_ref, nw_ref, w_ref, cos_ref, sin_ref,
         q_ref, k_ref, v_ref, xr_ref, yr_ref, xf_ref) = refs
    else:
        (x_ref, sh_ref, sc_ref, nw_ref, w_ref,
         q_ref, k_ref, v_ref, xr_ref, yr_ref, xf_ref) = refs
    h = _modnorm(x_ref[...], nw_ref[...], sh_ref[...], sc_ref[...]).astype(bf16)
    o = ATTN_WIDTH
    k = _dot(h, w_ref[:, o:o + KV_WIDTH])
    if rope:
        cos = cos_ref[...]
        sin = sin_ref[...]
        k = k * cos + _swap16(k) * sin
    k_ref[...] = k
    for c in range(ATTN_WIDTH // LANES):
        q = _dot(h, w_ref[:, c * LANES:(c + 1) * LANES])
        if rope:
            q = q * cos + _swap16(q) * sin
        q_ref[:, c * LANES:(c + 1) * LANES] = q.astype(bf16)
    o += KV_WIDTH
    v_ref[...] = _dot(h, w_ref[:, o:o + KV_WIDTH])
    o += KV_WIDTH
    xr_ref[...] = _dot(h, w_ref[:, o:o + LRU_WIDTH])
    o += LRU_WIDTH
    yr_ref[...] = _dot(h, w_ref[:, o:o + LRU_WIDTH])
    o += LRU_WIDTH
    xf_ref[...] = _dot(h, w_ref[:, o:o + FOURIER_WIDTH])


def _inproj_call(x, mod5, norm_w4, w_in_bf, rope_tabs, *, layer, row_fn, tm, seq, name):
    n_tok = x.shape[0]
    rope = rope_tabs is not None
    in_specs = [
        pl.BlockSpec((tm, D_MODEL), lambda i: (i, 0)),
        _mod_spec(layer, 3, row_fn),
        _mod_spec(layer, 4, row_fn),
        _vec_spec(layer, 1),
        pl.BlockSpec((None, D_MODEL, IN_WIDTH), lambda i: (layer, 0, 0)),
    ]
    args = [x, mod5, mod5, norm_w4, w_in_bf]
    if rope:
        per_seq = seq // tm
        in_specs += [pl.BlockSpec((tm, LANES), lambda i: (i % per_seq, 0))] * 2
        args += list(rope_tabs)
    widths = (ATTN_WIDTH, KV_WIDTH, KV_WIDTH, LRU_WIDTH, LRU_WIDTH, FOURIER_WIDTH)
    dtypes = (bf16, f32, f32, f32, f32, f32)
    return pl.pallas_call(
        functools.partial(_inproj_kernel, rope=rope),
        out_shape=[jax.ShapeDtypeStruct((n_tok, w), d) for w, d in zip(widths, dtypes)],
        grid=(n_tok // tm,),
        in_specs=in_specs,
        out_specs=[pl.BlockSpec((tm, w), lambda i: (i, 0)) for w in widths],
        compiler_params=_cparams(("parallel",), 48),
        name=name,
    )(*args)


def _sink_heads(sink_ref, q, keys, vals, bias, o_ref):
    prev = None
    for n in range(N_HEADS):
        kvh = n // Q_GROUP
        qh = q[:, n * HEAD_DIM:(n + 1) * HEAD_DIM]
        kh = keys[:, kvh * HEAD_DIM:(kvh + 1) * HEAD_DIM]
        vh = vals[:, kvh * HEAD_DIM:(kvh + 1) * HEAD_DIM]
        s = lax.dot_general(qh, kh, (((1,), (1,)), ((), ())),
                            preferred_element_type=f32) * (HEAD_DIM ** -0.5)
        if bias is not None:
            s = s + bias
        sk = sink_ref[n]
        m = jnp.maximum(jnp.max(s, axis=-1, keepdims=True), sk)
        p = jnp.exp(s - m)
        denom = jnp.sum(p, axis=-1, keepdims=True) + jnp.exp(sk - m)
        o = _dot((p / denom).astype(bf16), vh)
        if n % 2 == 0:
            prev = o
        else:
            pair = jnp.concatenate([prev, o], axis=1)
            o_ref[:, (n - 1) * HEAD_DIM:(n + 1) * HEAD_DIM] = pair.astype(o_ref.dtype)


def _ctx_attn_kernel(sink_ref, q_ref, k_ref, v_ref, o_ref):
    _sink_heads(sink_ref, q_ref[...], k_ref[...].astype(bf16), v_ref[...].astype(bf16), None, o_ref)


def _ctx_attn_call(sink, q, k, v, *, batch, seq):
    return pl.pallas_call(
        _ctx_attn_kernel,
        out_shape=jax.ShapeDtypeStruct((batch * seq, ATTN_WIDTH), bf16),
        grid=(batch,),
        in_specs=[
            pl.BlockSpec(memory_space=pltpu.SMEM),
            pl.BlockSpec((seq, ATTN_WIDTH), lambda b: (b, 0)),
            pl.BlockSpec((seq, KV_WIDTH), lambda b: (b, 0)),
            pl.BlockSpec((seq, KV_WIDTH), lambda b: (b, 0)),
        ],
        out_specs=pl.BlockSpec((seq, ATTN_WIDTH), lambda b: (b, 0)),
        compiler_params=_cparams(("parallel",), 32),
        name="ctx_attn",
    )(sink, q, k, v)


def _lat_attn_kernel(sink_ref, q_ref, kp_ref, kc_ref, kn_ref, vp_ref, vc_ref, vn_ref,
                     kx_ref, vx_ref, o_ref, *, n_blocks, past):
    n = pl.program_id(1)
    keys = jnp.concatenate([kp_ref[...], kc_ref[...], kn_ref[...], kx_ref[...]], axis=0).astype(bf16)
    vals = jnp.concatenate([vp_ref[...], vc_ref[...], vn_ref[...], vx_ref[...]], axis=0).astype(bf16)
    a_idx = lax.broadcasted_iota(jnp.int32, (BLOCK, 3 * BLOCK + past), 0)
    b_idx = lax.broadcasted_iota(jnp.int32, (BLOCK, 3 * BLOCK + past), 1)
    pos = (n - 1) * BLOCK + b_idx
    ok = (jnp.abs(b_idx - BLOCK - a_idx) <= WINDOW) & (pos >= 0) & (pos < n_blocks * BLOCK)
    ok = ok | (b_idx >= 3 * BLOCK)
    bias = jnp.where(ok, 0.0, NEG).astype(f32)
    _sink_heads(sink_ref, q_ref[...], keys, vals, bias, o_ref)


def _lat_attn_call(sink, q, k, v, cache_k4, cache_v4, *, layer, batch, seq):
    nb = seq // BLOCK
    past = cache_k4.shape[2]

    def blk(off):
        return pl.BlockSpec((BLOCK, KV_WIDTH),
                            lambda b, n: (b * nb + jnp.clip(n + off, 0, nb - 1), 0))

    ctx = pl.BlockSpec((None, None, past, KV_WIDTH), lambda b, n: (b, layer, 0, 0))
    return pl.pallas_call(
        functools.partial(_lat_attn_kernel, n_blocks=nb, past=past),
        out_shape=jax.ShapeDtypeStruct((batch * seq, ATTN_WIDTH), bf16),
        grid=(batch, nb),
        in_specs=[
            pl.BlockSpec(memory_space=pltpu.SMEM),
            pl.BlockSpec((BLOCK, ATTN_WIDTH), lambda b, n: (b * nb + n, 0)),
            blk(-1), blk(0), blk(1), blk(-1), blk(0), blk(1), ctx, ctx,
        ],
        out_specs=pl.BlockSpec((BLOCK, ATTN_WIDTH), lambda b, n: (b * nb + n, 0)),
        compiler_params=_cparams(("parallel", "parallel"), 32),
        name="lat_attn",
    )(sink, q, k, k, k, v, v, v, cache_k4, cache_v4)


def _softplus(x):
    return jnp.maximum(x, 0.0) + jnp.log1p(jnp.exp(-jnp.abs(x)))


def _gelu_tanh(x):
    return 0.5 * x * (1.0 + jnp.tanh(math.sqrt(2.0 / math.pi) * (x + 0.044715 * (x * x * x))))


def _lru_kernel(xr_ref, yr_ref, cw_ref, cb_ref, wd_ref, bias_ref, lam_ref, h0_ref,
                rec_ref, st_ref, xs_ref, af_ref, bf_ref, ab_ref, bb_ref, *, seq):
    seg = seq // LRU_SEGMENTS
    chunk = min(seq, 256)
    hw = LRU_HALF
    n_lt = hw // LANES

    zeros_pad = jnp.zeros((PAD_ROWS, hw), f32)
    xs_ref[0:PAD_ROWS, :] = zeros_pad
    xs_ref[PAD_ROWS + seq:2 * PAD_ROWS + seq, :] = zeros_pad
    for c in range(seq // chunk):
        xs_ref[PAD_ROWS + c * chunk:PAD_ROWS + (c + 1) * chunk, :] = xr_ref[c * chunk:(c + 1) * chunk, :]

    neg_c_sp = [-LRU_C * _softplus(-lam_ref[d:d + 1, :]) for d in range(2)]
    a_refs = (af_ref, ab_ref)
    b_refs = (bf_ref, bb_ref)

    for c in range(seq // chunk):
        r0 = c * chunk
        xc = cb_ref[...]
        for j in range(CONV_W):
            s0 = PAD_ROWS + r0 + j - CONV_LEFT
            xc = xc + xs_ref[s0:s0 + chunk, :] * cw_ref[j:j + 1, :]
        y = _dot(xc.astype(bf16), wd_ref[...]) + bias_ref[...]
        for d in range(2):
            r = _sigmoid(y[:, (2 * d) * hw:(2 * d + 1) * hw])
            gi = _sigmoid(y[:, (2 * d + 1) * hw:(2 * d + 2) * hw])
            log_a = neg_c_sp[d] * r
            a = jnp.exp(log_a)
            b = jnp.sqrt(-jnp.tanh(log_a) * (1.0 + a * a)) * (gi * xc)
            for t in range(n_lt):
                a_refs[d][t, r0:r0 + chunk, :] = a[:, t * LANES:(t + 1) * LANES]
                b_refs[d][t, r0:r0 + chunk, :] = b[:, t * LANES:(t + 1) * LANES]

    def step(j, carry):
        jf = pl.ds(j, LRU_SEGMENTS, stride=seg)
        jb = pl.ds(seg - 1 - j, LRU_SEGMENTS, stride=seg)
        out = []
        for t in range(n_lt):
            hf, pf, hb, pb = carry[4 * t:4 * t + 4]
            a = af_ref[t, jf, :]
            hf = a * hf + bf_ref[t, jf, :]
            pf = a * pf
            bf_ref[t, jf, :] = hf
            af_ref[t, jf, :] = pf
            a = ab_ref[t, jb, :]
            hb = a * hb + bb_ref[t, jb, :]
            pb = a * pb
            bb_ref[t, jb, :] = hb
            ab_ref[t, jb, :] = pb
            out += [hf, pf, hb, pb]
        return tuple(out)

    z = jnp.zeros((LRU_SEGMENTS, LANES), f32)
    o = jnp.ones((LRU_SEGMENTS, LANES), f32)
    fin = lax.fori_loop(0, seg, step, (z, o, z, o) * n_lt)

    sub = min(seg, 256)
    for t in range(n_lt):
        hf, pf, hb, pb = fin[4 * t:4 * t + 4]
        lanes = slice(t * LANES, (t + 1) * LANES)
        cf = h0_ref[0:1, lanes]
        carry_f = []
        for s in range(LRU_SEGMENTS):
            carry_f.append(cf)
            cf = hf[s:s + 1, :] + pf[s:s + 1, :] * cf
        cb = h0_ref[1:2, lanes]
        carry_b = [None] * LRU_SEGMENTS
        for s in reversed(range(LRU_SEGMENTS)):
            carry_b[s] = cb
            cb = hb[s:s + 1, :] + pb[s:s + 1, :] * cb
        st_ref[0:1, lanes] = cf
        st_ref[1:2, lanes] = cb
        for s in range(LRU_SEGMENTS):
            for u in range(seg // sub):
                rows = slice(s * seg + u * sub, s * seg + (u + 1) * sub)
                h = ((bf_ref[t, rows, :] + af_ref[t, rows, :] * carry_f[s])
                     + (bb_ref[t, rows, :] + ab_ref[t, rows, :] * carry_b[s]))
                rec_ref[rows, lanes] = (h * _gelu_tanh(yr_ref[rows, lanes])).astype(rec_ref.dtype)


def _lru_call(xr, yr, conv_w, conv_b, wd, bias, lam, h0, *, layer, batch, seq):
    hw = LRU_HALF
    return pl.pallas_call(
        functools.partial(_lru_kernel, seq=seq),
        out_shape=[jax.ShapeDtypeStruct((batch * seq, LRU_WIDTH), bf16),
                   jax.ShapeDtypeStruct((batch, 2, LRU_WIDTH), f32)],
        grid=(batch, 2),
        in_specs=[
            pl.BlockSpec((seq, hw), lambda b, c: (b, c)),
            pl.BlockSpec((seq, hw), lambda b, c: (b, c)),
            pl.BlockSpec((None, CONV_W, hw), lambda b, c: (layer, 0, c)),
            pl.BlockSpec((None, 1, hw), lambda b, c: (layer, 0, c)),
            pl.BlockSpec((None, None, hw, 4 * hw), lambda b, c: (layer, c, 0, 0)),
            pl.BlockSpec((None, None, 1, 4 * hw), lambda b, c: (layer, c, 0, 0)),
            pl.BlockSpec((None, 2, hw), lambda b, c: (layer, 0, c)),
            pl.BlockSpec((None, 2, hw), lambda b, c: (b, 0, c)),
        ],
        out_specs=[pl.BlockSpec((seq, hw), lambda b, c: (b, c)),
                   pl.BlockSpec((None, 2, hw), lambda b, c: (b, 0, c))],
        scratch_shapes=([pltpu.VMEM((seq + 2 * PAD_ROWS, hw), f32)]
                        + [pltpu.VMEM((hw // LANES, seq, LANES), f32)] * 4),
        compiler_params=_cparams(("parallel", "parallel"), 40),
        name=f"lru_{seq}",
    )(xr, yr, conv_w, conv_b, wd, bias, lam, h0)


def _fourier_kernel(x_ref, csc_ref, csl_ref, o_ref, u_ref, *, seq, scale):
    i = pl.program_id(1)

    @pl.when(i == 0)
    def _():
        chunk = min(seq, 512)
        for c in range(seq // chunk):
            rows = slice(c * chunk, (c + 1) * chunk)
            for g in range(FOURIER_GROUPS):
                cols = slice(g * FOURIER_GW, (g + 1) * FOURIER_GW)
                u = _dot(x_ref[rows, cols].astype(bf16), csc_ref[...])
                u_ref[c * chunk:(c + 1) * chunk, cols] = u[:, :FOURIER_GW].astype(bf16)
                u_ref[seq + c * chunk:seq + (c + 1) * chunk, cols] = u[:, FOURIER_GW:].astype(bf16)

    o_ref[...] = (_dot(csl_ref[...], u_ref[...]) * scale).astype(o_ref.dtype)


def _dft_tables(n):
    k = np.arange(n, dtype=np.int64)
    ang = (2.0 * np.pi / n) * ((k[:, None] * k[None, :]) % n).astype(np.float64)
    return np.cos(ang), np.sin(ang)


def _fourier_call(xf, csc, csl, *, batch, seq):
    tl = min(seq, 512)
    nt = seq // tl
    scale = 1.0 / math.sqrt(seq * FOURIER_GW)
    return pl.pallas_call(
        functools.partial(_fourier_kernel, seq=seq, scale=scale),
        out_shape=jax.ShapeDtypeStruct((batch * seq, FOURIER_WIDTH), bf16),
        grid=(batch, nt),
        in_specs=[
            pl.BlockSpec((seq, FOURIER_WIDTH), lambda b, i: (b, 0)),
            pl.BlockSpec((FOURIER_GW, 2 * FOURIER_GW), lambda b, i: (0, 0)),
            pl.BlockSpec((tl, 2 * seq), lambda b, i: (i, 0)),
        ],
        out_specs=pl.BlockSpec((tl, FOURIER_WIDTH), lambda b, i: (b * nt + i, 0)),
        scratch_shapes=[pltpu.VMEM((2 * seq, FOURIER_WIDTH), bf16)],
        compiler_params=_cparams(("parallel", "arbitrary"), 40),
        name=f"fourier_{seq}",
    )(xf, csc, csl)


def _merge_kernel(x_ref, sh_ref, sc_ref, gt_ref, nw_ref, at_ref, rc_ref, fr_ref,
                  wbg_ref, bbg_ref, wao_ref, wlo_ref, wfo_ref, wo_ref, o_ref):
    x = x_ref[...]
    h = _modnorm(x, nw_ref[...], sh_ref[...], sc_ref[...]).astype(bf16)
    merged = None
    for idx, (br_ref, w_ref) in enumerate(((at_ref, wao_ref), (rc_ref, wlo_ref), (fr_ref, wfo_ref))):
        cols = slice(idx * D_MODEL, (idx + 1) * D_MODEL)
        g = _sigmoid(_dot(h, wbg_ref[:, cols]) + bbg_ref[:, cols])
        y = g * _dot(br_ref[...], w_ref[...])
        merged = y if merged is None else merged + y
    out = _dot(merged.astype(bf16), wo_ref[...])
    o_ref[...] = x + gt_ref[...] * out


def _merge_call(x, mod5, norm_w4, attn, rec, four, wbg, bbg, wao, wlo, wfo, wo, *, layer, row_fn, tm, name):
    n_tok = x.shape[0]

    def const(shape):
        nd = len(shape)
        return pl.BlockSpec((None,) + shape, lambda i: (layer,) + (0,) * nd,
                            pipeline_mode=pl.Buffered(1))

    def branch():
        return pl.BlockSpec((tm, ATTN_WIDTH), lambda i: (i, 0))

    return pl.pallas_call(
        _merge_kernel,
        out_shape=jax.ShapeDtypeStruct((n_tok, D_MODEL), f32),
        grid=(n_tok // tm,),
        in_specs=[
            pl.BlockSpec((tm, D_MODEL), lambda i: (i, 0)),
            _mod_spec(layer, 3, row_fn),
            _mod_spec(layer, 4, row_fn),
            _mod_spec(layer, 5, row_fn),
            _vec_spec(layer, 1),
            branch(), branch(), branch(),
            const((D_MODEL, N_BRANCH * D_MODEL)),
            const((1, N_BRANCH * D_MODEL)),
            const((ATTN_WIDTH, D_MODEL)),
            const((LRU_WIDTH, D_MODEL)),
            const((FOURIER_WIDTH, D_MODEL)),
            const((D_MODEL, D_MODEL)),
        ],
        out_specs=pl.BlockSpec((tm, D_MODEL), lambda i: (i, 0)),
        compiler_params=_cparams(("parallel",), 48),
        name=name,
    )(x, mod5, mod5, mod5, norm_w4, attn, rec, four, wbg, bbg, wao, wlo, wfo, wo)


def _rope_lane_tables(n_tokens):
    rows = n_tokens // GRID_W
    row = jnp.repeat(jnp.arange(rows), GRID_W).astype(f32)
    col = jnp.tile(jnp.arange(GRID_W), rows).astype(f32)
    inv = ROPE_BASE ** (-jnp.arange(0, AXIS_ROT, 2, dtype=f32) / AXIS_ROT)
    ang = jnp.stack([row[:, None] * inv, col[:, None] * inv], axis=1)
    cos, sin = jnp.cos(ang), jnp.sin(ang)
    cos_h = jnp.stack([cos, cos], axis=2).reshape(n_tokens, HEAD_DIM)
    sin_h = jnp.stack([-sin, sin], axis=2).reshape(n_tokens, HEAD_DIM)
    reps = LANES // HEAD_DIM
    return jnp.tile(cos_h, (1, reps)), jnp.tile(sin_h, (1, reps))


def _lru_dense_weights(lru_wa, lru_wi, lru_ba, lru_bi):
    eye = jnp.eye(LRU_BLOCKS, dtype=f32)

    def dense(w):
        d = w[:, :, :, :, None, :] * eye[None, None, :, None, :, None]
        return d.reshape(DEPTH, 2, LRU_WIDTH, LRU_WIDTH)

    da, di = dense(lru_wa), dense(lru_wi)
    hw = LRU_HALF
    halves, biases = [], []
    for c in range(2):
        sl = slice(c * hw, (c + 1) * hw)
        halves.append(jnp.concatenate(
            [da[:, 0, sl, sl], di[:, 0, sl, sl], da[:, 1, sl, sl], di[:, 1, sl, sl]], axis=-1))
        biases.append(jnp.concatenate(
            [lru_ba[:, 0, sl], lru_bi[:, 0, sl], lru_ba[:, 1, sl], lru_bi[:, 1, sl]], axis=-1))
    wd = jnp.stack(halves, axis=1).astype(bf16)
    bias = jnp.stack(biases, axis=1)[:, :, None, :]
    return wd, bias


def kernel(x_prompt, x_sample, c, cache_k, cache_v, state_lru, c_ctx, w_ada, b_ada, norm_w, final_norm_w,
           ffn1_wg, ffn1_wu, ffn1_wd, ffn2_wg, ffn2_wu, ffn2_wd, w_in, w_branch_gate, b_branch_gate,
           attn_sink, w_attn_out, conv_w, conv_b, lru_wa, lru_ba, lru_wi, lru_bi, lru_lambda,
           w_lru_out, w_fourier_out, w_o):
    batch, seq, _ = x_prompt.shape
    dec_batch, dec_seq, _ = x_sample.shape
    past = cache_k.shape[2]
    assert 1 + dec_batch <= COND_ROWS

    cond = jnp.concatenate([c_ctx[None, :], c, jnp.zeros((COND_ROWS - 1 - dec_batch, D_MODEL), f32)], axis=0)
    mod = _ada_call(cond, w_ada, b_ada)
    mod5 = mod.reshape(DEPTH, COND_ROWS, N_SUB * 3, D_MODEL).transpose(0, 2, 1, 3)[:, :, :, None, :]
    norm_w4 = norm_w[:, :, None, :]
    final_w = final_norm_w[None, :]

    w_in_bf = w_in.astype(bf16)
    wbg_bf = w_branch_gate.astype(bf16)
    bbg = b_branch_gate[:, None, :]
    wao_bf = w_attn_out.astype(bf16)
    wlo_bf = w_lru_out.astype(bf16)
    wfo_bf = w_fourier_out.astype(bf16)
    wo_bf = w_o.astype(bf16)
    lru_wd, lru_bias = _lru_dense_weights(lru_wa, lru_wi, lru_ba, lru_bi)
    conv_b3 = conv_b[:, None, :]
    rope_tabs = _rope_lane_tables(dec_seq)
    cc, sc_ = _dft_tables(FOURIER_GW)
    csc = jnp.asarray(np.concatenate([cc, sc_], axis=1), f32).astype(bf16)
    csl = {}
    for n in (seq, dec_seq):
        cl, sl = _dft_tables(n)
        csl[n] = jnp.asarray(np.concatenate([cl, -sl], axis=1), f32).astype(bf16)
    cache_k4 = cache_k.reshape(dec_batch, DEPTH, past, KV_WIDTH)
    cache_v4 = cache_v.reshape(dec_batch, DEPTH, past, KV_WIDTH)
    h0_ctx = jnp.zeros((batch, 2, LRU_WIDTH), f32)

    tm_p, tm_s = 1024, 1024
    row_p = lambda i: 0
    row_s = lambda i: 1 + (i * tm_s) // dec_seq
    tmi = 512
    row_pi = lambda i: 0
    row_si = lambda i: 1 + (i * tmi) // dec_seq

    xp = x_prompt.reshape(batch * seq, D_MODEL)
    xs = x_sample.reshape(dec_batch * dec_seq, D_MODEL)
    ks, vs, ss = [], [], []
    for l in range(DEPTH):
        last = l == DEPTH - 1
        sink = attn_sink[l]
        xp = _ffn_call(xp, mod5, norm_w4, ffn1_wg, ffn1_wu, ffn1_wd, None,
                       layer=l, sub=0, row_fn=row_p, tm=tm_p, name="ffn1_ctx")
        xs = _ffn_call(xs, mod5, norm_w4, ffn1_wg, ffn1_wu, ffn1_wd, None,
                       layer=l, sub=0, row_fn=row_s, tm=tm_s, name="ffn1_lat")
        q, k, v, xr, yr, xf = _inproj_call(xp, mod5, norm_w4, w_in_bf, None,
                                           layer=l, row_fn=row_pi, tm=tmi, seq=seq, name="inproj_ctx")
        ks.append(k.reshape(batch, seq, KV_HEADS, HEAD_DIM))
        vs.append(v.reshape(batch, seq, KV_HEADS, HEAD_DIM))
        attn = _ctx_attn_call(sink, q, k, v, batch=batch, seq=seq)
        rec, st = _lru_call(xr, yr, conv_w, conv_b3, lru_wd, lru_bias, lru_lambda, h0_ctx,
                            layer=l, batch=batch, seq=seq)
        ss.append(st)
        four = _fourier_call(xf, csc, csl[seq], batch=batch, seq=seq)
        xp = _merge_call(xp, mod5, norm_w4, attn, rec, four, wbg_bf, bbg, wao_bf, wlo_bf, wfo_bf, wo_bf,
                         layer=l, row_fn=row_pi, tm=tmi, name="merge_ctx")
        q, k, v, xr, yr, xf = _inproj_call(xs, mod5, norm_w4, w_in_bf, rope_tabs,
                                           layer=l, row_fn=row_si, tm=tmi, seq=dec_seq, name="inproj_lat")
        attn = _lat_attn_call(sink, q, k, v, cache_k4, cache_v4, layer=l, batch=dec_batch, seq=dec_seq)
        rec, _ = _lru_call(xr, yr, conv_w, conv_b3, lru_wd, lru_bias, lru_lambda, state_lru[:, l],
                           layer=l, batch=dec_batch, seq=dec_seq)
        four = _fourier_call(xf, csc, csl[dec_seq], batch=dec_batch, seq=dec_seq)
        xs = _merge_call(xs, mod5, norm_w4, attn, rec, four, wbg_bf, bbg, wao_bf, wlo_bf, wfo_bf, wo_bf,
                         layer=l, row_fn=row_si, tm=tmi, name="merge_lat")
        xp = _ffn_call(xp, mod5, norm_w4, ffn2_wg, ffn2_wu, ffn2_wd, final_w if last else None,
                       layer=l, sub=2, row_fn=row_p, tm=tm_p, name="ffn2_ctx")
        xs = _ffn_call(xs, mod5, norm_w4, ffn2_wg, ffn2_wu, ffn2_wd, final_w if last else None,
                       layer=l, sub=2, row_fn=row_s, tm=tm_s, name="ffn2_lat")

    y_prompt = xp.reshape(batch, seq, D_MODEL)
    y_sample = xs.reshape(dec_batch, dec_seq, D_MODEL)
    return (y_prompt, y_sample, jnp.stack(ks, axis=1), jnp.stack(vs, axis=1), jnp.stack(ss, axis=1))
```

```python
import functools
import math

import numpy as np
import jax
import jax.numpy as jnp
from jax import lax
from jax.experimental import pallas as pl
from jax.experimental.pallas import tpu as pltpu

f32 = jnp.float32
bf16 = jnp.bfloat16

D_MODEL = 1024
DEPTH = 2
GRID_W = 64
N_HEADS = 8
KV_HEADS = 2
HEAD_DIM = 64
Q_GROUP = N_HEADS // KV_HEADS
ATTN_WIDTH = N_HEADS * HEAD_DIM
KV_WIDTH = KV_HEADS * HEAD_DIM
WINDOW = 128
BLOCK = 128
AXIS_ROT = HEAD_DIM // 2
ROPE_BASE = 10000.0
LRU_WIDTH = 512
LRU_BLOCKS = 8
LRU_BW = LRU_WIDTH // LRU_BLOCKS
LRU_C = 8.0
CONV_W = 4
CONV_LEFT = 2
FOURIER_WIDTH = 512
FOURIER_GROUPS = 4
FOURIER_GW = FOURIER_WIDTH // FOURIER_GROUPS
D_FF = 2816
N_BRANCH = 3
N_SUB = 3
EPS = 1e-6
NEG = -1e30
IN_WIDTH = ATTN_WIDTH + 2 * KV_WIDTH + 2 * LRU_WIDTH + FOURIER_WIDTH

V7X_VMEM_BYTES = 64 * 1024 * 1024
SUBLANES = 8
LANES = 128
COND_ROWS = 8
LRU_HALF = LRU_WIDTH // 2
LRU_SEGMENTS = SUBLANES
PAD_ROWS = SUBLANES

_MIB = 1024 * 1024


def _cparams(sem, vmem_mib):
    return pltpu.CompilerParams(dimension_semantics=sem,
                                vmem_limit_bytes=min(vmem_mib * _MIB, V7X_VMEM_BYTES - 8 * _MIB))


def _dot(a, b):
    return jnp.dot(a, b, preferred_element_type=f32)


def _sigmoid(x):
    return jax.nn.sigmoid(x)


def _modnorm(x, nw, shift, scale):
    y = x * lax.rsqrt(jnp.mean(x * x, axis=-1, keepdims=True) + EPS)
    return (y * nw) * (1.0 + scale) + shift


def _ada_kernel(cond_ref, w_ref, b_ref, o_ref):
    c = cond_ref[...]
    s = (c * _sigmoid(c)).astype(bf16)
    o_ref[...] = _dot(s, w_ref[...].astype(bf16)) + b_ref[...]


def _ada_call(cond, w_ada, b_ada):
    n_out = w_ada.shape[-1]
    tn = 1024
    return pl.pallas_call(
        _ada_kernel,
        out_shape=jax.ShapeDtypeStruct((DEPTH, COND_ROWS, n_out), f32),
        grid=(DEPTH, n_out // tn),
        in_specs=[
            pl.BlockSpec((COND_ROWS, D_MODEL), lambda l, j: (0, 0)),
            pl.BlockSpec((None, D_MODEL, tn), lambda l, j: (l, 0, j)),
            pl.BlockSpec((None, 1, tn), lambda l, j: (l, 0, j)),
        ],
        out_specs=pl.BlockSpec((None, COND_ROWS, tn), lambda l, j: (l, 0, j)),
        compiler_params=_cparams(("parallel", "parallel"), 24),
        name="adaln",
    )(cond, w_ada, b_ada.reshape(DEPTH, 1, n_out))


def _mod_spec(layer, slot, row_fn):
    return pl.BlockSpec((None, None, None, 1, D_MODEL),
                        lambda i, *_: (layer, slot, row_fn(i), 0, 0))


def _vec_spec(layer, sub):
    return pl.BlockSpec((None, None, 1, D_MODEL), lambda i, *_: (layer, sub, 0, 0))


def _ffn_kernel(*refs, n_ff, final):
    if final:
        (x_ref, sh_ref, sc_ref, gt_ref, nw_ref, wg_ref, wu_ref, wd_ref, fw_ref,
         o_ref, h_ref) = refs
    else:
        (x_ref, sh_ref, sc_ref, gt_ref, nw_ref, wg_ref, wu_ref, wd_ref,
         o_ref, h_ref) = refs
    j = pl.program_id(1)

    @pl.when(j == 0)
    def _():
        h_ref[...] = _modnorm(x_ref[...], nw_ref[...], sh_ref[...], sc_ref[...]).astype(bf16)
        o_ref[...] = jnp.zeros_like(o_ref)

    h = h_ref[...]
    g = _dot(h, wg_ref[...].astype(bf16))
    u = _dot(h, wu_ref[...].astype(bf16))
    a = ((g * _sigmoid(g)) * u).astype(bf16)
    o_ref[...] += _dot(a, wd_ref[...].astype(bf16))

    @pl.when(j == n_ff - 1)
    def _():
        y = x_ref[...] + (0.5 * gt_ref[...]) * o_ref[...]
        if final:
            y = y * lax.rsqrt(jnp.mean(y * y, axis=-1, keepdims=True) + EPS) * fw_ref[...]
        o_ref[...] = y


def _ffn_call(x, mod5, norm_w4, wg, wu, wd, final_w, *, layer, sub, row_fn, tm, name):
    n_tok = x.shape[0]
    tf = 256
    n_ff = D_FF // tf
    final = final_w is not None
    in_specs = [
        pl.BlockSpec((tm, D_MODEL), lambda i, j: (i, 0), pipeline_mode=pl.Buffered(1)),
        _mod_spec(layer, sub * 3 + 0, row_fn),
        _mod_spec(layer, sub * 3 + 1, row_fn),
        _mod_spec(layer, sub * 3 + 2, row_fn),
        _vec_spec(layer, sub),
        pl.BlockSpec((None, D_MODEL, tf), lambda i, j: (layer, 0, j)),
        pl.BlockSpec((None, D_MODEL, tf), lambda i, j: (layer, 0, j)),
        pl.BlockSpec((None, tf, D_MODEL), lambda i, j: (layer, j, 0)),
    ]
    args = [x, mod5, mod5, mod5, norm_w4, wg, wu, wd]
    if final:
        in_specs.append(pl.BlockSpec((1, D_MODEL), lambda i, j: (0, 0)))
        args.append(final_w)
    return pl.pallas_call(
        functools.partial(_ffn_kernel, n_ff=n_ff, final=final),
        out_shape=jax.ShapeDtypeStruct((n_tok, D_MODEL), f32),
        grid=(n_tok // tm, n_ff),
        in_specs=in_specs,
        out_specs=pl.BlockSpec((tm, D_MODEL), lambda i, j: (i, 0)),
        scratch_shapes=[pltpu.VMEM((tm, D_MODEL), bf16)],
        compiler_params=_cparams(("parallel", "arbitrary"), 52),
        name=name,
    )(*args)


def _swap16(x):
    n = x.shape[-1]
    lane = lax.broadcasted_iota(jnp.int32, x.shape, x.ndim - 1)
    first = (lane % AXIS_ROT) < (AXIS_ROT // 2)
    return jnp.where(first, pltpu.roll(x, n - AXIS_ROT // 2, x.ndim - 1),
                     pltpu.roll(x, AXIS_ROT // 2, x.ndim - 1))


def _inproj_kernel(*refs, rope):
    if rope:
        (x_ref, sh_ref, sc_ref, nw_ref, w_ref, cos_ref, sin_ref,
         q_ref, k_ref, v_ref, xr_ref, yr_ref, xf_ref) = refs
    else:
        (x_ref, sh_ref, sc_ref, nw_ref, w_ref,
         q_ref, k_ref, v_ref, xr_ref, yr_ref, xf_ref) = refs
    h = _modnorm(x_ref[...], nw_ref[...], sh_ref[...], sc_ref[...]).astype(bf16)
    o = ATTN_WIDTH
    k = _dot(h, w_ref[:, o:o + KV_WIDTH])
    if rope:
        cos = cos_ref[...]
        sin = sin_ref[...]
        k = k * cos + _swap16(k) * sin
    k_ref[...] = k
    for c in range(ATTN_WIDTH // LANES):
        q = _dot(h, w_ref[:, c * LANES:(c + 1) * LANES])
        if rope:
            q = q * cos + _swap16(q) * sin
        q_ref[:, c * LANES:(c + 1) * LANES] = q.astype(bf16)
    o += KV_WIDTH
    v_ref[...] = _dot(h, w_ref[:, o:o + KV_WIDTH])
    o += KV_WIDTH
    xr_ref[...] = _dot(h, w_ref[:, o:o + LRU_WIDTH])
    o += LRU_WIDTH
    yr_ref[...] = _dot(h, w_ref[:, o:o + LRU_WIDTH])
    o += LRU_WIDTH
    xf_ref[...] = _dot(h, w_ref[:, o:o + FOURIER_WIDTH])


def _inproj_call(x, mod5, norm_w4, w_in_bf, rope_tabs, *, layer, row_fn, tm, seq, name):
    n_tok = x.shape[0]
    rope = rope_tabs is not None
    in_specs = [
        pl.BlockSpec((tm, D_MODEL), lambda i: (i, 0)),
        _mod_spec(layer, 3, row_fn),
        _mod_spec(layer, 4, row_fn),
        _vec_spec(layer, 1),
        pl.BlockSpec((None, D_MODEL, IN_WIDTH), lambda i: (layer, 0, 0)),
    ]
    args = [x, mod5, mod5, norm_w4, w_in_bf]
    if rope:
        per_seq = seq // tm
        in_specs += [pl.BlockSpec((tm, LANES), lambda i: (i % per_seq, 0))] * 2
        args += list(rope_tabs)
    widths = (ATTN_WIDTH, KV_WIDTH, KV_WIDTH, LRU_WIDTH, LRU_WIDTH, FOURIER_WIDTH)
    dtypes = (bf16, f32, f32, f32, f32, f32)
    return pl.pallas_call(
        functools.partial(_inproj_kernel, rope=rope),
        out_shape=[jax.ShapeDtypeStruct((n_tok, w), d) for w, d in zip(widths, dtypes)],
        grid=(n_tok // tm,),
        in_specs=in_specs,
        out_specs=[pl.BlockSpec((tm, w), lambda i: (i, 0)) for w in widths],
        compiler_params=_cparams(("parallel",), 48),
        name=name,
    )(*args)


def _half_lane_variants(x):
    lane = lax.broadcasted_iota(jnp.int32, x.shape, 1)
    low = lane < HEAD_DIM
    sw = pltpu.roll(x, HEAD_DIM, 1)
    zero = jnp.zeros_like(x)
    head0 = (jnp.where(low, x, zero), jnp.where(low, zero, sw))
    head1 = (jnp.where(low, sw, zero), jnp.where(low, zero, x))
    return [tuple(v.astype(bf16) for v in h) for h in (head0, head1)]


def _sink_softmax(s, sk):
    m = jnp.maximum(jnp.max(s, axis=-1, keepdims=True), sk)
    p = jnp.exp(s - m)
    denom = jnp.sum(p, axis=-1, keepdims=True) + jnp.exp(sk - m)
    return p.astype(bf16), denom


def _sink_heads(sink_ref, q_ref, keys, vals, bias_fn, o_ref):
    nq = q_ref.shape[0]
    kvar = _half_lane_variants(keys)
    vvar = _half_lane_variants(vals)
    top = lax.broadcasted_iota(jnp.int32, (2 * nq, 1), 0) < nq
    low = lax.broadcasted_iota(jnp.int32, (2 * nq, LANES), 1) < HEAD_DIM
    nt = (((1,), (1,)), ((), ()))
    for g in range(KV_HEADS):
        c0 = g * Q_GROUP * HEAD_DIM
        q2 = jnp.concatenate([q_ref[:, c0:c0 + LANES], q_ref[:, c0 + LANES:c0 + 2 * LANES]], axis=0)
        q2 = q2 * jnp.asarray(HEAD_DIM ** -0.5, bf16)
        parts = []
        for par in range(2):
            s = lax.dot_general(q2, kvar[g][par], nt, preferred_element_type=f32)
            if bias_fn is not None:
                s = bias_fn(s)
            n_top = g * Q_GROUP + par
            sk = jnp.where(top, sink_ref[n_top], sink_ref[n_top + 2])
            parts.append(_sink_softmax(s, sk))
        (p_e, d_e), (p_o, d_o) = parts
        o2 = _dot(p_e, vvar[g][0]) + _dot(p_o, vvar[g][1])
        o2 = o2 * jnp.where(low, 1.0 / d_e, 1.0 / d_o)
        o_ref[:, c0:c0 + LANES] = o2[:nq].astype(o_ref.dtype)
        o_ref[:, c0 + LANES:c0 + 2 * LANES] = o2[nq:].astype(o_ref.dtype)


def _ctx_attn_kernel(sink_ref, q_ref, k_ref, v_ref, o_ref):
    _sink_heads(sink_ref, q_ref, k_ref[...], v_ref[...], None, o_ref)


def _ctx_attn_call(sink, q, k, v, *, batch, seq):
    return pl.pallas_call(
        _ctx_attn_kernel,
        out_shape=jax.ShapeDtypeStruct((batch * seq, ATTN_WIDTH), bf16),
        grid=(batch,),
        in_specs=[
            pl.BlockSpec(memory_space=pltpu.SMEM),
            pl.BlockSpec((seq, ATTN_WIDTH), lambda b: (b, 0)),
            pl.BlockSpec((seq, KV_WIDTH), lambda b: (b, 0)),
            pl.BlockSpec((seq, KV_WIDTH), lambda b: (b, 0)),
        ],
        out_specs=pl.BlockSpec((seq, ATTN_WIDTH), lambda b: (b, 0)),
        compiler_params=_cparams(("parallel",), 32),
        name="ctx_attn",
    )(sink, q, k, v)


def _lat_attn_kernel(sink_ref, q_ref, kp_ref, kc_ref, kn_ref, vp_ref, vc_ref, vn_ref,
                     kx_ref, vx_ref, o_ref, *, n_blocks, past):
    n = pl.program_id(1)
    keys = jnp.concatenate([kp_ref[...], kc_ref[...], kn_ref[...], kx_ref[...]], axis=0)
    vals = jnp.concatenate([vp_ref[...], vc_ref[...], vn_ref[...], vx_ref[...]], axis=0)
    assert WINDOW == BLOCK
    a_idx = lax.broadcasted_iota(jnp.int32, (2 * BLOCK, BLOCK), 0) % BLOCK
    c_idx = lax.broadcasted_iota(jnp.int32, (2 * BLOCK, BLOCK), 1)
    bias_p = jnp.where((c_idx >= a_idx) & (n > 0), 0.0, NEG).astype(f32)
    bias_n = jnp.where((c_idx <= a_idx) & (n < n_blocks - 1), 0.0, NEG).astype(f32)

    def bias_fn(s):
        return jnp.concatenate([s[:, :BLOCK] + bias_p, s[:, BLOCK:2 * BLOCK],
                                s[:, 2 * BLOCK:3 * BLOCK] + bias_n, s[:, 3 * BLOCK:]], axis=1)

    _sink_heads(sink_ref, q_ref, keys, vals, bias_fn, o_ref)


def _lat_attn_call(sink, q, k, v, cache_k4, cache_v4, *, layer, batch, seq):
    nb = seq // BLOCK
    past = cache_k4.shape[2]

    def blk(off):
        return pl.BlockSpec((BLOCK, KV_WIDTH),
                            lambda b, n: (b * nb + jnp.clip(n + off, 0, nb - 1), 0))

    ctx = pl.BlockSpec((None, None, past, KV_WIDTH), lambda b, n: (b, layer, 0, 0))
    return pl.pallas_call(
        functools.partial(_lat_attn_kernel, n_blocks=nb, past=past),
        out_shape=jax.ShapeDtypeStruct((batch * seq, ATTN_WIDTH), bf16),
        grid=(batch, nb),
        in_specs=[
            pl.BlockSpec(memory_space=pltpu.SMEM),
            pl.BlockSpec((BLOCK, ATTN_WIDTH), lambda b, n: (b * nb + n, 0)),
            blk(-1), blk(0), blk(1), blk(-1), blk(0), blk(1), ctx, ctx,
        ],
        out_specs=pl.BlockSpec((BLOCK, ATTN_WIDTH), lambda b, n: (b * nb + n, 0)),
        compiler_params=_cparams(("parallel", "parallel"), 32),
        name="lat_attn",
    )(sink, q, k, k, k, v, v, v, cache_k4, cache_v4)


def _softplus(x):
    return jnp.maximum(x, 0.0) + jnp.log1p(jnp.exp(-jnp.abs(x)))


def _gelu_tanh(x):
    return 0.5 * x * (1.0 + jnp.tanh(math.sqrt(2.0 / math.pi) * (x + 0.044715 * (x * x * x))))


def _seg_pitch(seg):
    assert seg % SUBLANES == 0
    tiles = seg // SUBLANES
    return seg if tiles % 2 else seg + SUBLANES


def _seg_row(t, seg, pitch):
    return (t // seg) * pitch + t % seg


def _lru_kernel(xr_ref, yr_ref, cw_ref, cb_ref, wd_ref, bias_ref, lam_ref, h0_ref,
                rec_ref, st_ref, xs_ref, af_ref, bf_ref, ab_ref, bb_ref, *, seq):
    seg = seq // LRU_SEGMENTS
    chunk = min(seq, 256)
    hw = LRU_HALF
    n_lt = hw // LANES
    pitch = _seg_pitch(seg)
    piece = min(seg, chunk)

    zeros_pad = jnp.zeros((PAD_ROWS, hw), f32)
    xs_ref[0:PAD_ROWS, :] = zeros_pad
    xs_ref[PAD_ROWS + seq:2 * PAD_ROWS + seq, :] = zeros_pad
    for c in range(seq // chunk):
        xs_ref[PAD_ROWS + c * chunk:PAD_ROWS + (c + 1) * chunk, :] = xr_ref[c * chunk:(c + 1) * chunk, :]

    neg_c_sp = [-LRU_C * _softplus(-lam_ref[d:d + 1, :]) for d in range(2)]
    a_refs = (af_ref, ab_ref)
    b_refs = (bf_ref, bb_ref)

    for c in range(seq // chunk):
        r0 = c * chunk
        xc = cb_ref[...]
        for j in range(CONV_W):
            s0 = PAD_ROWS + r0 + j - CONV_LEFT
            xc = xc + xs_ref[s0:s0 + chunk, :] * cw_ref[j:j + 1, :]
        y = _dot(xc.astype(bf16), wd_ref[...]) + bias_ref[...]
        for d in range(2):
            r = _sigmoid(y[:, (2 * d) * hw:(2 * d + 1) * hw])
            gi = _sigmoid(y[:, (2 * d + 1) * hw:(2 * d + 2) * hw])
            log_a = neg_c_sp[d] * r
            a = jnp.exp(log_a)
            z = -jnp.tanh(log_a) * (1.0 + a * a)
            b = jnp.where(z > 0.0, z * lax.rsqrt(z), 0.0) * (gi * xc)
            for p0 in range(0, chunk, piece):
                dst = _seg_row(r0 + p0, seg, pitch)
                for t in range(n_lt):
                    lanes = slice(t * LANES, (t + 1) * LANES)
                    a_refs[d][t, dst:dst + piece, :] = a[p0:p0 + piece, lanes]
                    b_refs[d][t, dst:dst + piece, :] = b[p0:p0 + piece, lanes]

    def step(j, carry):
        jf = pl.ds(j, LRU_SEGMENTS, stride=pitch)
        jb = pl.ds(seg - 1 - j, LRU_SEGMENTS, stride=pitch)
        out = []
        for t in range(n_lt):
            hf, pf, hb, pb = carry[4 * t:4 * t + 4]
            a = af_ref[t, jf, :]
            hf = a * hf + bf_ref[t, jf, :]
            pf = a * pf
            bf_ref[t, jf, :] = hf
            af_ref[t, jf, :] = pf
            a = ab_ref[t, jb, :]
            hb = a * hb + bb_ref[t, jb, :]
            pb = a * pb
            bb_ref[t, jb, :] = hb
            ab_ref[t, jb, :] = pb
            out += [hf, pf, hb, pb]
        return tuple(out)

    z = jnp.zeros((LRU_SEGMENTS, LANES), f32)
    o = jnp.ones((LRU_SEGMENTS, LANES), f32)
    fin = lax.fori_loop(0, seg, step, (z, o, z, o) * n_lt)

    sub = min(seg, 256)
    for t in range(n_lt):
        hf, pf, hb, pb = fin[4 * t:4 * t + 4]
        lanes = slice(t * LANES, (t + 1) * LANES)
        cf = h0_ref[0:1, lanes]
        carry_f = []
        for s in range(LRU_SEGMENTS):
            carry_f.append(cf)
            cf = hf[s:s + 1, :] + pf[s:s + 1, :] * cf
        cb = h0_ref[1:2, lanes]
        carry_b = [None] * LRU_SEGMENTS
        for s in reversed(range(LRU_SEGMENTS)):
            carry_b[s] = cb
            cb = hb[s:s + 1, :] + pb[s:s + 1, :] * cb
        st_ref[0:1, lanes] = cf
        st_ref[1:2, lanes] = cb
        for s in range(LRU_SEGMENTS):
            for u in range(seg // sub):
                rows = slice(s * seg + u * sub, s * seg + (u + 1) * sub)
                src = slice(s * pitch + u * sub, s * pitch + (u + 1) * sub)
                h = ((bf_ref[t, src, :] + af_ref[t, src, :] * carry_f[s])
                     + (bb_ref[t, src, :] + ab_ref[t, src, :] * carry_b[s]))
                rec_ref[rows, lanes] = (h * _gelu_tanh(yr_ref[rows, lanes])).astype(rec_ref.dtype)


def _lru_call(xr, yr, conv_w, conv_b, wd, bias, lam, h0, *, layer, batch, seq):
    hw = LRU_HALF
    return pl.pallas_call(
        functools.partial(_lru_kernel, seq=seq),
        out_shape=[jax.ShapeDtypeStruct((batch * seq, LRU_WIDTH), bf16),
                   jax.ShapeDtypeStruct((batch, 2, LRU_WIDTH), f32)],
        grid=(batch, 2),
        in_specs=[
            pl.BlockSpec((seq, hw), lambda b, c: (b, c)),
            pl.BlockSpec((seq, hw), lambda b, c: (b, c)),
            pl.BlockSpec((None, CONV_W, hw), lambda b, c: (layer, 0, c)),
            pl.BlockSpec((None, 1, hw), lambda b, c: (layer, 0, c)),
            pl.BlockSpec((None, None, hw, 4 * hw), lambda b, c: (layer, c, 0, 0)),
            pl.BlockSpec((None, None, 1, 4 * hw), lambda b, c: (layer, c, 0, 0)),
            pl.BlockSpec((None, 2, hw), lambda b, c: (layer, 0, c)),
            pl.BlockSpec((None, 2, hw), lambda b, c: (b, 0, c)),
        ],
        out_specs=[pl.BlockSpec((seq, hw), lambda b, c: (b, c)),
                   pl.BlockSpec((None, 2, hw), lambda b, c: (b, 0, c))],
        scratch_shapes=([pltpu.VMEM((seq + 2 * PAD_ROWS, hw), f32)]
                        + [pltpu.VMEM((hw // LANES, LRU_SEGMENTS * _seg_pitch(seq // LRU_SEGMENTS), LANES),
                                      f32)] * 4),
        compiler_params=_cparams(("parallel", "parallel"), 40),
        name=f"lru_{seq}",
    )(xr, yr, conv_w, conv_b, wd, bias, lam, h0)


def _fourier_kernel(x_ref, csc_ref, csl_ref, o_ref, u_ref, *, seq, scale):
    i = pl.program_id(1)

    @pl.when(i == 0)
    def _():
        chunk = min(seq, 512)
        for c in range(seq // chunk):
            rows = slice(c * chunk, (c + 1) * chunk)
            for g in range(FOURIER_GROUPS):
                cols = slice(g * FOURIER_GW, (g + 1) * FOURIER_GW)
                u = _dot(x_ref[rows, cols].astype(bf16), csc_ref[...])
                u_ref[c * chunk:(c + 1) * chunk, cols] = u[:, :FOURIER_GW].astype(bf16)
                u_ref[seq + c * chunk:seq + (c + 1) * chunk, cols] = u[:, FOURIER_GW:].astype(bf16)

    o_ref[...] = (_dot(csl_ref[...], u_ref[...]) * scale).astype(o_ref.dtype)


def _dft_tables(n):
    k = np.arange(n, dtype=np.int64)
    ang = (2.0 * np.pi / n) * ((k[:, None] * k[None, :]) % n).astype(np.float64)
    return np.cos(ang), np.sin(ang)


def _fourier_call(xf, csc, csl, *, batch, seq):
    tl = min(seq, 512)
    nt = seq // tl
    scale = 1.0 / math.sqrt(seq * FOURIER_GW)
    return pl.pallas_call(
        functools.partial(_fourier_kernel, seq=seq, scale=scale),
        out_shape=jax.ShapeDtypeStruct((batch * seq, FOURIER_WIDTH), bf16),
        grid=(batch, nt),
        in_specs=[
            pl.BlockSpec((seq, FOURIER_WIDTH), lambda b, i: (b, 0)),
            pl.BlockSpec((FOURIER_GW, 2 * FOURIER_GW), lambda b, i: (0, 0)),
            pl.BlockSpec((tl, 2 * seq), lambda b, i: (i, 0)),
        ],
        out_specs=pl.BlockSpec((tl, FOURIER_WIDTH), lambda b, i: (b * nt + i, 0)),
        scratch_shapes=[pltpu.VMEM((2 * seq, FOURIER_WIDTH), bf16)],
        compiler_params=_cparams(("parallel", "arbitrary"), 40),
        name=f"fourier_{seq}",
    )(xf, csc, csl)


def _merge_kernel(x_ref, sh_ref, sc_ref, gt_ref, nw_ref, at_ref, rc_ref, fr_ref,
                  wbg_ref, bbg_ref, wao_ref, wlo_ref, wfo_ref, wo_ref, o_ref):
    x = x_ref[...]
    h = _modnorm(x, nw_ref[...], sh_ref[...], sc_ref[...]).astype(bf16)
    merged = None
    for idx, (br_ref, w_ref) in enumerate(((at_ref, wao_ref), (rc_ref, wlo_ref), (fr_ref, wfo_ref))):
        cols = slice(idx * D_MODEL, (idx + 1) * D_MODEL)
        g = _sigmoid(_dot(h, wbg_ref[:, cols]) + bbg_ref[:, cols])
        y = g * _dot(br_ref[...], w_ref[...])
        merged = y if merged is None else merged + y
    out = _dot(merged.astype(bf16), wo_ref[...])
    o_ref[...] = x + gt_ref[...] * out


def _merge_call(x, mod5, norm_w4, attn, rec, four, wbg, bbg, wao, wlo, wfo, wo, *, layer, row_fn, tm, name):
    n_tok = x.shape[0]

    def const(shape):
        nd = len(shape)
        return pl.BlockSpec((None,) + shape, lambda i: (layer,) + (0,) * nd,
                            pipeline_mode=pl.Buffered(1))

    def branch():
        return pl.BlockSpec((tm, ATTN_WIDTH), lambda i: (i, 0))

    return pl.pallas_call(
        _merge_kernel,
        out_shape=jax.ShapeDtypeStruct((n_tok, D_MODEL), f32),
        grid=(n_tok // tm,),
        in_specs=[
            pl.BlockSpec((tm, D_MODEL), lambda i: (i, 0)),
            _mod_spec(layer, 3, row_fn),
            _mod_spec(layer, 4, row_fn),
            _mod_spec(layer, 5, row_fn),
            _vec_spec(layer, 1),
            branch(), branch(), branch(),
            const((D_MODEL, N_BRANCH * D_MODEL)),
            const((1, N_BRANCH * D_MODEL)),
            const((ATTN_WIDTH, D_MODEL)),
            const((LRU_WIDTH, D_MODEL)),
            const((FOURIER_WIDTH, D_MODEL)),
            const((D_MODEL, D_MODEL)),
        ],
        out_specs=pl.BlockSpec((tm, D_MODEL), lambda i: (i, 0)),
        compiler_params=_cparams(("parallel",), 48),
        name=name,
    )(x, mod5, mod5, mod5, norm_w4, attn, rec, four, wbg, bbg, wao, wlo, wfo, wo)


def _rope_lane_tables(n_tokens):
    rows = n_tokens // GRID_W
    row = jnp.repeat(jnp.arange(rows), GRID_W).astype(f32)
    col = jnp.tile(jnp.arange(GRID_W), rows).astype(f32)
    inv = ROPE_BASE ** (-jnp.arange(0, AXIS_ROT, 2, dtype=f32) / AXIS_ROT)
    ang = jnp.stack([row[:, None] * inv, col[:, None] * inv], axis=1)
    cos, sin = jnp.cos(ang), jnp.sin(ang)
    cos_h = jnp.stack([cos, cos], axis=2).reshape(n_tokens, HEAD_DIM)
    sin_h = jnp.stack([-sin, sin], axis=2).reshape(n_tokens, HEAD_DIM)
    reps = LANES // HEAD_DIM
    return jnp.tile(cos_h, (1, reps)), jnp.tile(sin_h, (1, reps))


def _lru_dense_weights(lru_wa, lru_wi, lru_ba, lru_bi):
    eye = jnp.eye(LRU_BLOCKS, dtype=f32)

    def dense(w):
        d = w[:, :, :, :, None, :] * eye[None, None, :, None, :, None]
        return d.reshape(DEPTH, 2, LRU_WIDTH, LRU_WIDTH)

    da, di = dense(lru_wa), dense(lru_wi)
    hw = LRU_HALF
    halves, biases = [], []
    for c in range(2):
        sl = slice(c * hw, (c + 1) * hw)
        halves.append(jnp.concatenate(
            [da[:, 0, sl, sl], di[:, 0, sl, sl], da[:, 1, sl, sl], di[:, 1, sl, sl]], axis=-1))
        biases.append(jnp.concatenate(
            [lru_ba[:, 0, sl], lru_bi[:, 0, sl], lru_ba[:, 1, sl], lru_bi[:, 1, sl]], axis=-1))
    wd = jnp.stack(halves, axis=1).astype(bf16)
    bias = jnp.stack(biases, axis=1)[:, :, None, :]
    return wd, bias


def kernel(x_prompt, x_sample, c, cache_k, cache_v, state_lru, c_ctx, w_ada, b_ada, norm_w, final_norm_w,
           ffn1_wg, ffn1_wu, ffn1_wd, ffn2_wg, ffn2_wu, ffn2_wd, w_in, w_branch_gate, b_branch_gate,
           attn_sink, w_attn_out, conv_w, conv_b, lru_wa, lru_ba, lru_wi, lru_bi, lru_lambda,
           w_lru_out, w_fourier_out, w_o):
    batch, seq, _ = x_prompt.shape
    dec_batch, dec_seq, _ = x_sample.shape
    past = cache_k.shape[2]
    assert 1 + dec_batch <= COND_ROWS

    cond = jnp.concatenate([c_ctx[None, :], c, jnp.zeros((COND_ROWS - 1 - dec_batch, D_MODEL), f32)], axis=0)
    mod = _ada_call(cond, w_ada, b_ada)
    mod5 = mod.reshape(DEPTH, COND_ROWS, N_SUB * 3, D_MODEL).transpose(0, 2, 1, 3)[:, :, :, None, :]
    norm_w4 = norm_w[:, :, None, :]
    final_w = final_norm_w[None, :]

    w_in_bf = w_in.astype(bf16)
    wbg_bf = w_branch_gate.astype(bf16)
    bbg = b_branch_gate[:, None, :]
    wao_bf = w_attn_out.astype(bf16)
    wlo_bf = w_lru_out.astype(bf16)
    wfo_bf = w_fourier_out.astype(bf16)
    wo_bf = w_o.astype(bf16)
    lru_wd, lru_bias = _lru_dense_weights(lru_wa, lru_wi, lru_ba, lru_bi)
    conv_b3 = conv_b[:, None, :]
    rope_tabs = _rope_lane_tables(dec_seq)
    cc, sc_ = _dft_tables(FOURIER_GW)
    csc = jnp.asarray(np.concatenate([cc, sc_], axis=1), f32).astype(bf16)
    csl = {}
    for n in (seq, dec_seq):
        cl, sl = _dft_tables(n)
        csl[n] = jnp.asarray(np.concatenate([cl, -sl], axis=1), f32).astype(bf16)
    cache_k4 = cache_k.reshape(dec_batch, DEPTH, past, KV_WIDTH)
    cache_v4 = cache_v.reshape(dec_batch, DEPTH, past, KV_WIDTH)
    h0_ctx = jnp.zeros((batch, 2, LRU_WIDTH), f32)

    tm_p, tm_s = 2048, 2048
    row_p = lambda i: 0
    row_s = lambda i: 1 + (i * tm_s) // dec_seq
    tmi = 512
    row_pi = lambda i: 0
    row_si = lambda i: 1 + (i * tmi) // dec_seq

    xp = x_prompt.reshape(batch * seq, D_MODEL)
    xs = x_sample.reshape(dec_batch * dec_seq, D_MODEL)
    ks, vs, ss = [], [], []
    for l in range(DEPTH):
        last = l == DEPTH - 1
        sink = attn_sink[l]
        xp = _ffn_call(xp, mod5, norm_w4, ffn1_wg, ffn1_wu, ffn1_wd, None,
                       layer=l, sub=0, row_fn=row_p, tm=tm_p, name="ffn1_ctx")
        xs = _ffn_call(xs, mod5, norm_w4, ffn1_wg, ffn1_wu, ffn1_wd, None,
                       layer=l, sub=0, row_fn=row_s, tm=tm_s, name="ffn1_lat")
        q, k, v, xr, yr, xf = _inproj_call(xp, mod5, norm_w4, w_in_bf, None,
                                           layer=l, row_fn=row_pi, tm=tmi, seq=seq, name="inproj_ctx")
        ks.append(k.reshape(batch, seq, KV_HEADS, HEAD_DIM))
        vs.append(v.reshape(batch, seq, KV_HEADS, HEAD_DIM))
        attn = _ctx_attn_call(sink, q, k, v, batch=batch, seq=seq)
        rec, st = _lru_call(xr, yr, conv_w, conv_b3, lru_wd, lru_bias, lru_lambda, h0_ctx,
                            layer=l, batch=batch, seq=seq)
        ss.append(st)
        four = _fourier_call(xf, csc, csl[seq], batch=batch, seq=seq)
        xp = _merge_call(xp, mod5, norm_w4, attn, rec, four, wbg_bf, bbg, wao_bf, wlo_bf, wfo_bf, wo_bf,
                         layer=l, row_fn=row_pi, tm=tmi, name="merge_ctx")
        q, k, v, xr, yr, xf = _inproj_call(xs, mod5, norm_w4, w_in_bf, rope_tabs,
                                           layer=l, row_fn=row_si, tm=tmi, seq=dec_seq, name="inproj_lat")
        attn = _lat_attn_call(sink, q, k, v, cache_k4, cache_v4, layer=l, batch=dec_batch, seq=dec_seq)
        rec, _ = _lru_call(xr, yr, conv_w, conv_b3, lru_wd, lru_bias, lru_lambda, state_lru[:, l],
                           layer=l, batch=dec_batch, seq=dec_seq)
        four = _fourier_call(xf, csc, csl[dec_seq], batch=dec_batch, seq=dec_seq)
        xs = _merge_call(xs, mod5, norm_w4, attn, rec, four, wbg_bf, bbg, wao_bf, wlo_bf, wfo_bf, wo_bf,
                         layer=l, row_fn=row_si, tm=tmi, name="merge_lat")
        xp = _ffn_call(xp, mod5, norm_w4, ffn2_wg, ffn2_wu, ffn2_wd, final_w if last else None,
                       layer=l, sub=2, row_fn=row_p, tm=tm_p, name="ffn2_ctx")
        xs = _ffn_call(xs, mod5, norm_w4, ffn2_wg, ffn2_wu, ffn2_wd, final_w if last else None,
                       layer=l, sub=2, row_fn=row_s, tm=tm_s, name="ffn2_lat")

    y_prompt = xp.reshape(batch, seq, D_MODEL)
    y_sample = xs.reshape(dec_batch, dec_seq, D_MODEL)
    return (y_prompt, y_sample, jnp.stack(ks, axis=1), jnp.stack(vs, axis=1), jnp.stack(ss, axis=1))
```

```python
import functools
import math

import numpy as np
import jax
import jax.numpy as jnp
from jax import lax
from jax.experimental import pallas as pl
from jax.experimental.pallas import tpu as pltpu

f32 = jnp.float32
bf16 = jnp.bfloat16

D_MODEL = 1024
DEPTH = 2
GRID_W = 64
N_HEADS = 8
KV_HEADS = 2
HEAD_DIM = 64
Q_GROUP = N_HEADS // KV_HEADS
ATTN_WIDTH = N_HEADS * HEAD_DIM
KV_WIDTH = KV_HEADS * HEAD_DIM
WINDOW = 128
BLOCK = 128
AXIS_ROT = HEAD_DIM // 2
ROPE_BASE = 10000.0
LRU_WIDTH = 512
LRU_BLOCKS = 8
LRU_BW = LRU_WIDTH // LRU_BLOCKS
LRU_C = 8.0
CONV_W = 4
CONV_LEFT = 2
FOURIER_WIDTH = 512
FOURIER_GROUPS = 4
FOURIER_GW = FOURIER_WIDTH // FOURIER_GROUPS
D_FF = 2816
N_BRANCH = 3
N_SUB = 3
EPS = 1e-6
NEG = -1e30
IN_WIDTH = ATTN_WIDTH + 2 * KV_WIDTH + 2 * LRU_WIDTH + FOURIER_WIDTH

V7X_VMEM_BYTES = 64 * 1024 * 1024
SUBLANES = 8
LANES = 128
COND_ROWS = 8
LRU_HALF = LRU_WIDTH // 2
LRU_SEGMENTS = SUBLANES
PAD_ROWS = SUBLANES

_MIB = 1024 * 1024


def _cparams(sem, vmem_mib):
    del vmem_mib
    return pltpu.CompilerParams(dimension_semantics=sem, vmem_limit_bytes=V7X_VMEM_BYTES)


def _dot(a, b):
    return jnp.dot(a, b, preferred_element_type=f32)


def _sigmoid(x):
    return jax.nn.sigmoid(x)


def _modnorm(x, nw, shift, scale):
    y = x * lax.rsqrt(jnp.mean(x * x, axis=-1, keepdims=True) + EPS)
    return y * (nw * (1.0 + scale)) + shift


def _ada_kernel(cond_ref, w_ref, b_ref, o_ref):
    c = cond_ref[...]
    s = (c * _sigmoid(c)).astype(bf16)
    o_ref[...] = _dot(s, w_ref[...].astype(bf16)) + b_ref[...]


def _ada_call(cond, w_ada, b_ada):
    n_out = w_ada.shape[-1]
    tn = 1024
    return pl.pallas_call(
        _ada_kernel,
        out_shape=jax.ShapeDtypeStruct((DEPTH, COND_ROWS, n_out), f32),
        grid=(DEPTH, n_out // tn),
        in_specs=[
            pl.BlockSpec((COND_ROWS, D_MODEL), lambda l, j: (0, 0)),
            pl.BlockSpec((None, D_MODEL, tn), lambda l, j: (l, 0, j)),
            pl.BlockSpec((None, 1, tn), lambda l, j: (l, 0, j)),
        ],
        out_specs=pl.BlockSpec((None, COND_ROWS, tn), lambda l, j: (l, 0, j)),
        compiler_params=_cparams(("parallel", "parallel"), 24),
        name="adaln",
    )(cond, w_ada, b_ada.reshape(DEPTH, 1, n_out))


def _mod_spec(layer, slot, row_fn):
    return pl.BlockSpec((None, None, None, 1, D_MODEL),
                        lambda i, *_: (layer, slot, row_fn(i), 0, 0))


def _vec_spec(layer, sub):
    return pl.BlockSpec((None, None, 1, D_MODEL), lambda i, *_: (layer, sub, 0, 0))


def _ffn_kernel(*refs, n_ff, final):
    if final:
        (x_ref, sh_ref, sc_ref, gt_ref, nw_ref, wg_ref, wu_ref, wd_ref, fw_ref,
         o_ref, h_ref) = refs
    else:
        (x_ref, sh_ref, sc_ref, gt_ref, nw_ref, wg_ref, wu_ref, wd_ref,
         o_ref, h_ref) = refs
    j = pl.program_id(1)

    @pl.when(j == 0)
    def _():
        h_ref[...] = _modnorm(x_ref[...], nw_ref[...], sh_ref[...], sc_ref[...]).astype(bf16)
        o_ref[...] = jnp.zeros_like(o_ref)

    h = h_ref[...]
    g = _dot(h, wg_ref[...].astype(bf16))
    u = _dot(h, wu_ref[...].astype(bf16))
    a = ((g * _sigmoid(g)) * u).astype(bf16)
    o_ref[...] += _dot(a, wd_ref[...].astype(bf16))

    @pl.when(j == n_ff - 1)
    def _():
        y = x_ref[...] + (0.5 * gt_ref[...]) * o_ref[...]
        if final:
            y = y * lax.rsqrt(jnp.mean(y * y, axis=-1, keepdims=True) + EPS) * fw_ref[...]
        o_ref[...] = y


def _ffn_call(x, mod5, norm_w4, wg, wu, wd, final_w, *, layer, sub, row_fn, tm, name):
    n_tok = x.shape[0]
    tf = 256
    n_ff = D_FF // tf
    final = final_w is not None
    in_specs = [
        pl.BlockSpec((tm, D_MODEL), lambda i, j: (i, 0), pipeline_mode=pl.Buffered(1)),
        _mod_spec(layer, sub * 3 + 0, row_fn),
        _mod_spec(layer, sub * 3 + 1, row_fn),
        _mod_spec(layer, sub * 3 + 2, row_fn),
        _vec_spec(layer, sub),
        pl.BlockSpec((None, D_MODEL, tf), lambda i, j: (layer, 0, j)),
        pl.BlockSpec((None, D_MODEL, tf), lambda i, j: (layer, 0, j)),
        pl.BlockSpec((None, tf, D_MODEL), lambda i, j: (layer, j, 0)),
    ]
    args = [x, mod5, mod5, mod5, norm_w4, wg, wu, wd]
    if final:
        in_specs.append(pl.BlockSpec((1, D_MODEL), lambda i, j: (0, 0)))
        args.append(final_w)
    return pl.pallas_call(
        functools.partial(_ffn_kernel, n_ff=n_ff, final=final),
        out_shape=jax.ShapeDtypeStruct((n_tok, D_MODEL), f32),
        grid=(n_tok // tm, n_ff),
        in_specs=in_specs,
        out_specs=pl.BlockSpec((tm, D_MODEL), lambda i, j: (i, 0)),
        scratch_shapes=[pltpu.VMEM((tm, D_MODEL), bf16)],
        compiler_params=_cparams(("parallel", "arbitrary"), 52),
        name=name,
    )(*args)


def _swap16(x):
    n = x.shape[-1]
    lane = lax.broadcasted_iota(jnp.int32, x.shape, x.ndim - 1)
    first = (lane % AXIS_ROT) < (AXIS_ROT // 2)
    return jnp.where(first, pltpu.roll(x, n - AXIS_ROT // 2, x.ndim - 1),
                     pltpu.roll(x, AXIS_ROT // 2, x.ndim - 1))


def _inproj_kernel(*refs, rope):
    if rope:
        (x_ref, sh_ref, sc_ref, nw_ref, w_ref, cos_ref, sin_ref,
         q_ref, k_ref, v_ref, xr_ref, yr_ref, xf_ref) = refs
    else:
        (x_ref, sh_ref, sc_ref, nw_ref, w_ref,
         q_ref, k_ref, v_ref, xr_ref, yr_ref, xf_ref) = refs
    h = _modnorm(x_ref[...], nw_ref[...], sh_ref[...], sc_ref[...]).astype(bf16)
    o = ATTN_WIDTH
    k = _dot(h, w_ref[:, o:o + KV_WIDTH])
    if rope:
        cos = cos_ref[...]
        sin = sin_ref[...]
        k = k * cos + _swap16(k) * sin
    k_ref[...] = k
    for c in range(ATTN_WIDTH // LANES):
        q = _dot(h, w_ref[:, c * LANES:(c + 1) * LANES])
        if rope:
            q = q * cos + _swap16(q) * sin
        q_ref[:, c * LANES:(c + 1) * LANES] = q.astype(bf16)
    o += KV_WIDTH
    v_ref[...] = _dot(h, w_ref[:, o:o + KV_WIDTH])
    o += KV_WIDTH
    xr_ref[...] = _dot(h, w_ref[:, o:o + LRU_WIDTH])
    o += LRU_WIDTH
    yr_ref[...] = _dot(h, w_ref[:, o:o + LRU_WIDTH])
    o += LRU_WIDTH
    xf_ref[...] = _dot(h, w_ref[:, o:o + FOURIER_WIDTH]).astype(bf16)


def _inproj_call(x, mod5, norm_w4, w_in_bf, rope_tabs, *, layer, row_fn, tm, seq, name):
    n_tok = x.shape[0]
    rope = rope_tabs is not None
    in_specs = [
        pl.BlockSpec((tm, D_MODEL), lambda i: (i, 0)),
        _mod_spec(layer, 3, row_fn),
        _mod_spec(layer, 4, row_fn),
        _vec_spec(layer, 1),
        pl.BlockSpec((None, D_MODEL, IN_WIDTH), lambda i: (layer, 0, 0)),
    ]
    args = [x, mod5, mod5, norm_w4, w_in_bf]
    if rope:
        per_seq = seq // tm
        in_specs += [pl.BlockSpec((tm, LANES), lambda i: (i % per_seq, 0))] * 2
        args += list(rope_tabs)
    widths = (ATTN_WIDTH, KV_WIDTH, KV_WIDTH, LRU_WIDTH, LRU_WIDTH, FOURIER_WIDTH)
    dtypes = (bf16, f32, f32, f32, f32, bf16)
    return pl.pallas_call(
        functools.partial(_inproj_kernel, rope=rope),
        out_shape=[jax.ShapeDtypeStruct((n_tok, w), d) for w, d in zip(widths, dtypes)],
        grid=(n_tok // tm,),
        in_specs=in_specs,
        out_specs=[pl.BlockSpec((tm, w), lambda i: (i, 0)) for w in widths],
        compiler_params=_cparams(("parallel",), 48),
        name=name,
    )(*args)


def _half_lane_variants(x):
    lane = lax.broadcasted_iota(jnp.int32, x.shape, 1)
    low = lane < HEAD_DIM
    sw = pltpu.roll(x, HEAD_DIM, 1)
    zero = jnp.zeros_like(x)
    head0 = (jnp.where(low, x, zero), jnp.where(low, zero, sw))
    head1 = (jnp.where(low, sw, zero), jnp.where(low, zero, x))
    return [tuple(v.astype(bf16) for v in h) for h in (head0, head1)]


def _sink_softmax(s, sk):
    m = jnp.maximum(jnp.max(s, axis=-1, keepdims=True), sk)
    p = jnp.exp(s - m)
    denom = jnp.sum(p, axis=-1, keepdims=True) + jnp.exp(sk - m)
    return p.astype(bf16), denom


def _sink_heads(sink_ref, q_ref, keys, vals, bias_fn, o_ref):
    nq = q_ref.shape[0]
    kvar = _half_lane_variants(keys)
    vvar = _half_lane_variants(vals)
    top = lax.broadcasted_iota(jnp.int32, (2 * nq, 1), 0) < nq
    low = lax.broadcasted_iota(jnp.int32, (2 * nq, LANES), 1) < HEAD_DIM
    nt = (((1,), (1,)), ((), ()))
    for g in range(KV_HEADS):
        c0 = g * Q_GROUP * HEAD_DIM
        q2 = jnp.concatenate([q_ref[:, c0:c0 + LANES], q_ref[:, c0 + LANES:c0 + 2 * LANES]], axis=0)
        q2 = q2 * jnp.asarray(HEAD_DIM ** -0.5, bf16)
        parts = []
        for par in range(2):
            s = lax.dot_general(q2, kvar[g][par], nt, preferred_element_type=f32)
            if bias_fn is not None:
                s = bias_fn(s)
            n_top = g * Q_GROUP + par
            sk = jnp.where(top, sink_ref[n_top], sink_ref[n_top + 2])
            parts.append(_sink_softmax(s, sk))
        (p_e, d_e), (p_o, d_o) = parts
        o2 = _dot(p_e, vvar[g][0]) + _dot(p_o, vvar[g][1])
        o2 = o2 * jnp.where(low, 1.0 / d_e, 1.0 / d_o)
        o_ref[:, c0:c0 + LANES] = o2[:nq].astype(o_ref.dtype)
        o_ref[:, c0 + LANES:c0 + 2 * LANES] = o2[nq:].astype(o_ref.dtype)


def _ctx_attn_kernel(sink_ref, q_ref, k_ref, v_ref, o_ref):
    _sink_heads(sink_ref, q_ref, k_ref[...], v_ref[...], None, o_ref)


def _ctx_attn_call(sink, q, k, v, *, batch, seq):
    return pl.pallas_call(
        _ctx_attn_kernel,
        out_shape=jax.ShapeDtypeStruct((batch * seq, ATTN_WIDTH), bf16),
        grid=(batch,),
        in_specs=[
            pl.BlockSpec(memory_space=pltpu.SMEM),
            pl.BlockSpec((seq, ATTN_WIDTH), lambda b: (b, 0)),
            pl.BlockSpec((seq, KV_WIDTH), lambda b: (b, 0)),
            pl.BlockSpec((seq, KV_WIDTH), lambda b: (b, 0)),
        ],
        out_specs=pl.BlockSpec((seq, ATTN_WIDTH), lambda b: (b, 0)),
        compiler_params=_cparams(("parallel",), 32),
        name="ctx_attn",
    )(sink, q, k, v)


def _lat_attn_kernel(sink_ref, q_ref, kp_ref, kc_ref, kn_ref, vp_ref, vc_ref, vn_ref,
                     kx_ref, vx_ref, o_ref, *, n_blocks, past):
    n = pl.program_id(1)
    keys = jnp.concatenate([kp_ref[...], kc_ref[...], kn_ref[...], kx_ref[...]], axis=0)
    vals = jnp.concatenate([vp_ref[...], vc_ref[...], vn_ref[...], vx_ref[...]], axis=0)
    assert WINDOW == BLOCK
    a_idx = lax.broadcasted_iota(jnp.int32, (2 * BLOCK, BLOCK), 0) % BLOCK
    c_idx = lax.broadcasted_iota(jnp.int32, (2 * BLOCK, BLOCK), 1)
    bias_p = jnp.where((c_idx >= a_idx) & (n > 0), 0.0, NEG).astype(f32)
    bias_n = jnp.where((c_idx <= a_idx) & (n < n_blocks - 1), 0.0, NEG).astype(f32)

    def bias_fn(s):
        return jnp.concatenate([s[:, :BLOCK] + bias_p, s[:, BLOCK:2 * BLOCK],
                                s[:, 2 * BLOCK:3 * BLOCK] + bias_n, s[:, 3 * BLOCK:]], axis=1)

    _sink_heads(sink_ref, q_ref, keys, vals, bias_fn, o_ref)


def _lat_attn_call(sink, q, k, v, cache_k4, cache_v4, *, layer, batch, seq):
    nb = seq // BLOCK
    past = cache_k4.shape[2]

    def blk(off):
        return pl.BlockSpec((BLOCK, KV_WIDTH),
                            lambda b, n: (b * nb + jnp.clip(n + off, 0, nb - 1), 0))

    ctx = pl.BlockSpec((None, None, past, KV_WIDTH), lambda b, n: (b, layer, 0, 0))
    return pl.pallas_call(
        functools.partial(_lat_attn_kernel, n_blocks=nb, past=past),
        out_shape=jax.ShapeDtypeStruct((batch * seq, ATTN_WIDTH), bf16),
        grid=(batch, nb),
        in_specs=[
            pl.BlockSpec(memory_space=pltpu.SMEM),
            pl.BlockSpec((BLOCK, ATTN_WIDTH), lambda b, n: (b * nb + n, 0)),
            blk(-1), blk(0), blk(1), blk(-1), blk(0), blk(1), ctx, ctx,
        ],
        out_specs=pl.BlockSpec((BLOCK, ATTN_WIDTH), lambda b, n: (b * nb + n, 0)),
        compiler_params=_cparams(("parallel", "parallel"), 32),
        name="lat_attn",
    )(sink, q, k, k, k, v, v, v, cache_k4, cache_v4)


def _softplus(x):
    return jnp.maximum(x, 0.0) + jnp.log1p(jnp.exp(-jnp.abs(x)))


def _gelu_tanh(x):
    return 0.5 * x * (1.0 + jnp.tanh(math.sqrt(2.0 / math.pi) * (x + 0.044715 * (x * x * x))))


def _seg_pitch(seg):
    assert seg % SUBLANES == 0
    tiles = seg // SUBLANES
    return seg if tiles % 2 else seg + SUBLANES


def _seg_row(t, seg, pitch):
    return (t // seg) * pitch + t % seg


def _lru_kernel(xr_ref, yr_ref, cw_ref, cb_ref, wd_ref, bias_ref, lam_ref, h0_ref,
                rec_ref, st_ref, xs_ref, af_ref, bf_ref, ab_ref, bb_ref, *, seq):
    seg = seq // LRU_SEGMENTS
    chunk = min(seq, 256)
    hw = LRU_HALF
    n_lt = hw // LANES
    pitch = _seg_pitch(seg)
    piece = min(seg, chunk)

    zeros_pad = jnp.zeros((PAD_ROWS, hw), f32)
    xs_ref[0:PAD_ROWS, :] = zeros_pad
    xs_ref[PAD_ROWS + seq:2 * PAD_ROWS + seq, :] = zeros_pad
    for c in range(seq // chunk):
        xs_ref[PAD_ROWS + c * chunk:PAD_ROWS + (c + 1) * chunk, :] = xr_ref[c * chunk:(c + 1) * chunk, :]

    neg_c_sp = [-LRU_C * _softplus(-lam_ref[d:d + 1, :]) for d in range(2)]
    a_refs = (af_ref, ab_ref)
    b_refs = (bf_ref, bb_ref)

    for c in range(seq // chunk):
        r0 = c * chunk
        xc = cb_ref[...]
        for j in range(CONV_W):
            s0 = PAD_ROWS + r0 + j - CONV_LEFT
            xc = xc + xs_ref[s0:s0 + chunk, :] * cw_ref[j:j + 1, :]
        y = _dot(xc.astype(bf16), wd_ref[...]) + bias_ref[...]
        for d in range(2):
            r = _sigmoid(y[:, (2 * d) * hw:(2 * d + 1) * hw])
            gi = _sigmoid(y[:, (2 * d + 1) * hw:(2 * d + 2) * hw])
            log_a = neg_c_sp[d] * r
            a = jnp.exp(log_a)
            z = -jnp.tanh(log_a) * (1.0 + a * a)
            b = jnp.where(z > 0.0, z * lax.rsqrt(z), 0.0) * (gi * xc)
            for p0 in range(0, chunk, piece):
                dst = _seg_row(r0 + p0, seg, pitch)
                for t in range(n_lt):
                    lanes = slice(t * LANES, (t + 1) * LANES)
                    a_refs[d][t, dst:dst + piece, :] = a[p0:p0 + piece, lanes]
                    b_refs[d][t, dst:dst + piece, :] = b[p0:p0 + piece, lanes]

    def step(j, carry):
        jf = pl.ds(j, LRU_SEGMENTS, stride=pitch)
        jb = pl.ds(seg - 1 - j, LRU_SEGMENTS, stride=pitch)
        out = []
        for t in range(n_lt):
            hf, pf, hb, pb = carry[4 * t:4 * t + 4]
            a = af_ref[t, jf, :]
            hf = a * hf + bf_ref[t, jf, :]
            pf = a * pf
            bf_ref[t, jf, :] = hf
            af_ref[t, jf, :] = pf
            a = ab_ref[t, jb, :]
            hb = a * hb + bb_ref[t, jb, :]
            pb = a * pb
            bb_ref[t, jb, :] = hb
            ab_ref[t, jb, :] = pb
            out += [hf, pf, hb, pb]
        return tuple(out)

    z = jnp.zeros((LRU_SEGMENTS, LANES), f32)
    o = jnp.ones((LRU_SEGMENTS, LANES), f32)
    fin = lax.fori_loop(0, seg, step, (z, o, z, o) * n_lt, unroll=2)

    sub = min(seg, 256)
    for t in range(n_lt):
        hf, pf, hb, pb = fin[4 * t:4 * t + 4]
        lanes = slice(t * LANES, (t + 1) * LANES)
        cf = h0_ref[0:1, lanes]
        carry_f = []
        for s in range(LRU_SEGMENTS):
            carry_f.append(cf)
            cf = hf[s:s + 1, :] + pf[s:s + 1, :] * cf
        cb = h0_ref[1:2, lanes]
        carry_b = [None] * LRU_SEGMENTS
        for s in reversed(range(LRU_SEGMENTS)):
            carry_b[s] = cb
            cb = hb[s:s + 1, :] + pb[s:s + 1, :] * cb
        st_ref[0:1, lanes] = cf
        st_ref[1:2, lanes] = cb
        for s in range(LRU_SEGMENTS):
            for u in range(seg // sub):
                rows = slice(s * seg + u * sub, s * seg + (u + 1) * sub)
                src = slice(s * pitch + u * sub, s * pitch + (u + 1) * sub)
                h = ((bf_ref[t, src, :] + af_ref[t, src, :] * carry_f[s])
                     + (bb_ref[t, src, :] + ab_ref[t, src, :] * carry_b[s]))
                rec_ref[rows, lanes] = (h * _gelu_tanh(yr_ref[rows, lanes])).astype(rec_ref.dtype)


def _lru_call(xr, yr, conv_w, conv_b, wd, bias, lam, h0, *, layer, batch, seq):
    hw = LRU_HALF
    return pl.pallas_call(
        functools.partial(_lru_kernel, seq=seq),
        out_shape=[jax.ShapeDtypeStruct((batch * seq, LRU_WIDTH), bf16),
                   jax.ShapeDtypeStruct((batch, 2, LRU_WIDTH), f32)],
        grid=(batch, 2),
        in_specs=[
            pl.BlockSpec((seq, hw), lambda b, c: (b, c)),
            pl.BlockSpec((seq, hw), lambda b, c: (b, c)),
            pl.BlockSpec((None, CONV_W, hw), lambda b, c: (layer, 0, c)),
            pl.BlockSpec((None, 1, hw), lambda b, c: (layer, 0, c)),
            pl.BlockSpec((None, None, hw, 4 * hw), lambda b, c: (layer, c, 0, 0)),
            pl.BlockSpec((None, None, 1, 4 * hw), lambda b, c: (layer, c, 0, 0)),
            pl.BlockSpec((None, 2, hw), lambda b, c: (layer, 0, c)),
            pl.BlockSpec((None, 2, hw), lambda b, c: (b, 0, c)),
        ],
        out_specs=[pl.BlockSpec((seq, hw), lambda b, c: (b, c)),
                   pl.BlockSpec((None, 2, hw), lambda b, c: (b, 0, c))],
        scratch_shapes=([pltpu.VMEM((seq + 2 * PAD_ROWS, hw), f32)]
                        + [pltpu.VMEM((hw // LANES, LRU_SEGMENTS * _seg_pitch(seq // LRU_SEGMENTS), LANES),
                                      f32)] * 4),
        compiler_params=_cparams(("parallel", "parallel"), 40),
        name=f"lru_{seq}",
    )(xr, yr, conv_w, conv_b, wd, bias, lam, h0)


FOURIER_REV = 128
FOURIER_PAD = 16


def _fourier_kernel(x_ref, csc_ref, ch_ref, sh_ref, rev_ref, o_ref, u_ref, *, seq, scale):
    half = seq // 2
    chunk = min(seq, 512)
    for c in range(seq // chunk):
        rows = slice(c * chunk, (c + 1) * chunk)
        for g in range(FOURIER_GROUPS):
            cols = slice(g * FOURIER_GW, (g + 1) * FOURIER_GW)
            u = _dot(x_ref[rows, cols], csc_ref[...])
            u_ref[c * chunk:(c + 1) * chunk, cols] = u[:, :FOURIER_GW].astype(bf16)
            u_ref[seq + c * chunk:seq + (c + 1) * chunk, cols] = u[:, FOURIER_GW:].astype(bf16)
    a = _dot(ch_ref[...], u_ref[0:seq, :])
    b = _dot(sh_ref[...], u_ref[seq:2 * seq, :])
    o_ref[0:half, :] = ((a[0:half] - b[0:half]) * scale).astype(o_ref.dtype)
    z = ((a[1:half + 1] + b[1:half + 1]) * scale).astype(bf16)
    nb = half // FOURIER_REV
    for k in range(nb):
        blk = z[(nb - 1 - k) * FOURIER_REV:(nb - k) * FOURIER_REV]
        o_ref[half + k * FOURIER_REV:half + (k + 1) * FOURIER_REV, :] = _dot(rev_ref[...], blk).astype(o_ref.dtype)


def _dft_tables(n):
    k = np.arange(n, dtype=np.int64)
    ang = (2.0 * np.pi / n) * ((k[:, None] * k[None, :]) % n).astype(np.float64)
    return np.cos(ang), np.sin(ang)


def _fourier_call(xf, csc, ch, sh, rev, *, batch, seq):
    scale = 1.0 / math.sqrt(seq * FOURIER_GW)
    rows = seq // 2 + FOURIER_PAD

    def const(shape):
        return pl.BlockSpec(shape, lambda b: (0, 0), pipeline_mode=pl.Buffered(1))

    return pl.pallas_call(
        functools.partial(_fourier_kernel, seq=seq, scale=scale),
        out_shape=jax.ShapeDtypeStruct((batch * seq, FOURIER_WIDTH), bf16),
        grid=(batch,),
        in_specs=[
            pl.BlockSpec((seq, FOURIER_WIDTH), lambda b: (b, 0)),
            const((FOURIER_GW, 2 * FOURIER_GW)),
            const((rows, seq)),
            const((rows, seq)),
            const((FOURIER_REV, FOURIER_REV)),
        ],
        out_specs=pl.BlockSpec((seq, FOURIER_WIDTH), lambda b: (b, 0)),
        scratch_shapes=[pltpu.VMEM((2 * seq, FOURIER_WIDTH), bf16)],
        compiler_params=_cparams(("parallel",), 40),
        name=f"fourier_{seq}",
    )(xf, csc, ch, sh, rev)


def _merge_kernel(x_ref, sh_ref, sc_ref, gt_ref, nw_ref, at_ref, rc_ref, fr_ref,
                  wbg_ref, bbg_ref, wao_ref, wlo_ref, wfo_ref, wo_ref, o_ref):
    x = x_ref[...]
    h = _modnorm(x, nw_ref[...], sh_ref[...], sc_ref[...]).astype(bf16)
    merged = None
    for idx, (br_ref, w_ref) in enumerate(((at_ref, wao_ref), (rc_ref, wlo_ref), (fr_ref, wfo_ref))):
        cols = slice(idx * D_MODEL, (idx + 1) * D_MODEL)
        g = _sigmoid(_dot(h, wbg_ref[:, cols]) + bbg_ref[:, cols])
        y = g * _dot(br_ref[...], w_ref[...])
        merged = y if merged is None else merged + y
    out = _dot(merged.astype(bf16), wo_ref[...])
    o_ref[...] = x + gt_ref[...] * out


def _merge_call(x, mod5, norm_w4, attn, rec, four, wbg, bbg, wao, wlo, wfo, wo, *, layer, row_fn, tm, name):
    n_tok = x.shape[0]

    def const(shape):
        nd = len(shape)
        return pl.BlockSpec((None,) + shape, lambda i: (layer,) + (0,) * nd,
                            pipeline_mode=pl.Buffered(1))

    def branch():
        return pl.BlockSpec((tm, ATTN_WIDTH), lambda i: (i, 0))

    return pl.pallas_call(
        _merge_kernel,
        out_shape=jax.ShapeDtypeStruct((n_tok, D_MODEL), f32),
        grid=(n_tok // tm,),
        in_specs=[
            pl.BlockSpec((tm, D_MODEL), lambda i: (i, 0)),
            _mod_spec(layer, 3, row_fn),
            _mod_spec(layer, 4, row_fn),
            _mod_spec(layer, 5, row_fn),
            _vec_spec(layer, 1),
            branch(), branch(), branch(),
            const((D_MODEL, N_BRANCH * D_MODEL)),
            const((1, N_BRANCH * D_MODEL)),
            const((ATTN_WIDTH, D_MODEL)),
            const((LRU_WIDTH, D_MODEL)),
            const((FOURIER_WIDTH, D_MODEL)),
            const((D_MODEL, D_MODEL)),
        ],
        out_specs=pl.BlockSpec((tm, D_MODEL), lambda i: (i, 0)),
        compiler_params=_cparams(("parallel",), 48),
        name=name,
    )(x, mod5, mod5, mod5, norm_w4, attn, rec, four, wbg, bbg, wao, wlo, wfo, wo)


def _rope_lane_tables(n_tokens):
    rows = n_tokens // GRID_W
    row = jnp.repeat(jnp.arange(rows), GRID_W).astype(f32)
    col = jnp.tile(jnp.arange(GRID_W), rows).astype(f32)
    inv = ROPE_BASE ** (-jnp.arange(0, AXIS_ROT, 2, dtype=f32) / AXIS_ROT)
    ang = jnp.stack([row[:, None] * inv, col[:, None] * inv], axis=1)
    cos, sin = jnp.cos(ang), jnp.sin(ang)
    cos_h = jnp.stack([cos, cos], axis=2).reshape(n_tokens, HEAD_DIM)
    sin_h = jnp.stack([-sin, sin], axis=2).reshape(n_tokens, HEAD_DIM)
    reps = LANES // HEAD_DIM
    return jnp.tile(cos_h, (1, reps)), jnp.tile(sin_h, (1, reps))


def _lru_dense_weights(lru_wa, lru_wi, lru_ba, lru_bi):
    per = LRU_BLOCKS // 2
    hw = LRU_HALF
    eye = jnp.eye(per, dtype=bf16)

    def dense(w, c):
        wh = w[:, :, c * per:(c + 1) * per].astype(bf16)
        d = wh[:, :, :, :, None, :] * eye[None, None, :, None, :, None]
        return d.reshape(DEPTH, 2, hw, hw)

    halves, biases = [], []
    for c in range(2):
        sl = slice(c * hw, (c + 1) * hw)
        da, di = dense(lru_wa, c), dense(lru_wi, c)
        halves.append(jnp.concatenate([da[:, 0], di[:, 0], da[:, 1], di[:, 1]], axis=-1))
        biases.append(jnp.concatenate(
            [lru_ba[:, 0, sl], lru_bi[:, 0, sl], lru_ba[:, 1, sl], lru_bi[:, 1, sl]], axis=-1))
    wd = jnp.stack(halves, axis=1)
    bias = jnp.stack(biases, axis=1)[:, :, None, :]
    return wd, bias


def kernel(x_prompt, x_sample, c, cache_k, cache_v, state_lru, c_ctx, w_ada, b_ada, norm_w, final_norm_w,
           ffn1_wg, ffn1_wu, ffn1_wd, ffn2_wg, ffn2_wu, ffn2_wd, w_in, w_branch_gate, b_branch_gate,
           attn_sink, w_attn_out, conv_w, conv_b, lru_wa, lru_ba, lru_wi, lru_bi, lru_lambda,
           w_lru_out, w_fourier_out, w_o):
    batch, seq, _ = x_prompt.shape
    dec_batch, dec_seq, _ = x_sample.shape
    past = cache_k.shape[2]
    assert 1 + dec_batch <= COND_ROWS

    cond = jnp.concatenate([c_ctx[None, :], c, jnp.zeros((COND_ROWS - 1 - dec_batch, D_MODEL), f32)], axis=0)
    mod = _ada_call(cond, w_ada, b_ada)
    mod5 = mod.reshape(DEPTH, COND_ROWS, N_SUB * 3, D_MODEL).transpose(0, 2, 1, 3)[:, :, :, None, :]
    norm_w4 = norm_w[:, :, None, :]
    final_w = final_norm_w[None, :]

    w_in_bf = w_in.astype(bf16)
    wbg_bf = w_branch_gate.astype(bf16)
    bbg = b_branch_gate[:, None, :]
    wao_bf = w_attn_out.astype(bf16)
    wlo_bf = w_lru_out.astype(bf16)
    wfo_bf = w_fourier_out.astype(bf16)
    wo_bf = w_o.astype(bf16)
    lru_wd, lru_bias = _lru_dense_weights(lru_wa, lru_wi, lru_ba, lru_bi)
    conv_b3 = conv_b[:, None, :]
    rope_tabs = _rope_lane_tables(dec_seq)
    cc, sc_ = _dft_tables(FOURIER_GW)
    csc = jnp.asarray(np.concatenate([cc, sc_], axis=1), f32).astype(bf16)
    dft = {}
    for n in (seq, dec_seq):
        cl, sl = _dft_tables(n)
        rows = n // 2 + FOURIER_PAD
        dft[n] = (jnp.asarray(cl[:rows], f32).astype(bf16), jnp.asarray(sl[:rows], f32).astype(bf16))
    rev = jnp.asarray(np.eye(FOURIER_REV)[::-1], f32).astype(bf16)
    cache_k4 = cache_k.reshape(dec_batch, DEPTH, past, KV_WIDTH)
    cache_v4 = cache_v.reshape(dec_batch, DEPTH, past, KV_WIDTH)
    h0_ctx = jnp.zeros((batch, 2, LRU_WIDTH), f32)

    tm_p, tm_s = 2048, 2048
    row_p = lambda i: 0
    row_s = lambda i: 1 + (i * tm_s) // dec_seq
    tmi = 512
    row_pi = lambda i: 0
    row_si = lambda i: 1 + (i * tmi) // dec_seq

    xp = x_prompt.reshape(batch * seq, D_MODEL)
    xs = x_sample.reshape(dec_batch * dec_seq, D_MODEL)
    ks, vs, ss = [], [], []
    for l in range(DEPTH):
        last = l == DEPTH - 1
        sink = attn_sink[l]
        xp = _ffn_call(xp, mod5, norm_w4, ffn1_wg, ffn1_wu, ffn1_wd, None,
                       layer=l, sub=0, row_fn=row_p, tm=tm_p, name="ffn1_ctx")
        xs = _ffn_call(xs, mod5, norm_w4, ffn1_wg, ffn1_wu, ffn1_wd, None,
                       layer=l, sub=0, row_fn=row_s, tm=tm_s, name="ffn1_lat")
        q, k, v, xr, yr, xf = _inproj_call(xp, mod5, norm_w4, w_in_bf, None,
                                           layer=l, row_fn=row_pi, tm=tmi, seq=seq, name="inproj_ctx")
        ks.append(k.reshape(batch, seq, KV_HEADS, HEAD_DIM))
        vs.append(v.reshape(batch, seq, KV_HEADS, HEAD_DIM))
        attn = _ctx_attn_call(sink, q, k, v, batch=batch, seq=seq)
        rec, st = _lru_call(xr, yr, conv_w, conv_b3, lru_wd, lru_bias, lru_lambda, h0_ctx,
                            layer=l, batch=batch, seq=seq)
        ss.append(st)
        four = _fourier_call(xf, csc, *dft[seq], rev, batch=batch, seq=seq)
        xp = _merge_call(xp, mod5, norm_w4, attn, rec, four, wbg_bf, bbg, wao_bf, wlo_bf, wfo_bf, wo_bf,
                         layer=l, row_fn=row_pi, tm=tmi, name="merge_ctx")
        q, k, v, xr, yr, xf = _inproj_call(xs, mod5, norm_w4, w_in_bf, rope_tabs,
                                           layer=l, row_fn=row_si, tm=tmi, seq=dec_seq, name="inproj_lat")
        attn = _lat_attn_call(sink, q, k, v, cache_k4, cache_v4, layer=l, batch=dec_batch, seq=dec_seq)
        rec, _ = _lru_call(xr, yr, conv_w, conv_b3, lru_wd, lru_bias, lru_lambda, state_lru[:, l],
                           layer=l, batch=dec_batch, seq=dec_seq)
        four = _fourier_call(xf, csc, *dft[dec_seq], rev, batch=dec_batch, seq=dec_seq)
        xs = _merge_call(xs, mod5, norm_w4, attn, rec, four, wbg_bf, bbg, wao_bf, wlo_bf, wfo_bf, wo_bf,
                         layer=l, row_fn=row_si, tm=tmi, name="merge_lat")
        xp = _ffn_call(xp, mod5, norm_w4, ffn2_wg, ffn2_wu, ffn2_wd, final_w if last else None,
                       layer=l, sub=2, row_fn=row_p, tm=tm_p, name="ffn2_ctx")
        xs = _ffn_call(xs, mod5, norm_w4, ffn2_wg, ffn2_wu, ffn2_wd, final_w if last else None,
                       layer=l, sub=2, row_fn=row_s, tm=tm_s, name="ffn2_lat")

    y_prompt = xp.reshape(batch, seq, D_MODEL)
    y_sample = xs.reshape(dec_batch, dec_seq, D_MODEL)
    return (y_prompt, y_sample, jnp.stack(ks, axis=1), jnp.stack(vs, axis=1), jnp.stack(ss, axis=1))
```

```python
import functools
import math

import numpy as np
import jax
import jax.numpy as jnp
from jax import lax
from jax.experimental import pallas as pl
from jax.experimental.pallas import tpu as pltpu

f32 = jnp.float32
bf16 = jnp.bfloat16

D_MODEL = 1024
DEPTH = 2
GRID_W = 64
N_HEADS = 8
KV_HEADS = 2
HEAD_DIM = 64
Q_GROUP = N_HEADS // KV_HEADS
ATTN_WIDTH = N_HEADS * HEAD_DIM
KV_WIDTH = KV_HEADS * HEAD_DIM
WINDOW = 128
BLOCK = 128
AXIS_ROT = HEAD_DIM // 2
ROPE_BASE = 10000.0
LRU_WIDTH = 512
LRU_BLOCKS = 8
LRU_BW = LRU_WIDTH // LRU_BLOCKS
LRU_C = 8.0
CONV_W = 4
CONV_LEFT = 2
FOURIER_WIDTH = 512
FOURIER_GROUPS = 4
FOURIER_GW = FOURIER_WIDTH // FOURIER_GROUPS
D_FF = 2816
N_BRANCH = 3
N_SUB = 3
EPS = 1e-6
NEG = -1e30
IN_WIDTH = ATTN_WIDTH + 2 * KV_WIDTH + 2 * LRU_WIDTH + FOURIER_WIDTH

V7X_VMEM_BYTES = 64 * 1024 * 1024
SUBLANES = 8
LANES = 128
COND_ROWS = 8
LRU_HALF = LRU_WIDTH // 2
LRU_SEGMENTS = SUBLANES
PAD_ROWS = SUBLANES

_MIB = 1024 * 1024


def _cparams(sem, vmem_mib):
    del vmem_mib
    return pltpu.CompilerParams(dimension_semantics=sem, vmem_limit_bytes=V7X_VMEM_BYTES)


def _dot(a, b):
    return jnp.dot(a, b, preferred_element_type=f32)


def _sigmoid(x):
    return jax.nn.sigmoid(x)


def _modnorm(x, nw, shift, scale):
    y = x * lax.rsqrt(jnp.mean(x * x, axis=-1, keepdims=True) + EPS)
    return y * (nw * (1.0 + scale)) + shift


def _ada_kernel(cond_ref, w_ref, b_ref, o_ref):
    c = cond_ref[...]
    s = (c * _sigmoid(c)).astype(bf16)
    o_ref[...] = _dot(s, w_ref[...].astype(bf16)) + b_ref[...]


def _ada_call(cond, w_ada, b_ada):
    n_out = w_ada.shape[-1]
    tn = 1024
    return pl.pallas_call(
        _ada_kernel,
        out_shape=jax.ShapeDtypeStruct((DEPTH, COND_ROWS, n_out), f32),
        grid=(DEPTH, n_out // tn),
        in_specs=[
            pl.BlockSpec((COND_ROWS, D_MODEL), lambda l, j: (0, 0)),
            pl.BlockSpec((None, D_MODEL, tn), lambda l, j: (l, 0, j)),
            pl.BlockSpec((None, 1, tn), lambda l, j: (l, 0, j)),
        ],
        out_specs=pl.BlockSpec((None, COND_ROWS, tn), lambda l, j: (l, 0, j)),
        compiler_params=_cparams(("parallel", "parallel"), 24),
        name="adaln",
    )(cond, w_ada, b_ada.reshape(DEPTH, 1, n_out))


def _mod_spec(layer, slot, row_fn):
    return pl.BlockSpec((None, None, None, 1, D_MODEL),
                        lambda i, *_: (layer, slot, row_fn(i), 0, 0))


def _vec_spec(layer, sub):
    return pl.BlockSpec((None, None, 1, D_MODEL), lambda i, *_: (layer, sub, 0, 0))


def _ffn_kernel(*refs, n_ff, final):
    if final:
        (x_ref, sh_ref, sc_ref, gt_ref, nw_ref, wg_ref, wu_ref, wd_ref, fw_ref,
         o_ref, h_ref) = refs
    else:
        (x_ref, sh_ref, sc_ref, gt_ref, nw_ref, wg_ref, wu_ref, wd_ref,
         o_ref, h_ref) = refs
    j = pl.program_id(1)

    @pl.when(j == 0)
    def _():
        h_ref[...] = _modnorm(x_ref[...], nw_ref[...], sh_ref[...], sc_ref[...]).astype(bf16)
        o_ref[...] = jnp.zeros_like(o_ref)

    h = h_ref[...]
    g = _dot(h, wg_ref[...].astype(bf16))
    u = _dot(h, wu_ref[...].astype(bf16))
    a = ((g * _sigmoid(g)) * u).astype(bf16)
    o_ref[...] += _dot(a, wd_ref[...].astype(bf16))

    @pl.when(j == n_ff - 1)
    def _():
        y = x_ref[...] + (0.5 * gt_ref[...]) * o_ref[...]
        if final:
            y = y * lax.rsqrt(jnp.mean(y * y, axis=-1, keepdims=True) + EPS) * fw_ref[...]
        o_ref[...] = y


def _ffn_call(x, mod5, norm_w4, wg, wu, wd, final_w, *, layer, sub, row_fn, tm, name):
    n_tok = x.shape[0]
    tf = 256
    n_ff = D_FF // tf
    final = final_w is not None
    in_specs = [
        pl.BlockSpec((tm, D_MODEL), lambda i, j: (i, 0)),
        _mod_spec(layer, sub * 3 + 0, row_fn),
        _mod_spec(layer, sub * 3 + 1, row_fn),
        _mod_spec(layer, sub * 3 + 2, row_fn),
        _vec_spec(layer, sub),
        pl.BlockSpec((None, D_MODEL, tf), lambda i, j: (layer, 0, j)),
        pl.BlockSpec((None, D_MODEL, tf), lambda i, j: (layer, 0, j)),
        pl.BlockSpec((None, tf, D_MODEL), lambda i, j: (layer, j, 0)),
    ]
    args = [x, mod5, mod5, mod5, norm_w4, wg, wu, wd]
    if final:
        in_specs.append(pl.BlockSpec((1, D_MODEL), lambda i, j: (0, 0)))
        args.append(final_w)
    return pl.pallas_call(
        functools.partial(_ffn_kernel, n_ff=n_ff, final=final),
        out_shape=jax.ShapeDtypeStruct((n_tok, D_MODEL), f32),
        grid=(n_tok // tm, n_ff),
        in_specs=in_specs,
        out_specs=pl.BlockSpec((tm, D_MODEL), lambda i, j: (i, 0)),
        scratch_shapes=[pltpu.VMEM((tm, D_MODEL), bf16)],
        compiler_params=_cparams(("parallel", "arbitrary"), 52),
        name=name,
    )(*args)


def _swap16(x):
    n = x.shape[-1]
    lane = lax.broadcasted_iota(jnp.int32, x.shape, x.ndim - 1)
    first = (lane % AXIS_ROT) < (AXIS_ROT // 2)
    return jnp.where(first, pltpu.roll(x, n - AXIS_ROT // 2, x.ndim - 1),
                     pltpu.roll(x, AXIS_ROT // 2, x.ndim - 1))


def _inproj_kernel(*refs, rope):
    if rope:
        (x_ref, sh_ref, sc_ref, nw_ref, w_ref, cos_ref, sin_ref,
         q_ref, k_ref, v_ref, xr_ref, yr_ref, xf_ref) = refs
    else:
        (x_ref, sh_ref, sc_ref, nw_ref, w_ref,
         q_ref, k_ref, v_ref, xr_ref, yr_ref, xf_ref) = refs
    h = _modnorm(x_ref[...], nw_ref[...], sh_ref[...], sc_ref[...]).astype(bf16)
    o = ATTN_WIDTH
    k = _dot(h, w_ref[:, o:o + KV_WIDTH])
    if rope:
        cos = cos_ref[...]
        sin = sin_ref[...]
        k = k * cos + _swap16(k) * sin
    k_ref[...] = k
    for c in range(ATTN_WIDTH // LANES):
        q = _dot(h, w_ref[:, c * LANES:(c + 1) * LANES])
        if rope:
            q = q * cos + _swap16(q) * sin
        q_ref[:, c * LANES:(c + 1) * LANES] = q.astype(bf16)
    o += KV_WIDTH
    v_ref[...] = _dot(h, w_ref[:, o:o + KV_WIDTH])
    o += KV_WIDTH
    xr_ref[...] = _dot(h, w_ref[:, o:o + LRU_WIDTH])
    o += LRU_WIDTH
    yr_ref[...] = _dot(h, w_ref[:, o:o + LRU_WIDTH])
    o += LRU_WIDTH
    xf_ref[...] = _dot(h, w_ref[:, o:o + FOURIER_WIDTH]).astype(bf16)


def _inproj_call(x, mod5, norm_w4, w_in_bf, rope_tabs, *, layer, row_fn, tm, seq, name):
    n_tok = x.shape[0]
    rope = rope_tabs is not None
    in_specs = [
        pl.BlockSpec((tm, D_MODEL), lambda i: (i, 0)),
        _mod_spec(layer, 3, row_fn),
        _mod_spec(layer, 4, row_fn),
        _vec_spec(layer, 1),
        pl.BlockSpec((None, D_MODEL, IN_WIDTH), lambda i: (layer, 0, 0)),
    ]
    args = [x, mod5, mod5, norm_w4, w_in_bf]
    if rope:
        per_seq = seq // tm
        in_specs += [pl.BlockSpec((tm, LANES), lambda i: (i % per_seq, 0))] * 2
        args += list(rope_tabs)
    widths = (ATTN_WIDTH, KV_WIDTH, KV_WIDTH, LRU_WIDTH, LRU_WIDTH, FOURIER_WIDTH)
    dtypes = (bf16, f32, f32, f32, f32, bf16)
    return pl.pallas_call(
        functools.partial(_inproj_kernel, rope=rope),
        out_shape=[jax.ShapeDtypeStruct((n_tok, w), d) for w, d in zip(widths, dtypes)],
        grid=(n_tok // tm,),
        in_specs=in_specs,
        out_specs=[pl.BlockSpec((tm, w), lambda i: (i, 0)) for w in widths],
        compiler_params=_cparams(("parallel",), 48),
        name=name,
    )(*args)


def _half_lane_variants(x):
    lane = lax.broadcasted_iota(jnp.int32, x.shape, 1)
    low = lane < HEAD_DIM
    sw = pltpu.roll(x, HEAD_DIM, 1)
    zero = jnp.zeros_like(x)
    head0 = (jnp.where(low, x, zero), jnp.where(low, zero, sw))
    head1 = (jnp.where(low, sw, zero), jnp.where(low, zero, x))
    return [tuple(v.astype(bf16) for v in h) for h in (head0, head1)]


def _sink_softmax(scores, sk):
    m = sk
    for s in scores:
        m = jnp.maximum(m, jnp.max(s, axis=-1, keepdims=True))
    denom = jnp.exp(sk - m)
    probs = []
    for s in scores:
        p = jnp.exp(s - m)
        denom = denom + jnp.sum(p, axis=-1, keepdims=True)
        probs.append(p.astype(bf16))
    return probs, denom


def _sink_heads(sink_ref, q_ref, kvar, vvar, bias_fns, o_ref):
    nq = q_ref.shape[0]
    top = lax.broadcasted_iota(jnp.int32, (2 * nq, 1), 0) < nq
    low = lax.broadcasted_iota(jnp.int32, (2 * nq, LANES), 1) < HEAD_DIM
    nt = (((1,), (1,)), ((), ()))
    for g in range(KV_HEADS):
        c0 = g * Q_GROUP * HEAD_DIM
        q2 = jnp.concatenate([q_ref[:, c0:c0 + LANES], q_ref[:, c0 + LANES:c0 + 2 * LANES]], axis=0)
        q2 = q2 * jnp.asarray(HEAD_DIM ** -0.5, bf16)
        parts = []
        for par in range(2):
            scores = []
            for piece, bias_fn in zip(kvar[g][par], bias_fns):
                s = lax.dot_general(q2, piece, nt, preferred_element_type=f32)
                scores.append(s if bias_fn is None else bias_fn(s))
            n_top = g * Q_GROUP + par
            sk = jnp.where(top, sink_ref[n_top], sink_ref[n_top + 2])
            parts.append(_sink_softmax(scores, sk))
        (p_e, d_e), (p_o, d_o) = parts
        o2 = None
        for probs, vals in ((p_e, vvar[g][0]), (p_o, vvar[g][1])):
            for p, v in zip(probs, vals):
                o2 = _dot(p, v) if o2 is None else o2 + _dot(p, v)
        o2 = o2 * jnp.where(low, 1.0 / d_e, 1.0 / d_o)
        o_ref[:, c0:c0 + LANES] = o2[:nq].astype(o_ref.dtype)
        o_ref[:, c0 + LANES:c0 + 2 * LANES] = o2[nq:].astype(o_ref.dtype)


def _ctx_attn_kernel(sink_ref, q_ref, k_ref, v_ref, o_ref):
    kvar = [[[v] for v in head] for head in _half_lane_variants(k_ref[...])]
    vvar = [[[v] for v in head] for head in _half_lane_variants(v_ref[...])]
    _sink_heads(sink_ref, q_ref, kvar, vvar, [None], o_ref)


def _ctx_attn_call(sink, q, k, v, *, batch, seq):
    return pl.pallas_call(
        _ctx_attn_kernel,
        out_shape=jax.ShapeDtypeStruct((batch * seq, ATTN_WIDTH), bf16),
        grid=(batch,),
        in_specs=[
            pl.BlockSpec(memory_space=pltpu.SMEM),
            pl.BlockSpec((seq, ATTN_WIDTH), lambda b: (b, 0)),
            pl.BlockSpec((seq, KV_WIDTH), lambda b: (b, 0)),
            pl.BlockSpec((seq, KV_WIDTH), lambda b: (b, 0)),
        ],
        out_specs=pl.BlockSpec((seq, ATTN_WIDTH), lambda b: (b, 0)),
        compiler_params=_cparams(("parallel",), 32),
        name="ctx_attn",
    )(sink, q, k, v)


ATTN_PREP_ROWS = 256


def _lat_attn_kernel(sink_ref, q_ref, k_ref, v_ref, kx_ref, vx_ref, o_ref, kvar_ref, vvar_ref,
                     *, n_blocks, past):
    n = pl.program_id(1)
    seq = n_blocks * BLOCK
    ctx0 = seq + 2 * BLOCK

    @pl.when(n == 0)
    def _():
        zeros = jnp.zeros((BLOCK, KV_WIDTH), bf16)
        for src, ctx, dst in ((k_ref, kx_ref, kvar_ref), (v_ref, vx_ref, vvar_ref)):
            for i in range(2 * KV_HEADS):
                dst[i, 0:BLOCK, :] = zeros
                dst[i, BLOCK + seq:ctx0, :] = zeros
            for r0 in range(0, seq + past, ATTN_PREP_ROWS):
                rows = min(ATTN_PREP_ROWS, seq + past - r0)
                x = src[r0:r0 + rows, :] if r0 < seq else ctx[r0 - seq:r0 - seq + rows, :]
                d0 = BLOCK + r0 if r0 < seq else ctx0 + r0 - seq
                variants = _half_lane_variants(x)
                for g in range(KV_HEADS):
                    for par in range(2):
                        dst[2 * g + par, d0:d0 + rows, :] = variants[g][par]

    assert WINDOW == BLOCK
    band0 = pl.multiple_of(n * BLOCK, BLOCK)

    def pieces(ref):
        return [[[ref[2 * g + par, pl.ds(band0, 3 * BLOCK), :], ref[2 * g + par, ctx0:ctx0 + past, :]]
                 for par in range(2)] for g in range(KV_HEADS)]

    a_idx = lax.broadcasted_iota(jnp.int32, (2 * BLOCK, BLOCK), 0) % BLOCK
    c_idx = lax.broadcasted_iota(jnp.int32, (2 * BLOCK, BLOCK), 1)
    bias_p = jnp.where((c_idx >= a_idx) & (n > 0), 0.0, NEG).astype(f32)
    bias_n = jnp.where((c_idx <= a_idx) & (n < n_blocks - 1), 0.0, NEG).astype(f32)

    def band_bias(s):
        return jnp.concatenate([s[:, :BLOCK] + bias_p, s[:, BLOCK:2 * BLOCK],
                                s[:, 2 * BLOCK:] + bias_n], axis=1)

    _sink_heads(sink_ref, q_ref, pieces(kvar_ref), pieces(vvar_ref), [band_bias, None], o_ref)


def _lat_attn_call(sink, q, k, v, cache_k4, cache_v4, *, layer, batch, seq):
    nb = seq // BLOCK
    past = cache_k4.shape[2]
    assert seq % ATTN_PREP_ROWS == 0
    whole = pl.BlockSpec((seq, KV_WIDTH), lambda b, n: (b, 0))
    ctx = pl.BlockSpec((None, None, past, KV_WIDTH), lambda b, n: (b, layer, 0, 0))
    return pl.pallas_call(
        functools.partial(_lat_attn_kernel, n_blocks=nb, past=past),
        out_shape=jax.ShapeDtypeStruct((batch * seq, ATTN_WIDTH), bf16),
        grid=(batch, nb),
        in_specs=[
            pl.BlockSpec(memory_space=pltpu.SMEM),
            pl.BlockSpec((BLOCK, ATTN_WIDTH), lambda b, n: (b * nb + n, 0)),
            whole, whole, ctx, ctx,
        ],
        out_specs=pl.BlockSpec((BLOCK, ATTN_WIDTH), lambda b, n: (b * nb + n, 0)),
        scratch_shapes=[pltpu.VMEM((2 * KV_HEADS, seq + 2 * BLOCK + past, KV_WIDTH), bf16)] * 2,
        compiler_params=_cparams(("parallel", "arbitrary"), 32),
        name="lat_attn",
    )(sink, q, k, v, cache_k4, cache_v4)


def _softplus(x):
    return jnp.maximum(x, 0.0) + jnp.log1p(jnp.exp(-jnp.abs(x)))


def _gelu_tanh(x):
    return 0.5 * x * (1.0 + jnp.tanh(math.sqrt(2.0 / math.pi) * (x + 0.044715 * (x * x * x))))


def _seg_pitch(seg):
    assert seg % SUBLANES == 0
    tiles = seg // SUBLANES
    return seg if tiles % 2 else seg + SUBLANES


def _seg_row(t, seg, pitch):
    return (t // seg) * pitch + t % seg


def _lru_kernel(xr_ref, yr_ref, cw_ref, cb_ref, wd_ref, bias_ref, lam_ref, h0_ref,
                rec_ref, st_ref, xs_ref, af_ref, bf_ref, ab_ref, bb_ref, *, seq):
    seg = seq // LRU_SEGMENTS
    chunk = min(seq, 256)
    hw = LRU_HALF
    n_lt = hw // LANES
    pitch = _seg_pitch(seg)
    piece = min(seg, chunk)

    zeros_pad = jnp.zeros((PAD_ROWS, hw), f32)
    xs_ref[0:PAD_ROWS, :] = zeros_pad
    xs_ref[PAD_ROWS + seq:2 * PAD_ROWS + seq, :] = zeros_pad
    for c in range(seq // chunk):
        xs_ref[PAD_ROWS + c * chunk:PAD_ROWS + (c + 1) * chunk, :] = xr_ref[c * chunk:(c + 1) * chunk, :]

    neg_c_sp = [-LRU_C * _softplus(-lam_ref[d:d + 1, :]) for d in range(2)]
    a_refs = (af_ref, ab_ref)
    b_refs = (bf_ref, bb_ref)

    for c in range(seq // chunk):
        r0 = c * chunk
        xc = cb_ref[...]
        for j in range(CONV_W):
            s0 = PAD_ROWS + r0 + j - CONV_LEFT
            xc = xc + xs_ref[s0:s0 + chunk, :] * cw_ref[j:j + 1, :]
        y = _dot(xc.astype(bf16), wd_ref[...]) + bias_ref[...]
        for d in range(2):
            r = _sigmoid(y[:, (2 * d) * hw:(2 * d + 1) * hw])
            gi = _sigmoid(y[:, (2 * d + 1) * hw:(2 * d + 2) * hw])
            log_a = neg_c_sp[d] * r
            a = jnp.exp(log_a)
            z = -jnp.tanh(log_a) * (1.0 + a * a)
            b = jnp.where(z > 0.0, z * lax.rsqrt(z), 0.0) * (gi * xc)
            for p0 in range(0, chunk, piece):
                dst = _seg_row(r0 + p0, seg, pitch)
                for t in range(n_lt):
                    lanes = slice(t * LANES, (t + 1) * LANES)
                    a_refs[d][t, dst:dst + piece, :] = a[p0:p0 + piece, lanes]
                    b_refs[d][t, dst:dst + piece, :] = b[p0:p0 + piece, lanes]

    def step(j, carry):
        jf = pl.ds(j, LRU_SEGMENTS, stride=pitch)
        jb = pl.ds(seg - 1 - j, LRU_SEGMENTS, stride=pitch)
        out = []
        for t in range(n_lt):
            hf, pf, hb, pb = carry[4 * t:4 * t + 4]
            a = af_ref[t, jf, :]
            hf = a * hf + bf_ref[t, jf, :]
            pf = a * pf
            bf_ref[t, jf, :] = hf
            af_ref[t, jf, :] = pf
            a = ab_ref[t, jb, :]
            hb = a * hb + bb_ref[t, jb, :]
            pb = a * pb
            bb_ref[t, jb, :] = hb
            ab_ref[t, jb, :] = pb
            out += [hf, pf, hb, pb]
        return tuple(out)

    z = jnp.zeros((LRU_SEGMENTS, LANES), f32)
    o = jnp.ones((LRU_SEGMENTS, LANES), f32)
    fin = lax.fori_loop(0, seg, step, (z, o, z, o) * n_lt, unroll=2)

    sub = min(seg, 256)
    for t in range(n_lt):
        hf, pf, hb, pb = fin[4 * t:4 * t + 4]
        lanes = slice(t * LANES, (t + 1) * LANES)
        cf = h0_ref[0:1, lanes]
        carry_f = []
        for s in range(LRU_SEGMENTS):
            carry_f.append(cf)
            cf = hf[s:s + 1, :] + pf[s:s + 1, :] * cf
        cb = h0_ref[1:2, lanes]
        carry_b = [None] * LRU_SEGMENTS
        for s in reversed(range(LRU_SEGMENTS)):
            carry_b[s] = cb
            cb = hb[s:s + 1, :] + pb[s:s + 1, :] * cb
        st_ref[0:1, lanes] = cf
        st_ref[1:2, lanes] = cb
        for s in range(LRU_SEGMENTS):
            for u in range(seg // sub):
                rows = slice(s * seg + u * sub, s * seg + (u + 1) * sub)
                src = slice(s * pitch + u * sub, s * pitch + (u + 1) * sub)
                h = ((bf_ref[t, src, :] + af_ref[t, src, :] * carry_f[s])
                     + (bb_ref[t, src, :] + ab_ref[t, src, :] * carry_b[s]))
                rec_ref[rows, lanes] = (h * _gelu_tanh(yr_ref[rows, lanes])).astype(rec_ref.dtype)


def _lru_call(xr, yr, conv_w, conv_b, wd, bias, lam, h0, *, layer, batch, seq):
    hw = LRU_HALF
    return pl.pallas_call(
        functools.partial(_lru_kernel, seq=seq),
        out_shape=[jax.ShapeDtypeStruct((batch * seq, LRU_WIDTH), bf16),
                   jax.ShapeDtypeStruct((batch, 2, LRU_WIDTH), f32)],
        grid=(batch, 2),
        in_specs=[
            pl.BlockSpec((seq, hw), lambda b, c: (b, c)),
            pl.BlockSpec((seq, hw), lambda b, c: (b, c)),
            pl.BlockSpec((None, CONV_W, hw), lambda b, c: (layer, 0, c)),
            pl.BlockSpec((None, 1, hw), lambda b, c: (layer, 0, c)),
            pl.BlockSpec((None, None, hw, 4 * hw), lambda b, c: (layer, c, 0, 0)),
            pl.BlockSpec((None, None, 1, 4 * hw), lambda b, c: (layer, c, 0, 0)),
            pl.BlockSpec((None, 2, hw), lambda b, c: (layer, 0, c)),
            pl.BlockSpec((None, 2, hw), lambda b, c: (b, 0, c)),
        ],
        out_specs=[pl.BlockSpec((seq, hw), lambda b, c: (b, c)),
                   pl.BlockSpec((None, 2, hw), lambda b, c: (b, 0, c))],
        scratch_shapes=([pltpu.VMEM((seq + 2 * PAD_ROWS, hw), f32)]
                        + [pltpu.VMEM((hw // LANES, LRU_SEGMENTS * _seg_pitch(seq // LRU_SEGMENTS), LANES),
                                      f32)] * 4),
        compiler_params=_cparams(("parallel", "parallel"), 40),
        name=f"lru_{seq}",
    )(xr, yr, conv_w, conv_b, wd, bias, lam, h0)


FOURIER_REV = 128
FOURIER_PAD = 16
FOURIER_STEP_ROWS = 2048


def _fourier_kernel(x_ref, csc_ref, ch_ref, sh_ref, rev_ref, o_ref, u_ref, *, seq, per_step, scale):
    half = seq // 2
    chunk = min(seq, 512)
    w = FOURIER_WIDTH
    for s in range(per_step):
        for c in range(seq // chunk):
            rows = slice(s * seq + c * chunk, s * seq + (c + 1) * chunk)
            for g in range(FOURIER_GROUPS):
                cols = slice(g * FOURIER_GW, (g + 1) * FOURIER_GW)
                dst = slice(s * w + g * FOURIER_GW, s * w + (g + 1) * FOURIER_GW)
                u = _dot(x_ref[rows, cols], csc_ref[...])
                u_ref[c * chunk:(c + 1) * chunk, dst] = u[:, :FOURIER_GW].astype(bf16)
                u_ref[seq + c * chunk:seq + (c + 1) * chunk, dst] = u[:, FOURIER_GW:].astype(bf16)
    a = _dot(ch_ref[...], u_ref[0:seq, :])
    b = _dot(sh_ref[...], u_ref[seq:2 * seq, :])
    top = ((a[0:half] - b[0:half]) * scale).astype(o_ref.dtype)
    z = ((a[1:half + 1] + b[1:half + 1]) * scale).astype(bf16)
    nb = half // FOURIER_REV
    for s in range(per_step):
        cols = slice(s * w, (s + 1) * w)
        o_ref[s * seq:s * seq + half, :] = top[:, cols]
        for k in range(nb):
            blk = z[(nb - 1 - k) * FOURIER_REV:(nb - k) * FOURIER_REV, cols]
            r0 = s * seq + half + k * FOURIER_REV
            o_ref[r0:r0 + FOURIER_REV, :] = _dot(rev_ref[...], blk).astype(o_ref.dtype)


def _dft_tables(n):
    k = np.arange(n, dtype=np.int64)
    ang = (2.0 * np.pi / n) * ((k[:, None] * k[None, :]) % n).astype(np.float64)
    return np.cos(ang), np.sin(ang)


def _fourier_call(xf, csc, ch, sh, rev, *, batch, seq):
    scale = 1.0 / math.sqrt(seq * FOURIER_GW)
    rows = seq // 2 + FOURIER_PAD
    per_step = max(1, min(batch, FOURIER_STEP_ROWS // seq))
    assert batch % per_step == 0

    def const(shape):
        return pl.BlockSpec(shape, lambda b: (0, 0), pipeline_mode=pl.Buffered(1))

    return pl.pallas_call(
        functools.partial(_fourier_kernel, seq=seq, per_step=per_step, scale=scale),
        out_shape=jax.ShapeDtypeStruct((batch * seq, FOURIER_WIDTH), bf16),
        grid=(batch // per_step,),
        in_specs=[
            pl.BlockSpec((per_step * seq, FOURIER_WIDTH), lambda b: (b, 0)),
            const((FOURIER_GW, 2 * FOURIER_GW)),
            const((rows, seq)),
            const((rows, seq)),
            const((FOURIER_REV, FOURIER_REV)),
        ],
        out_specs=pl.BlockSpec((per_step * seq, FOURIER_WIDTH), lambda b: (b, 0)),
        scratch_shapes=[pltpu.VMEM((2 * seq, per_step * FOURIER_WIDTH), bf16)],
        compiler_params=_cparams(("parallel",), 40),
        name=f"fourier_{seq}",
    )(xf, csc, ch, sh, rev)


def _merge_kernel(x_ref, sh_ref, sc_ref, gt_ref, nw_ref, at_ref, rc_ref, fr_ref,
                  wbg_ref, bbg_ref, wao_ref, wlo_ref, wfo_ref, wo_ref, o_ref):
    x = x_ref[...]
    h = _modnorm(x, nw_ref[...], sh_ref[...], sc_ref[...]).astype(bf16)
    merged = None
    for idx, (br_ref, w_ref) in enumerate(((at_ref, wao_ref), (rc_ref, wlo_ref), (fr_ref, wfo_ref))):
        cols = slice(idx * D_MODEL, (idx + 1) * D_MODEL)
        g = _sigmoid(_dot(h, wbg_ref[:, cols]) + bbg_ref[:, cols])
        y = g * _dot(br_ref[...], w_ref[...])
        merged = y if merged is None else merged + y
    out = _dot(merged.astype(bf16), wo_ref[...])
    o_ref[...] = x + gt_ref[...] * out


def _merge_call(x, mod5, norm_w4, attn, rec, four, wbg, bbg, wao, wlo, wfo, wo, *, layer, row_fn, tm, name):
    n_tok = x.shape[0]

    def const(shape):
        nd = len(shape)
        return pl.BlockSpec((None,) + shape, lambda i: (layer,) + (0,) * nd,
                            pipeline_mode=pl.Buffered(1))

    def branch():
        return pl.BlockSpec((tm, ATTN_WIDTH), lambda i: (i, 0))

    return pl.pallas_call(
        _merge_kernel,
        out_shape=jax.ShapeDtypeStruct((n_tok, D_MODEL), f32),
        grid=(n_tok // tm,),
        in_specs=[
            pl.BlockSpec((tm, D_MODEL), lambda i: (i, 0)),
            _mod_spec(layer, 3, row_fn),
            _mod_spec(layer, 4, row_fn),
            _mod_spec(layer, 5, row_fn),
            _vec_spec(layer, 1),
            branch(), branch(), branch(),
            const((D_MODEL, N_BRANCH * D_MODEL)),
            const((1, N_BRANCH * D_MODEL)),
            const((ATTN_WIDTH, D_MODEL)),
            const((LRU_WIDTH, D_MODEL)),
            const((FOURIER_WIDTH, D_MODEL)),
            const((D_MODEL, D_MODEL)),
        ],
        out_specs=pl.BlockSpec((tm, D_MODEL), lambda i: (i, 0)),
        compiler_params=_cparams(("parallel",), 48),
        name=name,
    )(x, mod5, mod5, mod5, norm_w4, attn, rec, four, wbg, bbg, wao, wlo, wfo, wo)


def _rope_lane_tables(n_tokens):
    rows = n_tokens // GRID_W
    row = jnp.repeat(jnp.arange(rows), GRID_W).astype(f32)
    col = jnp.tile(jnp.arange(GRID_W), rows).astype(f32)
    inv = ROPE_BASE ** (-jnp.arange(0, AXIS_ROT, 2, dtype=f32) / AXIS_ROT)
    ang = jnp.stack([row[:, None] * inv, col[:, None] * inv], axis=1)
    cos, sin = jnp.cos(ang), jnp.sin(ang)
    cos_h = jnp.stack([cos, cos], axis=2).reshape(n_tokens, HEAD_DIM)
    sin_h = jnp.stack([-sin, sin], axis=2).reshape(n_tokens, HEAD_DIM)
    reps = LANES // HEAD_DIM
    return jnp.tile(cos_h, (1, reps)), jnp.tile(sin_h, (1, reps))


def _lru_dense_weights(lru_wa, lru_wi, lru_ba, lru_bi):
    per = LRU_BLOCKS // 2
    hw = LRU_HALF
    eye = jnp.eye(per, dtype=bf16)

    def dense(w, c):
        wh = w[:, :, c * per:(c + 1) * per].astype(bf16)
        d = wh[:, :, :, :, None, :] * eye[None, None, :, None, :, None]
        return d.reshape(DEPTH, 2, hw, hw)

    halves, biases = [], []
    for c in range(2):
        sl = slice(c * hw, (c + 1) * hw)
        da, di = dense(lru_wa, c), dense(lru_wi, c)
        halves.append(jnp.concatenate([da[:, 0], di[:, 0], da[:, 1], di[:, 1]], axis=-1))
        biases.append(jnp.concatenate(
            [lru_ba[:, 0, sl], lru_bi[:, 0, sl], lru_ba[:, 1, sl], lru_bi[:, 1, sl]], axis=-1))
    wd = jnp.stack(halves, axis=1)
    bias = jnp.stack(biases, axis=1)[:, :, None, :]
    return wd, bias


def kernel(x_prompt, x_sample, c, cache_k, cache_v, state_lru, c_ctx, w_ada, b_ada, norm_w, final_norm_w,
           ffn1_wg, ffn1_wu, ffn1_wd, ffn2_wg, ffn2_wu, ffn2_wd, w_in, w_branch_gate, b_branch_gate,
           attn_sink, w_attn_out, conv_w, conv_b, lru_wa, lru_ba, lru_wi, lru_bi, lru_lambda,
           w_lru_out, w_fourier_out, w_o):
    batch, seq, _ = x_prompt.shape
    dec_batch, dec_seq, _ = x_sample.shape
    past = cache_k.shape[2]
    assert 1 + dec_batch <= COND_ROWS

    cond = jnp.concatenate([c_ctx[None, :], c, jnp.zeros((COND_ROWS - 1 - dec_batch, D_MODEL), f32)], axis=0)
    mod = _ada_call(cond, w_ada, b_ada)
    mod5 = mod.reshape(DEPTH, COND_ROWS, N_SUB * 3, D_MODEL).transpose(0, 2, 1, 3)[:, :, :, None, :]
    norm_w4 = norm_w[:, :, None, :]
    final_w = final_norm_w[None, :]

    w_in_bf = w_in.astype(bf16)
    wbg_bf = w_branch_gate.astype(bf16)
    bbg = b_branch_gate[:, None, :]
    wao_bf = w_attn_out.astype(bf16)
    wlo_bf = w_lru_out.astype(bf16)
    wfo_bf = w_fourier_out.astype(bf16)
    wo_bf = w_o.astype(bf16)
    lru_wd, lru_bias = _lru_dense_weights(lru_wa, lru_wi, lru_ba, lru_bi)
    conv_b3 = conv_b[:, None, :]
    rope_tabs = _rope_lane_tables(dec_seq)
    cc, sc_ = _dft_tables(FOURIER_GW)
    csc = jnp.asarray(np.concatenate([cc, sc_], axis=1), f32).astype(bf16)
    dft = {}
    for n in (seq, dec_seq):
        cl, sl = _dft_tables(n)
        rows = n // 2 + FOURIER_PAD
        dft[n] = (jnp.asarray(cl[:rows], f32).astype(bf16), jnp.asarray(sl[:rows], f32).astype(bf16))
    rev = jnp.asarray(np.eye(FOURIER_REV)[::-1], f32).astype(bf16)
    cache_k4 = cache_k.reshape(dec_batch, DEPTH, past, KV_WIDTH)
    cache_v4 = cache_v.reshape(dec_batch, DEPTH, past, KV_WIDTH)
    h0_ctx = jnp.zeros((batch, 2, LRU_WIDTH), f32)

    tm_p, tm_s = 2048, 2048
    row_p = lambda i: 0
    row_s = lambda i: 1 + (i * tm_s) // dec_seq
    tmi = 512
    row_pi = lambda i: 0
    row_si = lambda i: 1 + (i * tmi) // dec_seq

    xp = x_prompt.reshape(batch * seq, D_MODEL)
    xs = x_sample.reshape(dec_batch * dec_seq, D_MODEL)
    ks, vs, ss = [], [], []
    for l in range(DEPTH):
        last = l == DEPTH - 1
        sink = attn_sink[l]
        xp = _ffn_call(xp, mod5, norm_w4, ffn1_wg, ffn1_wu, ffn1_wd, None,
                       layer=l, sub=0, row_fn=row_p, tm=tm_p, name="ffn1_ctx")
        xs = _ffn_call(xs, mod5, norm_w4, ffn1_wg, ffn1_wu, ffn1_wd, None,
                       layer=l, sub=0, row_fn=row_s, tm=tm_s, name="ffn1_lat")
        q, k, v, xr, yr, xf = _inproj_call(xp, mod5, norm_w4, w_in_bf, None,
                                           layer=l, row_fn=row_pi, tm=tmi, seq=seq, name="inproj_ctx")
        ks.append(k.reshape(batch, seq, KV_HEADS, HEAD_DIM))
        vs.append(v.reshape(batch, seq, KV_HEADS, HEAD_DIM))
        attn = _ctx_attn_call(sink, q, k, v, batch=batch, seq=seq)
        rec, st = _lru_call(xr, yr, conv_w, conv_b3, lru_wd, lru_bias, lru_lambda, h0_ctx,
                            layer=l, batch=batch, seq=seq)
        ss.append(st)
        four = _fourier_call(xf, csc, *dft[seq], rev, batch=batch, seq=seq)
        xp = _merge_call(xp, mod5, norm_w4, attn, rec, four, wbg_bf, bbg, wao_bf, wlo_bf, wfo_bf, wo_bf,
                         layer=l, row_fn=row_pi, tm=tmi, name="merge_ctx")
        q, k, v, xr, yr, xf = _inproj_call(xs, mod5, norm_w4, w_in_bf, rope_tabs,
                                           layer=l, row_fn=row_si, tm=tmi, seq=dec_seq, name="inproj_lat")
        attn = _lat_attn_call(sink, q, k, v, cache_k4, cache_v4, layer=l, batch=dec_batch, seq=dec_seq)
        rec, _ = _lru_call(xr, yr, conv_w, conv_b3, lru_wd, lru_bias, lru_lambda, state_lru[:, l],
                           layer=l, batch=dec_batch, seq=dec_seq)
        four = _fourier_call(xf, csc, *dft[dec_seq], rev, batch=dec_batch, seq=dec_seq)
        xs = _merge_call(xs, mod5, norm_w4, attn, rec, four, wbg_bf, bbg, wao_bf, wlo_bf, wfo_bf, wo_bf,
                         layer=l, row_fn=row_si, tm=tmi, name="merge_lat")
        xp = _ffn_call(xp, mod5, norm_w4, ffn2_wg, ffn2_wu, ffn2_wd, final_w if last else None,
                       layer=l, sub=2, row_fn=row_p, tm=tm_p, name="ffn2_ctx")
        xs = _ffn_call(xs, mod5, norm_w4, ffn2_wg, ffn2_wu, ffn2_wd, final_w if last else None,
                       layer=l, sub=2, row_fn=row_s, tm=tm_s, name="ffn2_lat")

    y_prompt = xp.reshape(batch, seq, D_MODEL)
    y_sample = xs.reshape(dec_batch, dec_seq, D_MODEL)
    return (y_prompt, y_sample, jnp.stack(ks, axis=1), jnp.stack(vs, axis=1), jnp.stack(ss, axis=1))
```

```python
import functools
import math

import numpy as np
import jax
import jax.numpy as jnp
from jax import lax
from jax.experimental import pallas as pl
from jax.experimental.pallas import tpu as pltpu

f32 = jnp.float32
bf16 = jnp.bfloat16

D_MODEL = 1024
DEPTH = 2
GRID_W = 64
N_HEADS = 8
KV_HEADS = 2
HEAD_DIM = 64
Q_GROUP = N_HEADS // KV_HEADS
ATTN_WIDTH = N_HEADS * HEAD_DIM
KV_WIDTH = KV_HEADS * HEAD_DIM
WINDOW = 128
BLOCK = 128
AXIS_ROT = HEAD_DIM // 2
ROPE_BASE = 10000.0
LRU_WIDTH = 512
LRU_BLOCKS = 8
LRU_BW = LRU_WIDTH // LRU_BLOCKS
LRU_C = 8.0
CONV_W = 4
CONV_LEFT = 2
FOURIER_WIDTH = 512
FOURIER_GROUPS = 4
FOURIER_GW = FOURIER_WIDTH // FOURIER_GROUPS
D_FF = 2816
N_BRANCH = 3
N_SUB = 3
EPS = 1e-6
NEG = -1e30
IN_WIDTH = ATTN_WIDTH + 2 * KV_WIDTH + 2 * LRU_WIDTH + FOURIER_WIDTH

V7X_VMEM_BYTES = 64 * 1024 * 1024
SUBLANES = 8
LANES = 128
COND_ROWS = 8
LRU_HALF = LRU_WIDTH // 2
LRU_SEGMENTS = SUBLANES
PAD_ROWS = SUBLANES

_MIB = 1024 * 1024


def _cparams(sem, vmem_mib):
    del vmem_mib
    return pltpu.CompilerParams(dimension_semantics=sem, vmem_limit_bytes=V7X_VMEM_BYTES)


def _dot(a, b):
    return jnp.dot(a, b, preferred_element_type=f32)


def _sigmoid(x):
    return jax.nn.sigmoid(x)


def _modnorm(x, nw, shift, scale):
    y = x * lax.rsqrt(jnp.mean(x * x, axis=-1, keepdims=True) + EPS)
    return y * (nw * (1.0 + scale)) + shift


def _ada_kernel(cond_ref, w_ref, b_ref, o_ref):
    c = cond_ref[...]
    s = (c * _sigmoid(c)).astype(bf16)
    o_ref[...] = _dot(s, w_ref[...].astype(bf16)) + b_ref[...]


def _ada_call(cond, w_ada, b_ada):
    n_out = w_ada.shape[-1]
    tn = 1024
    return pl.pallas_call(
        _ada_kernel,
        out_shape=jax.ShapeDtypeStruct((DEPTH, COND_ROWS, n_out), f32),
        grid=(DEPTH, n_out // tn),
        in_specs=[
            pl.BlockSpec((COND_ROWS, D_MODEL), lambda l, j: (0, 0)),
            pl.BlockSpec((None, D_MODEL, tn), lambda l, j: (l, 0, j)),
            pl.BlockSpec((None, 1, tn), lambda l, j: (l, 0, j)),
        ],
        out_specs=pl.BlockSpec((None, COND_ROWS, tn), lambda l, j: (l, 0, j)),
        compiler_params=_cparams(("parallel", "parallel"), 24),
        name="adaln",
    )(cond, w_ada, b_ada.reshape(DEPTH, 1, n_out))


def _mod_spec(layer, slot, row_fn):
    return pl.BlockSpec((None, None, None, 1, D_MODEL),
                        lambda i, *_: (layer, slot, row_fn(i), 0, 0))


def _vec_spec(layer, sub):
    return pl.BlockSpec((None, None, 1, D_MODEL), lambda i, *_: (layer, sub, 0, 0))


def _ffn_kernel(*refs, n_ff, final):
    if final:
        (x_ref, sh_ref, sc_ref, gt_ref, nw_ref, wg_ref, wu_ref, wd_ref, fw_ref,
         o_ref, h_ref) = refs
    else:
        (x_ref, sh_ref, sc_ref, gt_ref, nw_ref, wg_ref, wu_ref, wd_ref,
         o_ref, h_ref) = refs
    j = pl.program_id(1)

    @pl.when(j == 0)
    def _():
        h_ref[...] = _modnorm(x_ref[...], nw_ref[...], sh_ref[...], sc_ref[...]).astype(bf16)
        o_ref[...] = jnp.zeros_like(o_ref)

    h = h_ref[...]
    g = _dot(h, wg_ref[...].astype(bf16))
    u = _dot(h, wu_ref[...].astype(bf16))
    a = ((g * _sigmoid(g)) * u).astype(bf16)
    o_ref[...] += _dot(a, wd_ref[...].astype(bf16))

    @pl.when(j == n_ff - 1)
    def _():
        y = x_ref[...] + (0.5 * gt_ref[...]) * o_ref[...]
        if final:
            y = y * lax.rsqrt(jnp.mean(y * y, axis=-1, keepdims=True) + EPS) * fw_ref[...]
        o_ref[...] = y


def _ffn_call(x, mod5, norm_w4, wg, wu, wd, final_w, *, layer, sub, row_fn, tm, name):
    n_tok = x.shape[0]
    tf = 256
    n_ff = D_FF // tf
    final = final_w is not None
    in_specs = [
        pl.BlockSpec((tm, D_MODEL), lambda i, j: (i, 0)),
        _mod_spec(layer, sub * 3 + 0, row_fn),
        _mod_spec(layer, sub * 3 + 1, row_fn),
        _mod_spec(layer, sub * 3 + 2, row_fn),
        _vec_spec(layer, sub),
        pl.BlockSpec((None, D_MODEL, tf), lambda i, j: (layer, 0, j)),
        pl.BlockSpec((None, D_MODEL, tf), lambda i, j: (layer, 0, j)),
        pl.BlockSpec((None, tf, D_MODEL), lambda i, j: (layer, j, 0)),
    ]
    args = [x, mod5, mod5, mod5, norm_w4, wg, wu, wd]
    if final:
        in_specs.append(pl.BlockSpec((1, D_MODEL), lambda i, j: (0, 0)))
        args.append(final_w)
    return pl.pallas_call(
        functools.partial(_ffn_kernel, n_ff=n_ff, final=final),
        out_shape=jax.ShapeDtypeStruct((n_tok, D_MODEL), f32),
        grid=(n_tok // tm, n_ff),
        in_specs=in_specs,
        out_specs=pl.BlockSpec((tm, D_MODEL), lambda i, j: (i, 0)),
        scratch_shapes=[pltpu.VMEM((tm, D_MODEL), bf16)],
        compiler_params=_cparams(("parallel", "arbitrary"), 52),
        name=name,
    )(*args)


def _swap16(x):
    n = x.shape[-1]
    lane = lax.broadcasted_iota(jnp.int32, x.shape, x.ndim - 1)
    first = (lane % AXIS_ROT) < (AXIS_ROT // 2)
    return jnp.where(first, pltpu.roll(x, n - AXIS_ROT // 2, x.ndim - 1),
                     pltpu.roll(x, AXIS_ROT // 2, x.ndim - 1))


def _inproj_kernel(*refs, rope):
    if rope:
        (x_ref, sh_ref, sc_ref, nw_ref, w_ref, cos_ref, sin_ref,
         q_ref, k_ref, v_ref, xr_ref, yr_ref, xf_ref) = refs
    else:
        (x_ref, sh_ref, sc_ref, nw_ref, w_ref,
         q_ref, k_ref, v_ref, xr_ref, yr_ref, xf_ref) = refs
    h = _modnorm(x_ref[...], nw_ref[...], sh_ref[...], sc_ref[...]).astype(bf16)
    o = ATTN_WIDTH
    k = _dot(h, w_ref[:, o:o + KV_WIDTH])
    if rope:
        cos = cos_ref[...]
        sin = sin_ref[...]
        k = k * cos + _swap16(k) * sin
    k_ref[...] = k
    for c in range(ATTN_WIDTH // LANES):
        q = _dot(h, w_ref[:, c * LANES:(c + 1) * LANES])
        if rope:
            q = q * cos + _swap16(q) * sin
        q_ref[:, c * LANES:(c + 1) * LANES] = q.astype(bf16)
    o += KV_WIDTH
    v_ref[...] = _dot(h, w_ref[:, o:o + KV_WIDTH])
    o += KV_WIDTH
    xr_ref[...] = _dot(h, w_ref[:, o:o + LRU_WIDTH])
    o += LRU_WIDTH
    yr_ref[...] = _dot(h, w_ref[:, o:o + LRU_WIDTH])
    o += LRU_WIDTH
    xf_ref[...] = _dot(h, w_ref[:, o:o + FOURIER_WIDTH]).astype(bf16)


def _inproj_call(x, mod5, norm_w4, w_in_bf, rope_tabs, *, layer, row_fn, tm, seq, name):
    n_tok = x.shape[0]
    rope = rope_tabs is not None
    in_specs = [
        pl.BlockSpec((tm, D_MODEL), lambda i: (i, 0)),
        _mod_spec(layer, 3, row_fn),
        _mod_spec(layer, 4, row_fn),
        _vec_spec(layer, 1),
        pl.BlockSpec((None, D_MODEL, IN_WIDTH), lambda i: (layer, 0, 0)),
    ]
    args = [x, mod5, mod5, norm_w4, w_in_bf]
    if rope:
        per_seq = seq // tm
        in_specs += [pl.BlockSpec((tm, LANES), lambda i: (i % per_seq, 0))] * 2
        args += list(rope_tabs)
    widths = (ATTN_WIDTH, KV_WIDTH, KV_WIDTH, LRU_WIDTH, LRU_WIDTH, FOURIER_WIDTH)
    dtypes = (bf16, f32, f32, f32, f32, bf16)
    return pl.pallas_call(
        functools.partial(_inproj_kernel, rope=rope),
        out_shape=[jax.ShapeDtypeStruct((n_tok, w), d) for w, d in zip(widths, dtypes)],
        grid=(n_tok // tm,),
        in_specs=in_specs,
        out_specs=[pl.BlockSpec((tm, w), lambda i: (i, 0)) for w in widths],
        compiler_params=_cparams(("parallel",), 48),
        name=name,
    )(*args)


def _half_lane_variants(x):
    lane = lax.broadcasted_iota(jnp.int32, x.shape, 1)
    low = lane < HEAD_DIM
    sw = pltpu.roll(x, HEAD_DIM, 1)
    zero = jnp.zeros_like(x)
    head0 = (jnp.where(low, x, zero), jnp.where(low, zero, sw))
    head1 = (jnp.where(low, sw, zero), jnp.where(low, zero, x))
    return [tuple(v.astype(bf16) for v in h) for h in (head0, head1)]


def _sink_softmax(scores, sk):
    m = sk
    for s in scores:
        m = jnp.maximum(m, jnp.max(s, axis=-1, keepdims=True))
    denom = jnp.exp(sk - m)
    probs = []
    for s in scores:
        p = jnp.exp(s - m)
        denom = denom + jnp.sum(p, axis=-1, keepdims=True)
        probs.append(p.astype(bf16))
    return probs, denom


def _sink_heads(sink_ref, q_ref, kvar, vvar, bias_fns, o_ref):
    nq = q_ref.shape[0]
    top = lax.broadcasted_iota(jnp.int32, (2 * nq, 1), 0) < nq
    low = lax.broadcasted_iota(jnp.int32, (2 * nq, LANES), 1) < HEAD_DIM
    nt = (((1,), (1,)), ((), ()))
    for g in range(KV_HEADS):
        c0 = g * Q_GROUP * HEAD_DIM
        q2 = jnp.concatenate([q_ref[:, c0:c0 + LANES], q_ref[:, c0 + LANES:c0 + 2 * LANES]], axis=0)
        q2 = q2 * jnp.asarray(HEAD_DIM ** -0.5, bf16)
        parts = []
        for par in range(2):
            scores = []
            for piece, bias_fn in zip(kvar[g][par], bias_fns):
                s = lax.dot_general(q2, piece, nt, preferred_element_type=f32)
                scores.append(s if bias_fn is None else bias_fn(s))
            n_top = g * Q_GROUP + par
            sk = jnp.where(top, sink_ref[n_top], sink_ref[n_top + 2])
            parts.append(_sink_softmax(scores, sk))
        (p_e, d_e), (p_o, d_o) = parts
        o2 = None
        for probs, vals in ((p_e, vvar[g][0]), (p_o, vvar[g][1])):
            for p, v in zip(probs, vals):
                o2 = _dot(p, v) if o2 is None else o2 + _dot(p, v)
        o2 = o2 * jnp.where(low, 1.0 / d_e, 1.0 / d_o)
        o_ref[:, c0:c0 + LANES] = o2[:nq].astype(o_ref.dtype)
        o_ref[:, c0 + LANES:c0 + 2 * LANES] = o2[nq:].astype(o_ref.dtype)


CTX_ATTN_PER_STEP = 2


def _ctx_attn_kernel(sink_ref, q_ref, k_ref, v_ref, o_ref, *, seq):
    for s in range(CTX_ATTN_PER_STEP):
        rows = slice(s * seq, (s + 1) * seq)
        kvar = [[[v] for v in head] for head in _half_lane_variants(k_ref[rows, :])]
        vvar = [[[v] for v in head] for head in _half_lane_variants(v_ref[rows, :])]
        _sink_heads(sink_ref, q_ref.at[rows, :], kvar, vvar, [None], o_ref.at[rows, :])


def _ctx_attn_call(sink, q, k, v, *, batch, seq):
    per = CTX_ATTN_PER_STEP
    assert batch % per == 0
    return pl.pallas_call(
        functools.partial(_ctx_attn_kernel, seq=seq),
        out_shape=jax.ShapeDtypeStruct((batch * seq, ATTN_WIDTH), bf16),
        grid=(batch // per,),
        in_specs=[
            pl.BlockSpec(memory_space=pltpu.SMEM),
            pl.BlockSpec((per * seq, ATTN_WIDTH), lambda b: (b, 0)),
            pl.BlockSpec((per * seq, KV_WIDTH), lambda b: (b, 0)),
            pl.BlockSpec((per * seq, KV_WIDTH), lambda b: (b, 0)),
        ],
        out_specs=pl.BlockSpec((per * seq, ATTN_WIDTH), lambda b: (b, 0)),
        compiler_params=_cparams(("parallel",), 32),
        name="ctx_attn",
    )(sink, q, k, v)


ATTN_PREP_ROWS = 256


def _lat_attn_kernel(sink_ref, q_ref, k_ref, v_ref, kx_ref, vx_ref, o_ref, kvar_ref, vvar_ref,
                     *, n_blocks, past):
    n = pl.program_id(1)
    seq = n_blocks * BLOCK
    ctx0 = seq + 2 * BLOCK

    @pl.when(n == 0)
    def _():
        zeros = jnp.zeros((BLOCK, KV_WIDTH), bf16)
        for src, ctx, dst in ((k_ref, kx_ref, kvar_ref), (v_ref, vx_ref, vvar_ref)):
            for i in range(2 * KV_HEADS):
                dst[i, 0:BLOCK, :] = zeros
                dst[i, BLOCK + seq:ctx0, :] = zeros
            for r0 in range(0, seq + past, ATTN_PREP_ROWS):
                rows = min(ATTN_PREP_ROWS, seq + past - r0)
                x = src[r0:r0 + rows, :] if r0 < seq else ctx[r0 - seq:r0 - seq + rows, :]
                d0 = BLOCK + r0 if r0 < seq else ctx0 + r0 - seq
                variants = _half_lane_variants(x)
                for g in range(KV_HEADS):
                    for par in range(2):
                        dst[2 * g + par, d0:d0 + rows, :] = variants[g][par]

    assert WINDOW == BLOCK
    band0 = pl.multiple_of(n * BLOCK, BLOCK)

    def pieces(ref):
        return [[[ref[2 * g + par, pl.ds(band0, 3 * BLOCK), :], ref[2 * g + par, ctx0:ctx0 + past, :]]
                 for par in range(2)] for g in range(KV_HEADS)]

    a_idx = lax.broadcasted_iota(jnp.int32, (2 * BLOCK, BLOCK), 0) % BLOCK
    c_idx = lax.broadcasted_iota(jnp.int32, (2 * BLOCK, BLOCK), 1)
    bias_p = jnp.where((c_idx >= a_idx) & (n > 0), 0.0, NEG).astype(f32)
    bias_n = jnp.where((c_idx <= a_idx) & (n < n_blocks - 1), 0.0, NEG).astype(f32)

    def band_bias(s):
        return jnp.concatenate([s[:, :BLOCK] + bias_p, s[:, BLOCK:2 * BLOCK],
                                s[:, 2 * BLOCK:] + bias_n], axis=1)

    _sink_heads(sink_ref, q_ref, pieces(kvar_ref), pieces(vvar_ref), [band_bias, None], o_ref)


def _lat_attn_call(sink, q, k, v, cache_k4, cache_v4, *, layer, batch, seq):
    nb = seq // BLOCK
    past = cache_k4.shape[2]
    assert seq % ATTN_PREP_ROWS == 0
    whole = pl.BlockSpec((seq, KV_WIDTH), lambda b, n: (b, 0))
    ctx = pl.BlockSpec((None, None, past, KV_WIDTH), lambda b, n: (b, layer, 0, 0))
    return pl.pallas_call(
        functools.partial(_lat_attn_kernel, n_blocks=nb, past=past),
        out_shape=jax.ShapeDtypeStruct((batch * seq, ATTN_WIDTH), bf16),
        grid=(batch, nb),
        in_specs=[
            pl.BlockSpec(memory_space=pltpu.SMEM),
            pl.BlockSpec((BLOCK, ATTN_WIDTH), lambda b, n: (b * nb + n, 0)),
            whole, whole, ctx, ctx,
        ],
        out_specs=pl.BlockSpec((BLOCK, ATTN_WIDTH), lambda b, n: (b * nb + n, 0)),
        scratch_shapes=[pltpu.VMEM((2 * KV_HEADS, seq + 2 * BLOCK + past, KV_WIDTH), bf16)] * 2,
        compiler_params=_cparams(("parallel", "arbitrary"), 32),
        name="lat_attn",
    )(sink, q, k, v, cache_k4, cache_v4)


def _softplus(x):
    return jnp.maximum(x, 0.0) + jnp.log1p(jnp.exp(-jnp.abs(x)))


def _gelu_tanh(x):
    return 0.5 * x * (1.0 + jnp.tanh(math.sqrt(2.0 / math.pi) * (x + 0.044715 * (x * x * x))))


def _seg_pitch(seg):
    assert seg % SUBLANES == 0
    tiles = seg // SUBLANES
    return seg if tiles % 2 else seg + SUBLANES


def _seg_row(t, seg, pitch):
    return (t // seg) * pitch + t % seg


def _lru_kernel(xr_ref, yr_ref, cw_ref, cb_ref, wd_ref, bias_ref, lam_ref, h0_ref,
                rec_ref, st_ref, xs_ref, af_ref, bf_ref, ab_ref, bb_ref, *, seq):
    seg = seq // LRU_SEGMENTS
    chunk = min(seq, 256)
    hw = LRU_HALF
    n_lt = hw // LANES
    pitch = _seg_pitch(seg)
    piece = min(seg, chunk)

    zeros_pad = jnp.zeros((PAD_ROWS, hw), f32)
    xs_ref[0:PAD_ROWS, :] = zeros_pad
    xs_ref[PAD_ROWS + seq:2 * PAD_ROWS + seq, :] = zeros_pad
    for c in range(seq // chunk):
        xs_ref[PAD_ROWS + c * chunk:PAD_ROWS + (c + 1) * chunk, :] = xr_ref[c * chunk:(c + 1) * chunk, :]

    neg_c_sp = [-LRU_C * _softplus(-lam_ref[d:d + 1, :]) for d in range(2)]
    a_refs = (af_ref, ab_ref)
    b_refs = (bf_ref, bb_ref)

    for c in range(seq // chunk):
        r0 = c * chunk
        xc = cb_ref[...]
        for j in range(CONV_W):
            s0 = PAD_ROWS + r0 + j - CONV_LEFT
            xc = xc + xs_ref[s0:s0 + chunk, :] * cw_ref[j:j + 1, :]
        y = _dot(xc.astype(bf16), wd_ref[...]) + bias_ref[...]
        for d in range(2):
            r = _sigmoid(y[:, (2 * d) * hw:(2 * d + 1) * hw])
            gi = _sigmoid(y[:, (2 * d + 1) * hw:(2 * d + 2) * hw])
            log_a = neg_c_sp[d] * r
            a = jnp.exp(log_a)
            z = jnp.tanh(log_a) * (-1.0 - a * a)
            b = jnp.where(z > 0.0, z * lax.rsqrt(z), 0.0) * (gi * xc)
            for p0 in range(0, chunk, piece):
                dst = _seg_row(r0 + p0, seg, pitch)
                for t in range(n_lt):
                    lanes = slice(t * LANES, (t + 1) * LANES)
                    a_refs[d][t, dst:dst + piece, :] = a[p0:p0 + piece, lanes]
                    b_refs[d][t, dst:dst + piece, :] = b[p0:p0 + piece, lanes]

    def step(j, carry):
        jf = pl.ds(j, LRU_SEGMENTS, stride=pitch)
        jb = pl.ds(seg - 1 - j, LRU_SEGMENTS, stride=pitch)
        out = []
        for t in range(n_lt):
            hf, pf, hb, pb = carry[4 * t:4 * t + 4]
            a = af_ref[t, jf, :]
            hf = a * hf + bf_ref[t, jf, :]
            pf = a * pf
            bf_ref[t, jf, :] = hf
            af_ref[t, jf, :] = pf
            a = ab_ref[t, jb, :]
            hb = a * hb + bb_ref[t, jb, :]
            pb = a * pb
            bb_ref[t, jb, :] = hb
            ab_ref[t, jb, :] = pb
            out += [hf, pf, hb, pb]
        return tuple(out)

    z = jnp.zeros((LRU_SEGMENTS, LANES), f32)
    o = jnp.ones((LRU_SEGMENTS, LANES), f32)
    fin = lax.fori_loop(0, seg, step, (z, o, z, o) * n_lt, unroll=2)

    sub = min(seg, 256)
    for t in range(n_lt):
        hf, pf, hb, pb = fin[4 * t:4 * t + 4]
        lanes = slice(t * LANES, (t + 1) * LANES)
        cf = h0_ref[0:1, lanes]
        carry_f = []
        for s in range(LRU_SEGMENTS):
            carry_f.append(cf)
            cf = hf[s:s + 1, :] + pf[s:s + 1, :] * cf
        cb = h0_ref[1:2, lanes]
        carry_b = [None] * LRU_SEGMENTS
        for s in reversed(range(LRU_SEGMENTS)):
            carry_b[s] = cb
            cb = hb[s:s + 1, :] + pb[s:s + 1, :] * cb
        st_ref[0:1, lanes] = cf
        st_ref[1:2, lanes] = cb
        for s in range(LRU_SEGMENTS):
            for u in range(seg // sub):
                rows = slice(s * seg + u * sub, s * seg + (u + 1) * sub)
                src = slice(s * pitch + u * sub, s * pitch + (u + 1) * sub)
                h = ((bf_ref[t, src, :] + af_ref[t, src, :] * carry_f[s])
                     + (bb_ref[t, src, :] + ab_ref[t, src, :] * carry_b[s]))
                rec_ref[rows, lanes] = (h * _gelu_tanh(yr_ref[rows, lanes])).astype(rec_ref.dtype)


def _lru_call(xr, yr, conv_w, conv_b, wd, bias, lam, h0, *, layer, batch, seq):
    hw = LRU_HALF
    return pl.pallas_call(
        functools.partial(_lru_kernel, seq=seq),
        out_shape=[jax.ShapeDtypeStruct((batch * seq, LRU_WIDTH), bf16),
                   jax.ShapeDtypeStruct((batch, 2, LRU_WIDTH), f32)],
        grid=(batch, 2),
        in_specs=[
            pl.BlockSpec((seq, hw), lambda b, c: (b, c)),
            pl.BlockSpec((seq, hw), lambda b, c: (b, c)),
            pl.BlockSpec((None, CONV_W, hw), lambda b, c: (layer, 0, c)),
            pl.BlockSpec((None, 1, hw), lambda b, c: (layer, 0, c)),
            pl.BlockSpec((None, None, hw, 4 * hw), lambda b, c: (layer, c, 0, 0)),
            pl.BlockSpec((None, None, 1, 4 * hw), lambda b, c: (layer, c, 0, 0)),
            pl.BlockSpec((None, 2, hw), lambda b, c: (layer, 0, c)),
            pl.BlockSpec((None, 2, hw), lambda b, c: (b, 0, c)),
        ],
        out_specs=[pl.BlockSpec((seq, hw), lambda b, c: (b, c)),
                   pl.BlockSpec((None, 2, hw), lambda b, c: (b, 0, c))],
        scratch_shapes=([pltpu.VMEM((seq + 2 * PAD_ROWS, hw), f32)]
                        + [pltpu.VMEM((hw // LANES, LRU_SEGMENTS * _seg_pitch(seq // LRU_SEGMENTS), LANES),
                                      f32)] * 4),
        compiler_params=_cparams(("parallel", "parallel"), 40),
        name=f"lru_{seq}",
    )(xr, yr, conv_w, conv_b, wd, bias, lam, h0)


FOURIER_REV = 128
FOURIER_PAD = 16
FOURIER_STEP_ROWS = 2048


def _fourier_kernel(x_ref, csc_ref, ch_ref, sh_ref, rev_ref, o_ref, u_ref, *, seq, per_step, scale):
    half = seq // 2
    chunk = min(seq, 512)
    w = FOURIER_WIDTH
    for s in range(per_step):
        for c in range(seq // chunk):
            rows = slice(s * seq + c * chunk, s * seq + (c + 1) * chunk)
            for g in range(FOURIER_GROUPS):
                cols = slice(g * FOURIER_GW, (g + 1) * FOURIER_GW)
                dst = slice(s * w + g * FOURIER_GW, s * w + (g + 1) * FOURIER_GW)
                u = _dot(x_ref[rows, cols], csc_ref[...])
                u_ref[c * chunk:(c + 1) * chunk, dst] = u[:, :FOURIER_GW].astype(bf16)
                u_ref[seq + c * chunk:seq + (c + 1) * chunk, dst] = u[:, FOURIER_GW:].astype(bf16)
    a = _dot(ch_ref[...], u_ref[0:seq, :])
    b = _dot(sh_ref[...], u_ref[seq:2 * seq, :])
    top = ((a[0:half] - b[0:half]) * scale).astype(o_ref.dtype)
    z = ((a[1:half + 1] + b[1:half + 1]) * scale).astype(bf16)
    nb = half // FOURIER_REV
    for s in range(per_step):
        cols = slice(s * w, (s + 1) * w)
        o_ref[s * seq:s * seq + half, :] = top[:, cols]
        for k in range(nb):
            blk = z[(nb - 1 - k) * FOURIER_REV:(nb - k) * FOURIER_REV, cols]
            r0 = s * seq + half + k * FOURIER_REV
            o_ref[r0:r0 + FOURIER_REV, :] = _dot(rev_ref[...], blk).astype(o_ref.dtype)


def _dft_tables(n):
    k = np.arange(n, dtype=np.int64)
    ang = (2.0 * np.pi / n) * ((k[:, None] * k[None, :]) % n).astype(np.float64)
    return np.cos(ang), np.sin(ang)


def _fourier_call(xf, csc, ch, sh, rev, *, batch, seq):
    scale = 1.0 / math.sqrt(seq * FOURIER_GW)
    rows = seq // 2 + FOURIER_PAD
    per_step = max(1, min(batch, FOURIER_STEP_ROWS // seq))
    assert batch % per_step == 0

    def const(shape):
        return pl.BlockSpec(shape, lambda b: (0, 0), pipeline_mode=pl.Buffered(1))

    return pl.pallas_call(
        functools.partial(_fourier_kernel, seq=seq, per_step=per_step, scale=scale),
        out_shape=jax.ShapeDtypeStruct((batch * seq, FOURIER_WIDTH), bf16),
        grid=(batch // per_step,),
        in_specs=[
            pl.BlockSpec((per_step * seq, FOURIER_WIDTH), lambda b: (b, 0)),
            const((FOURIER_GW, 2 * FOURIER_GW)),
            const((rows, seq)),
            const((rows, seq)),
            const((FOURIER_REV, FOURIER_REV)),
        ],
        out_specs=pl.BlockSpec((per_step * seq, FOURIER_WIDTH), lambda b: (b, 0)),
        scratch_shapes=[pltpu.VMEM((2 * seq, per_step * FOURIER_WIDTH), bf16)],
        compiler_params=_cparams(("parallel",), 40),
        name=f"fourier_{seq}",
    )(xf, csc, ch, sh, rev)


def _merge_kernel(x_ref, sh_ref, sc_ref, gt_ref, nw_ref, at_ref, rc_ref, fr_ref,
                  wbg_ref, bbg_ref, wao_ref, wlo_ref, wfo_ref, wo_ref, o_ref):
    x = x_ref[...]
    h = _modnorm(x, nw_ref[...], sh_ref[...], sc_ref[...]).astype(bf16)
    merged = None
    for idx, (br_ref, w_ref) in enumerate(((at_ref, wao_ref), (rc_ref, wlo_ref), (fr_ref, wfo_ref))):
        cols = slice(idx * D_MODEL, (idx + 1) * D_MODEL)
        g = _sigmoid(_dot(h, wbg_ref[:, cols]) + bbg_ref[:, cols])
        y = g * _dot(br_ref[...], w_ref[...])
        merged = y if merged is None else merged + y
    out = _dot(merged.astype(bf16), wo_ref[...])
    o_ref[...] = x + gt_ref[...] * out


def _merge_call(x, mod5, norm_w4, attn, rec, four, wbg, bbg, wao, wlo, wfo, wo, *, layer, row_fn, tm, name):
    n_tok = x.shape[0]

    def const(shape):
        nd = len(shape)
        return pl.BlockSpec((None,) + shape, lambda i: (layer,) + (0,) * nd,
                            pipeline_mode=pl.Buffered(1))

    def branch():
        return pl.BlockSpec((tm, ATTN_WIDTH), lambda i: (i, 0))

    return pl.pallas_call(
        _merge_kernel,
        out_shape=jax.ShapeDtypeStruct((n_tok, D_MODEL), f32),
        grid=(n_tok // tm,),
        in_specs=[
            pl.BlockSpec((tm, D_MODEL), lambda i: (i, 0)),
            _mod_spec(layer, 3, row_fn),
            _mod_spec(layer, 4, row_fn),
            _mod_spec(layer, 5, row_fn),
            _vec_spec(layer, 1),
            branch(), branch(), branch(),
            const((D_MODEL, N_BRANCH * D_MODEL)),
            const((1, N_BRANCH * D_MODEL)),
            const((ATTN_WIDTH, D_MODEL)),
            const((LRU_WIDTH, D_MODEL)),
            const((FOURIER_WIDTH, D_MODEL)),
            const((D_MODEL, D_MODEL)),
        ],
        out_specs=pl.BlockSpec((tm, D_MODEL), lambda i: (i, 0)),
        compiler_params=_cparams(("parallel",), 48),
        name=name,
    )(x, mod5, mod5, mod5, norm_w4, attn, rec, four, wbg, bbg, wao, wlo, wfo, wo)


def _rope_lane_tables(n_tokens):
    rows = n_tokens // GRID_W
    row = jnp.repeat(jnp.arange(rows), GRID_W).astype(f32)
    col = jnp.tile(jnp.arange(GRID_W), rows).astype(f32)
    inv = ROPE_BASE ** (-jnp.arange(0, AXIS_ROT, 2, dtype=f32) / AXIS_ROT)
    ang = jnp.stack([row[:, None] * inv, col[:, None] * inv], axis=1)
    cos, sin = jnp.cos(ang), jnp.sin(ang)
    cos_h = jnp.stack([cos, cos], axis=2).reshape(n_tokens, HEAD_DIM)
    sin_h = jnp.stack([-sin, sin], axis=2).reshape(n_tokens, HEAD_DIM)
    reps = LANES // HEAD_DIM
    return jnp.tile(cos_h, (1, reps)), jnp.tile(sin_h, (1, reps))


def _lru_dense_weights(lru_wa, lru_wi, lru_ba, lru_bi):
    per = LRU_BLOCKS // 2
    hw = LRU_HALF
    eye = jnp.eye(per, dtype=bf16)

    def dense(w, c):
        wh = w[:, :, c * per:(c + 1) * per].astype(bf16)
        d = wh[:, :, :, :, None, :] * eye[None, None, :, None, :, None]
        return d.reshape(DEPTH, 2, hw, hw)

    halves, biases = [], []
    for c in range(2):
        sl = slice(c * hw, (c + 1) * hw)
        da, di = dense(lru_wa, c), dense(lru_wi, c)
        halves.append(jnp.concatenate([da[:, 0], di[:, 0], da[:, 1], di[:, 1]], axis=-1))
        biases.append(jnp.concatenate(
            [lru_ba[:, 0, sl], lru_bi[:, 0, sl], lru_ba[:, 1, sl], lru_bi[:, 1, sl]], axis=-1))
    wd = jnp.stack(halves, axis=1)
    bias = jnp.stack(biases, axis=1)[:, :, None, :]
    return wd, bias


def kernel(x_prompt, x_sample, c, cache_k, cache_v, state_lru, c_ctx, w_ada, b_ada, norm_w, final_norm_w,
           ffn1_wg, ffn1_wu, ffn1_wd, ffn2_wg, ffn2_wu, ffn2_wd, w_in, w_branch_gate, b_branch_gate,
           attn_sink, w_attn_out, conv_w, conv_b, lru_wa, lru_ba, lru_wi, lru_bi, lru_lambda,
           w_lru_out, w_fourier_out, w_o):
    batch, seq, _ = x_prompt.shape
    dec_batch, dec_seq, _ = x_sample.shape
    past = cache_k.shape[2]
    assert 1 + dec_batch <= COND_ROWS

    cond = jnp.concatenate([c_ctx[None, :], c, jnp.zeros((COND_ROWS - 1 - dec_batch, D_MODEL), f32)], axis=0)
    mod = _ada_call(cond, w_ada, b_ada)
    mod5 = mod.reshape(DEPTH, COND_ROWS, N_SUB * 3, D_MODEL).transpose(0, 2, 1, 3)[:, :, :, None, :]
    norm_w4 = norm_w[:, :, None, :]
    final_w = final_norm_w[None, :]

    w_in_bf = w_in.astype(bf16)
    wbg_bf = w_branch_gate.astype(bf16)
    bbg = b_branch_gate[:, None, :]
    wao_bf = w_attn_out.astype(bf16)
    wlo_bf = w_lru_out.astype(bf16)
    wfo_bf = w_fourier_out.astype(bf16)
    wo_bf = w_o.astype(bf16)
    lru_wd, lru_bias = _lru_dense_weights(lru_wa, lru_wi, lru_ba, lru_bi)
    conv_b3 = conv_b[:, None, :]
    rope_tabs = _rope_lane_tables(dec_seq)
    cc, sc_ = _dft_tables(FOURIER_GW)
    csc = jnp.asarray(np.concatenate([cc, sc_], axis=1), f32).astype(bf16)
    dft = {}
    for n in (seq, dec_seq):
        cl, sl = _dft_tables(n)
        rows = n // 2 + FOURIER_PAD
        dft[n] = (jnp.asarray(cl[:rows], f32).astype(bf16), jnp.asarray(sl[:rows], f32).astype(bf16))
    rev = jnp.asarray(np.eye(FOURIER_REV)[::-1], f32).astype(bf16)
    cache_k4 = cache_k.reshape(dec_batch, DEPTH, past, KV_WIDTH)
    cache_v4 = cache_v.reshape(dec_batch, DEPTH, past, KV_WIDTH)
    h0_ctx = jnp.zeros((batch, 2, LRU_WIDTH), f32)

    tm_p, tm_s = 2048, 2048
    row_p = lambda i: 0
    row_s = lambda i: 1 + (i * tm_s) // dec_seq
    tmi = 1024
    row_pi = lambda i: 0
    row_si = lambda i: 1 + (i * tmi) // dec_seq

    xp = x_prompt.reshape(batch * seq, D_MODEL)
    xs = x_sample.reshape(dec_batch * dec_seq, D_MODEL)
    ks, vs, ss = [], [], []
    for l in range(DEPTH):
        last = l == DEPTH - 1
        sink = attn_sink[l]
        xp = _ffn_call(xp, mod5, norm_w4, ffn1_wg, ffn1_wu, ffn1_wd, None,
                       layer=l, sub=0, row_fn=row_p, tm=tm_p, name="ffn1_ctx")
        xs = _ffn_call(xs, mod5, norm_w4, ffn1_wg, ffn1_wu, ffn1_wd, None,
                       layer=l, sub=0, row_fn=row_s, tm=tm_s, name="ffn1_lat")
        q, k, v, xr, yr, xf = _inproj_call(xp, mod5, norm_w4, w_in_bf, None,
                                           layer=l, row_fn=row_pi, tm=tmi, seq=seq, name="inproj_ctx")
        ks.append(k.reshape(batch, seq, KV_HEADS, HEAD_DIM))
        vs.append(v.reshape(batch, seq, KV_HEADS, HEAD_DIM))
        attn = _ctx_attn_call(sink, q, k, v, batch=batch, seq=seq)
        rec, st = _lru_call(xr, yr, conv_w, conv_b3, lru_wd, lru_bias, lru_lambda, h0_ctx,
                            layer=l, batch=batch, seq=seq)
        ss.append(st)
        four = _fourier_call(xf, csc, *dft[seq], rev, batch=batch, seq=seq)
        xp = _merge_call(xp, mod5, norm_w4, attn, rec, four, wbg_bf, bbg, wao_bf, wlo_bf, wfo_bf, wo_bf,
                         layer=l, row_fn=row_pi, tm=tmi, name="merge_ctx")
        q, k, v, xr, yr, xf = _inproj_call(xs, mod5, norm_w4, w_in_bf, rope_tabs,
                                           layer=l, row_fn=row_si, tm=tmi, seq=dec_seq, name="inproj_lat")
        attn = _lat_attn_call(sink, q, k, v, cache_k4, cache_v4, layer=l, batch=dec_batch, seq=dec_seq)
        rec, _ = _lru_call(xr, yr, conv_w, conv_b3, lru_wd, lru_bias, lru_lambda, state_lru[:, l],
                           layer=l, batch=dec_batch, seq=dec_seq)
        four = _fourier_call(xf, csc, *dft[dec_seq], rev, batch=dec_batch, seq=dec_seq)
        xs = _merge_call(xs, mod5, norm_w4, attn, rec, four, wbg_bf, bbg, wao_bf, wlo_bf, wfo_bf, wo_bf,
                         layer=l, row_fn=row_si, tm=tmi, name="merge_lat")
        xp = _ffn_call(xp, mod5, norm_w4, ffn2_wg, ffn2_wu, ffn2_wd, final_w if last else None,
                       layer=l, sub=2, row_fn=row_p, tm=tm_p, name="ffn2_ctx")
        xs = _ffn_call(xs, mod5, norm_w4, ffn2_wg, ffn2_wu, ffn2_wd, final_w if last else None,
                       layer=l, sub=2, row_fn=row_s, tm=tm_s, name="ffn2_lat")

    y_prompt = xp.reshape(batch, seq, D_MODEL)
    y_sample = xs.reshape(dec_batch, dec_seq, D_MODEL)
    return (y_prompt, y_sample, jnp.stack(ks, axis=1), jnp.stack(vs, axis=1), jnp.stack(ss, axis=1))
```

```python
import functools
import math

import numpy as np
import jax
import jax.numpy as jnp
from jax import lax
from jax.experimental import pallas as pl
from jax.experimental.pallas import tpu as pltpu

f32 = jnp.float32
bf16 = jnp.bfloat16

D_MODEL = 1024
DEPTH = 2
GRID_W = 64
N_HEADS = 8
KV_HEADS = 2
HEAD_DIM = 64
Q_GROUP = N_HEADS // KV_HEADS
ATTN_WIDTH = N_HEADS * HEAD_DIM
KV_WIDTH = KV_HEADS * HEAD_DIM
WINDOW = 128
BLOCK = 128
AXIS_ROT = HEAD_DIM // 2
ROPE_BASE = 10000.0
LRU_WIDTH = 512
LRU_BLOCKS = 8
LRU_BW = LRU_WIDTH // LRU_BLOCKS
LRU_C = 8.0
CONV_W = 4
CONV_LEFT = 2
FOURIER_WIDTH = 512
FOURIER_GROUPS = 4
FOURIER_GW = FOURIER_WIDTH // FOURIER_GROUPS
D_FF = 2816
N_BRANCH = 3
N_SUB = 3
EPS = 1e-6
NEG = -1e30
IN_WIDTH = ATTN_WIDTH + 2 * KV_WIDTH + 2 * LRU_WIDTH + FOURIER_WIDTH

V7X_VMEM_BYTES = 64 * 1024 * 1024
SUBLANES = 8
LANES = 128
MXU_N = 256
assert 2 * KV_WIDTH == MXU_N
COND_ROWS = 8
LRU_HALF = LRU_WIDTH // 2
LRU_SEGMENTS = SUBLANES
PAD_ROWS = SUBLANES

_MIB = 1024 * 1024


def _cparams(sem, vmem_mib):
    del vmem_mib
    return pltpu.CompilerParams(dimension_semantics=sem, vmem_limit_bytes=V7X_VMEM_BYTES)


def _dot(a, b):
    return jnp.dot(a, b, preferred_element_type=f32)


def _sigmoid(x):
    return jax.nn.sigmoid(x)


def _modnorm(x, nw, shift, scale):
    y = x * lax.rsqrt(jnp.mean(x * x, axis=-1, keepdims=True) + EPS)
    return y * (nw * (1.0 + scale)) + shift


def _ada_kernel(cond_ref, w_ref, b_ref, o_ref):
    c = cond_ref[...]
    s = (c * _sigmoid(c)).astype(bf16)
    o_ref[...] = _dot(s, w_ref[...].astype(bf16)) + b_ref[...]


def _ada_call(cond, w_ada, b_ada):
    n_out = w_ada.shape[-1]
    tn = 1024
    return pl.pallas_call(
        _ada_kernel,
        out_shape=jax.ShapeDtypeStruct((DEPTH, COND_ROWS, n_out), f32),
        grid=(DEPTH, n_out // tn),
        in_specs=[
            pl.BlockSpec((COND_ROWS, D_MODEL), lambda l, j: (0, 0)),
            pl.BlockSpec((None, D_MODEL, tn), lambda l, j: (l, 0, j)),
            pl.BlockSpec((None, 1, tn), lambda l, j: (l, 0, j)),
        ],
        out_specs=pl.BlockSpec((None, COND_ROWS, tn), lambda l, j: (l, 0, j)),
        compiler_params=_cparams(("parallel", "parallel"), 24),
        name="adaln",
    )(cond, w_ada, b_ada.reshape(DEPTH, 1, n_out))


def _mod_spec(layer, slot, row_fn):
    return pl.BlockSpec((None, None, None, 1, D_MODEL),
                        lambda i, *_: (layer, slot, row_fn(i), 0, 0))


def _vec_spec(layer, sub):
    return pl.BlockSpec((None, None, 1, D_MODEL), lambda i, *_: (layer, sub, 0, 0))


def _ffn_kernel(*refs, n_ff, final):
    if final:
        (x_ref, sh_ref, sc_ref, gt_ref, nw_ref, wg_ref, wu_ref, wd_ref, fw_ref,
         o_ref, h_ref) = refs
    else:
        (x_ref, sh_ref, sc_ref, gt_ref, nw_ref, wg_ref, wu_ref, wd_ref,
         o_ref, h_ref) = refs
    j = pl.program_id(1)

    def down(h):
        g = _dot(h, wg_ref[...].astype(bf16))
        u = _dot(h, wu_ref[...].astype(bf16))
        a = ((g * _sigmoid(g)) * u).astype(bf16)
        return _dot(a, wd_ref[...].astype(bf16))

    @pl.when(j == 0)
    def _():
        h = _modnorm(x_ref[...], nw_ref[...], sh_ref[...], sc_ref[...]).astype(bf16)
        h_ref[...] = h
        o_ref[...] = down(h)

    @pl.when((j > 0) & (j < n_ff - 1))
    def _():
        o_ref[...] += down(h_ref[...])

    @pl.when(j == n_ff - 1)
    def _():
        y = x_ref[...] + (0.5 * gt_ref[...]) * (o_ref[...] + down(h_ref[...]))
        if final:
            y = y * lax.rsqrt(jnp.mean(y * y, axis=-1, keepdims=True) + EPS) * fw_ref[...]
        o_ref[...] = y


def _ffn_call(x, mod5, norm_w4, wg, wu, wd, final_w, *, layer, sub, row_fn, tm, name):
    n_tok = x.shape[0]
    tf = 256
    n_ff = D_FF // tf
    final = final_w is not None
    in_specs = [
        pl.BlockSpec((tm, D_MODEL), lambda i, j: (i, 0)),
        _mod_spec(layer, sub * 3 + 0, row_fn),
        _mod_spec(layer, sub * 3 + 1, row_fn),
        _mod_spec(layer, sub * 3 + 2, row_fn),
        _vec_spec(layer, sub),
        pl.BlockSpec((None, D_MODEL, tf), lambda i, j: (layer, 0, j)),
        pl.BlockSpec((None, D_MODEL, tf), lambda i, j: (layer, 0, j)),
        pl.BlockSpec((None, tf, D_MODEL), lambda i, j: (layer, j, 0)),
    ]
    args = [x, mod5, mod5, mod5, norm_w4, wg, wu, wd]
    if final:
        in_specs.append(pl.BlockSpec((1, D_MODEL), lambda i, j: (0, 0)))
        args.append(final_w)
    return pl.pallas_call(
        functools.partial(_ffn_kernel, n_ff=n_ff, final=final),
        out_shape=jax.ShapeDtypeStruct((n_tok, D_MODEL), f32),
        grid=(n_tok // tm, n_ff),
        in_specs=in_specs,
        out_specs=pl.BlockSpec((tm, D_MODEL), lambda i, j: (i, 0)),
        scratch_shapes=[pltpu.VMEM((tm, D_MODEL), bf16)],
        compiler_params=_cparams(("parallel", "arbitrary"), 52),
        name=name,
    )(*args)


def _swap16(x):
    n = x.shape[-1]
    lane = lax.broadcasted_iota(jnp.int32, x.shape, x.ndim - 1)
    first = (lane % AXIS_ROT) < (AXIS_ROT // 2)
    return jnp.where(first, pltpu.roll(x, n - AXIS_ROT // 2, x.ndim - 1),
                     pltpu.roll(x, AXIS_ROT // 2, x.ndim - 1))


def _inproj_kernel(*refs, rope):
    if rope:
        (x_ref, sh_ref, sc_ref, nw_ref, w_ref, cos_ref, sin_ref,
         q_ref, k_ref, v_ref, xr_ref, yr_ref, xf_ref) = refs
    else:
        (x_ref, sh_ref, sc_ref, nw_ref, w_ref,
         q_ref, k_ref, v_ref, xr_ref, yr_ref, xf_ref) = refs
    h = _modnorm(x_ref[...], nw_ref[...], sh_ref[...], sc_ref[...]).astype(bf16)
    if rope:
        cos = cos_ref[...]
        sin = sin_ref[...]

    def rot(t):
        return t * cos + _swap16(t) * sin if rope else t

    o = ATTN_WIDTH
    kv = _dot(h, w_ref[:, o:o + 2 * KV_WIDTH])
    k_ref[...] = rot(kv[:, :KV_WIDTH])
    v_ref[...] = kv[:, KV_WIDTH:]
    for c in range(ATTN_WIDTH // MXU_N):
        q = _dot(h, w_ref[:, c * MXU_N:(c + 1) * MXU_N])
        for t in range(MXU_N // LANES):
            c0 = c * MXU_N + t * LANES
            q_ref[:, c0:c0 + LANES] = rot(q[:, t * LANES:(t + 1) * LANES]).astype(bf16)
    o += 2 * KV_WIDTH
    xr_ref[...] = _dot(h, w_ref[:, o:o + LRU_WIDTH])
    o += LRU_WIDTH
    yr_ref[...] = _dot(h, w_ref[:, o:o + LRU_WIDTH])
    o += LRU_WIDTH
    xf_ref[...] = _dot(h, w_ref[:, o:o + FOURIER_WIDTH]).astype(bf16)


def _inproj_call(x, mod5, norm_w4, w_in_bf, rope_tabs, *, layer, row_fn, tm, seq, name):
    n_tok = x.shape[0]
    rope = rope_tabs is not None
    in_specs = [
        pl.BlockSpec((tm, D_MODEL), lambda i: (i, 0)),
        _mod_spec(layer, 3, row_fn),
        _mod_spec(layer, 4, row_fn),
        _vec_spec(layer, 1),
        pl.BlockSpec((None, D_MODEL, IN_WIDTH), lambda i: (layer, 0, 0)),
    ]
    args = [x, mod5, mod5, norm_w4, w_in_bf]
    if rope:
        per_seq = seq // tm
        in_specs += [pl.BlockSpec((tm, LANES), lambda i: (i % per_seq, 0))] * 2
        args += list(rope_tabs)
    widths = (ATTN_WIDTH, KV_WIDTH, KV_WIDTH, LRU_WIDTH, LRU_WIDTH, FOURIER_WIDTH)
    dtypes = (bf16, f32, f32, f32, f32, bf16)
    return pl.pallas_call(
        functools.partial(_inproj_kernel, rope=rope),
        out_shape=[jax.ShapeDtypeStruct((n_tok, w), d) for w, d in zip(widths, dtypes)],
        grid=(n_tok // tm,),
        in_specs=in_specs,
        out_specs=[pl.BlockSpec((tm, w), lambda i: (i, 0)) for w in widths],
        compiler_params=_cparams(("parallel",), 48),
        name=name,
    )(*args)


def _half_lane_variants(x):
    lane = lax.broadcasted_iota(jnp.int32, x.shape, 1)
    low = lane < HEAD_DIM
    sw = pltpu.roll(x, HEAD_DIM, 1)
    zero = jnp.zeros_like(x)
    head0 = (jnp.where(low, x, zero), jnp.where(low, zero, sw))
    head1 = (jnp.where(low, sw, zero), jnp.where(low, zero, x))
    return [tuple(v.astype(bf16) for v in h) for h in (head0, head1)]


def _sink_softmax(scores, sk):
    m = sk
    for s in scores:
        m = jnp.maximum(m, jnp.max(s, axis=-1, keepdims=True))
    denom = jnp.exp(sk - m)
    probs = []
    for s in scores:
        p = jnp.exp(s - m)
        denom = denom + jnp.sum(p, axis=-1, keepdims=True)
        probs.append(p.astype(bf16))
    return probs, denom


def _sink_heads(sink_ref, q_ref, kvar, vvar, bias_fns, o_ref):
    nq = q_ref.shape[0]
    top = lax.broadcasted_iota(jnp.int32, (2 * nq, 1), 0) < nq
    low = lax.broadcasted_iota(jnp.int32, (2 * nq, LANES), 1) < HEAD_DIM
    nt = (((1,), (1,)), ((), ()))
    for g in range(KV_HEADS):
        c0 = g * Q_GROUP * HEAD_DIM
        q2 = jnp.concatenate([q_ref[:, c0:c0 + LANES], q_ref[:, c0 + LANES:c0 + 2 * LANES]], axis=0)
        q2 = q2 * jnp.asarray(HEAD_DIM ** -0.5, bf16)
        parts = []
        for par in range(2):
            scores = []
            for piece, bias_fn in zip(kvar[g][par], bias_fns):
                s = lax.dot_general(q2, piece, nt, preferred_element_type=f32)
                scores.append(s if bias_fn is None else bias_fn(s))
            n_top = g * Q_GROUP + par
            sk = jnp.where(top, sink_ref[n_top], sink_ref[n_top + 2])
            parts.append(_sink_softmax(scores, sk))
        (p_e, d_e), (p_o, d_o) = parts
        o2 = None
        for probs, vals in ((p_e, vvar[g][0]), (p_o, vvar[g][1])):
            for p, v in zip(probs, vals):
                o2 = _dot(p, v) if o2 is None else o2 + _dot(p, v)
        o2 = o2 * jnp.where(low, 1.0 / d_e, 1.0 / d_o)
        o_ref[:, c0:c0 + LANES] = o2[:nq].astype(o_ref.dtype)
        o_ref[:, c0 + LANES:c0 + 2 * LANES] = o2[nq:].astype(o_ref.dtype)


CTX_ATTN_PER_STEP = 2


def _ctx_attn_kernel(sink_ref, q_ref, k_ref, v_ref, o_ref, *, seq):
    for s in range(CTX_ATTN_PER_STEP):
        rows = slice(s * seq, (s + 1) * seq)
        kvar = [[[v] for v in head] for head in _half_lane_variants(k_ref[rows, :])]
        vvar = [[[v] for v in head] for head in _half_lane_variants(v_ref[rows, :])]
        _sink_heads(sink_ref, q_ref.at[rows, :], kvar, vvar, [None], o_ref.at[rows, :])


def _ctx_attn_call(sink, q, k, v, *, batch, seq):
    per = CTX_ATTN_PER_STEP
    assert batch % per == 0
    return pl.pallas_call(
        functools.partial(_ctx_attn_kernel, seq=seq),
        out_shape=jax.ShapeDtypeStruct((batch * seq, ATTN_WIDTH), bf16),
        grid=(batch // per,),
        in_specs=[
            pl.BlockSpec(memory_space=pltpu.SMEM),
            pl.BlockSpec((per * seq, ATTN_WIDTH), lambda b: (b, 0)),
            pl.BlockSpec((per * seq, KV_WIDTH), lambda b: (b, 0)),
            pl.BlockSpec((per * seq, KV_WIDTH), lambda b: (b, 0)),
        ],
        out_specs=pl.BlockSpec((per * seq, ATTN_WIDTH), lambda b: (b, 0)),
        compiler_params=_cparams(("parallel",), 32),
        name="ctx_attn",
    )(sink, q, k, v)


ATTN_PREP_ROWS = 256


def _lat_attn_kernel(sink_ref, q_ref, k_ref, v_ref, kx_ref, vx_ref, o_ref, kvar_ref, vvar_ref,
                     *, n_blocks, past):
    n = pl.program_id(1)
    seq = n_blocks * BLOCK
    ctx0 = seq + 2 * BLOCK

    @pl.when(n == 0)
    def _():
        zeros = jnp.zeros((BLOCK, KV_WIDTH), bf16)
        for src, ctx, dst in ((k_ref, kx_ref, kvar_ref), (v_ref, vx_ref, vvar_ref)):
            for i in range(2 * KV_HEADS):
                dst[i, 0:BLOCK, :] = zeros
                dst[i, BLOCK + seq:ctx0, :] = zeros
            for r0 in range(0, seq + past, ATTN_PREP_ROWS):
                rows = min(ATTN_PREP_ROWS, seq + past - r0)
                x = src[r0:r0 + rows, :] if r0 < seq else ctx[r0 - seq:r0 - seq + rows, :]
                d0 = BLOCK + r0 if r0 < seq else ctx0 + r0 - seq
                variants = _half_lane_variants(x)
                for g in range(KV_HEADS):
                    for par in range(2):
                        dst[2 * g + par, d0:d0 + rows, :] = variants[g][par]

    assert WINDOW == BLOCK
    band0 = pl.multiple_of(n * BLOCK, BLOCK)

    def pieces(ref):
        return [[[ref[2 * g + par, pl.ds(band0, 3 * BLOCK), :], ref[2 * g + par, ctx0:ctx0 + past, :]]
                 for par in range(2)] for g in range(KV_HEADS)]

    a_idx = lax.broadcasted_iota(jnp.int32, (2 * BLOCK, BLOCK), 0) % BLOCK
    c_idx = lax.broadcasted_iota(jnp.int32, (2 * BLOCK, BLOCK), 1)
    bias_p = jnp.where((c_idx >= a_idx) & (n > 0), 0.0, NEG).astype(f32)
    bias_n = jnp.where((c_idx <= a_idx) & (n < n_blocks - 1), 0.0, NEG).astype(f32)

    def band_bias(s):
        return jnp.concatenate([s[:, :BLOCK] + bias_p, s[:, BLOCK:2 * BLOCK],
                                s[:, 2 * BLOCK:] + bias_n], axis=1)

    _sink_heads(sink_ref, q_ref, pieces(kvar_ref), pieces(vvar_ref), [band_bias, None], o_ref)


def _lat_attn_call(sink, q, k, v, cache_k4, cache_v4, *, layer, batch, seq):
    nb = seq // BLOCK
    past = cache_k4.shape[2]
    assert seq % ATTN_PREP_ROWS == 0
    whole = pl.BlockSpec((seq, KV_WIDTH), lambda b, n: (b, 0))
    ctx = pl.BlockSpec((None, None, past, KV_WIDTH), lambda b, n: (b, layer, 0, 0))
    return pl.pallas_call(
        functools.partial(_lat_attn_kernel, n_blocks=nb, past=past),
        out_shape=jax.ShapeDtypeStruct((batch * seq, ATTN_WIDTH), bf16),
        grid=(batch, nb),
        in_specs=[
            pl.BlockSpec(memory_space=pltpu.SMEM),
            pl.BlockSpec((BLOCK, ATTN_WIDTH), lambda b, n: (b * nb + n, 0)),
            whole, whole, ctx, ctx,
        ],
        out_specs=pl.BlockSpec((BLOCK, ATTN_WIDTH), lambda b, n: (b * nb + n, 0)),
        scratch_shapes=[pltpu.VMEM((2 * KV_HEADS, seq + 2 * BLOCK + past, KV_WIDTH), bf16)] * 2,
        compiler_params=_cparams(("parallel", "arbitrary"), 32),
        name="lat_attn",
    )(sink, q, k, v, cache_k4, cache_v4)


def _softplus(x):
    return jnp.maximum(x, 0.0) + jnp.log1p(jnp.exp(-jnp.abs(x)))


def _gelu_tanh(x):
    return 0.5 * x * (1.0 + jnp.tanh(math.sqrt(2.0 / math.pi) * (x + 0.044715 * (x * x * x))))


def _seg_pitch(seg):
    assert seg % SUBLANES == 0
    tiles = seg // SUBLANES
    return seg if tiles % 2 else seg + SUBLANES


def _seg_row(t, seg, pitch):
    return (t // seg) * pitch + t % seg


def _lru_kernel(xr_ref, yr_ref, cw_ref, cb_ref, wd_ref, bias_ref, lam_ref, h0_ref,
                rec_ref, st_ref, xs_ref, af_ref, bf_ref, ab_ref, bb_ref, *, seq):
    seg = seq // LRU_SEGMENTS
    chunk = min(seq, 256)
    hw = LRU_HALF
    n_lt = hw // LANES
    pitch = _seg_pitch(seg)
    piece = min(seg, chunk)

    zeros_pad = jnp.zeros((PAD_ROWS, hw), f32)
    xs_ref[0:PAD_ROWS, :] = zeros_pad
    xs_ref[PAD_ROWS + seq:2 * PAD_ROWS + seq, :] = zeros_pad
    for c in range(seq // chunk):
        xs_ref[PAD_ROWS + c * chunk:PAD_ROWS + (c + 1) * chunk, :] = xr_ref[c * chunk:(c + 1) * chunk, :]

    neg_c_sp = [-LRU_C * _softplus(-lam_ref[d:d + 1, :]) for d in range(2)]
    a_refs = (af_ref, ab_ref)
    b_refs = (bf_ref, bb_ref)

    for c in range(seq // chunk):
        r0 = c * chunk
        xc = cb_ref[...]
        for j in range(CONV_W):
            s0 = PAD_ROWS + r0 + j - CONV_LEFT
            xc = xc + xs_ref[s0:s0 + chunk, :] * cw_ref[j:j + 1, :]
        y = _dot(xc.astype(bf16), wd_ref[...]) + bias_ref[...]
        for d in range(2):
            r = _sigmoid(y[:, (2 * d) * hw:(2 * d + 1) * hw])
            gi = _sigmoid(y[:, (2 * d + 1) * hw:(2 * d + 2) * hw])
            log_a = neg_c_sp[d] * r
            a = jnp.exp(log_a)
            z = jnp.tanh(log_a) * (-1.0 - a * a)
            b = jnp.where(z > 0.0, z * lax.rsqrt(z), 0.0) * (gi * xc)
            for p0 in range(0, chunk, piece):
                dst = _seg_row(r0 + p0, seg, pitch)
                for t in range(n_lt):
                    lanes = slice(t * LANES, (t + 1) * LANES)
                    a_refs[d][t, dst:dst + piece, :] = a[p0:p0 + piece, lanes]
                    b_refs[d][t, dst:dst + piece, :] = b[p0:p0 + piece, lanes]

    def step(j, carry):
        jf = pl.ds(j, LRU_SEGMENTS, stride=pitch)
        jb = pl.ds(seg - 1 - j, LRU_SEGMENTS, stride=pitch)
        out = []
        for t in range(n_lt):
            hf, pf, hb, pb = carry[4 * t:4 * t + 4]
            a = af_ref[t, jf, :]
            hf = a * hf + bf_ref[t, jf, :]
            pf = a * pf
            bf_ref[t, jf, :] = hf
            af_ref[t, jf, :] = pf
            a = ab_ref[t, jb, :]
            hb = a * hb + bb_ref[t, jb, :]
            pb = a * pb
            bb_ref[t, jb, :] = hb
            ab_ref[t, jb, :] = pb
            out += [hf, pf, hb, pb]
        return tuple(out)

    z = jnp.zeros((LRU_SEGMENTS, LANES), f32)
    o = jnp.ones((LRU_SEGMENTS, LANES), f32)
    fin = lax.fori_loop(0, seg, step, (z, o, z, o) * n_lt, unroll=2)

    sub = min(seg, 256)
    for t in range(n_lt):
        hf, pf, hb, pb = fin[4 * t:4 * t + 4]
        lanes = slice(t * LANES, (t + 1) * LANES)
        cf = h0_ref[0:1, lanes]
        carry_f = []
        for s in range(LRU_SEGMENTS):
            carry_f.append(cf)
            cf = hf[s:s + 1, :] + pf[s:s + 1, :] * cf
        cb = h0_ref[1:2, lanes]
        carry_b = [None] * LRU_SEGMENTS
        for s in reversed(range(LRU_SEGMENTS)):
            carry_b[s] = cb
            cb = hb[s:s + 1, :] + pb[s:s + 1, :] * cb
        st_ref[0:1, lanes] = cf
        st_ref[1:2, lanes] = cb
        for s in range(LRU_SEGMENTS):
            for u in range(seg // sub):
                rows = slice(s * seg + u * sub, s * seg + (u + 1) * sub)
                src = slice(s * pitch + u * sub, s * pitch + (u + 1) * sub)
                h = ((bf_ref[t, src, :] + af_ref[t, src, :] * carry_f[s])
                     + (bb_ref[t, src, :] + ab_ref[t, src, :] * carry_b[s]))
                rec_ref[rows, lanes] = (h * _gelu_tanh(yr_ref[rows, lanes])).astype(rec_ref.dtype)


def _lru_call(xr, yr, conv_w, conv_b, wd, bias, lam, h0, *, layer, batch, seq):
    hw = LRU_HALF
    return pl.pallas_call(
        functools.partial(_lru_kernel, seq=seq),
        out_shape=[jax.ShapeDtypeStruct((batch * seq, LRU_WIDTH), bf16),
                   jax.ShapeDtypeStruct((batch, 2, LRU_WIDTH), f32)],
        grid=(batch, 2),
        in_specs=[
            pl.BlockSpec((seq, hw), lambda b, c: (b, c)),
            pl.BlockSpec((seq, hw), lambda b, c: (b, c)),
            pl.BlockSpec((None, CONV_W, hw), lambda b, c: (layer, 0, c)),
            pl.BlockSpec((None, 1, hw), lambda b, c: (layer, 0, c)),
            pl.BlockSpec((None, None, hw, 4 * hw), lambda b, c: (layer, c, 0, 0)),
            pl.BlockSpec((None, None, 1, 4 * hw), lambda b, c: (layer, c, 0, 0)),
            pl.BlockSpec((None, 2, hw), lambda b, c: (layer, 0, c)),
            pl.BlockSpec((None, 2, hw), lambda b, c: (b, 0, c)),
        ],
        out_specs=[pl.BlockSpec((seq, hw), lambda b, c: (b, c)),
                   pl.BlockSpec((None, 2, hw), lambda b, c: (b, 0, c))],
        scratch_shapes=([pltpu.VMEM((seq + 2 * PAD_ROWS, hw), f32)]
                        + [pltpu.VMEM((hw // LANES, LRU_SEGMENTS * _seg_pitch(seq // LRU_SEGMENTS), LANES),
                                      f32)] * 4),
        compiler_params=_cparams(("parallel", "parallel"), 40),
        name=f"lru_{seq}",
    )(xr, yr, conv_w, conv_b, wd, bias, lam, h0)


FOURIER_REV = 128
FOURIER_PAD = 16
FOURIER_STEP_ROWS = 2048


def _fourier_kernel(x_ref, csc_ref, ch_ref, sh_ref, rev_ref, o_ref, u_ref, *, seq, per_step, scale):
    half = seq // 2
    chunk = min(seq, 512)
    w = FOURIER_WIDTH
    for s in range(per_step):
        for c in range(seq // chunk):
            rows = slice(s * seq + c * chunk, s * seq + (c + 1) * chunk)
            for g in range(FOURIER_GROUPS):
                cols = slice(g * FOURIER_GW, (g + 1) * FOURIER_GW)
                dst = slice(s * w + g * FOURIER_GW, s * w + (g + 1) * FOURIER_GW)
                u = _dot(x_ref[rows, cols], csc_ref[...])
                u_ref[c * chunk:(c + 1) * chunk, dst] = u[:, :FOURIER_GW].astype(bf16)
                u_ref[seq + c * chunk:seq + (c + 1) * chunk, dst] = u[:, FOURIER_GW:].astype(bf16)
    a = _dot(ch_ref[...], u_ref[0:seq, :])
    b = _dot(sh_ref[...], u_ref[seq:2 * seq, :])
    top = ((a[0:half] - b[0:half]) * scale).astype(o_ref.dtype)
    z = ((a[1:half + 1] + b[1:half + 1]) * scale).astype(bf16)
    nb = half // FOURIER_REV
    for s in range(per_step):
        cols = slice(s * w, (s + 1) * w)
        o_ref[s * seq:s * seq + half, :] = top[:, cols]
        for k in range(nb):
            blk = z[(nb - 1 - k) * FOURIER_REV:(nb - k) * FOURIER_REV, cols]
            r0 = s * seq + half + k * FOURIER_REV
            o_ref[r0:r0 + FOURIER_REV, :] = _dot(rev_ref[...], blk).astype(o_ref.dtype)


def _dft_tables(n):
    k = np.arange(n, dtype=np.int64)
    ang = (2.0 * np.pi / n) * ((k[:, None] * k[None, :]) % n).astype(np.float64)
    return np.cos(ang), np.sin(ang)


def _fourier_call(xf, csc, ch, sh, rev, *, batch, seq):
    scale = 1.0 / math.sqrt(seq * FOURIER_GW)
    rows = seq // 2 + FOURIER_PAD
    per_step = max(1, min(batch, FOURIER_STEP_ROWS // seq))
    assert batch % per_step == 0

    def const(shape):
        return pl.BlockSpec(shape, lambda b: (0, 0), pipeline_mode=pl.Buffered(1))

    return pl.pallas_call(
        functools.partial(_fourier_kernel, seq=seq, per_step=per_step, scale=scale),
        out_shape=jax.ShapeDtypeStruct((batch * seq, FOURIER_WIDTH), bf16),
        grid=(batch // per_step,),
        in_specs=[
            pl.BlockSpec((per_step * seq, FOURIER_WIDTH), lambda b: (b, 0)),
            const((FOURIER_GW, 2 * FOURIER_GW)),
            const((rows, seq)),
            const((rows, seq)),
            const((FOURIER_REV, FOURIER_REV)),
        ],
        out_specs=pl.BlockSpec((per_step * seq, FOURIER_WIDTH), lambda b: (b, 0)),
        scratch_shapes=[pltpu.VMEM((2 * seq, per_step * FOURIER_WIDTH), bf16)],
        compiler_params=_cparams(("parallel",), 40),
        name=f"fourier_{seq}",
    )(xf, csc, ch, sh, rev)


def _merge_kernel(x_ref, sh_ref, sc_ref, gt_ref, nw_ref, at_ref, rc_ref, fr_ref,
                  wbg_ref, bbg_ref, wao_ref, wlo_ref, wfo_ref, wo_ref, o_ref):
    x = x_ref[...]
    h = _modnorm(x, nw_ref[...], sh_ref[...], sc_ref[...]).astype(bf16)
    merged = None
    for idx, (br_ref, w_ref) in enumerate(((at_ref, wao_ref), (rc_ref, wlo_ref), (fr_ref, wfo_ref))):
        cols = slice(idx * D_MODEL, (idx + 1) * D_MODEL)
        g = _sigmoid(_dot(h, wbg_ref[:, cols]) + bbg_ref[:, cols])
        y = g * _dot(br_ref[...], w_ref[...])
        merged = y if merged is None else merged + y
    out = _dot(merged.astype(bf16), wo_ref[...])
    o_ref[...] = x + gt_ref[...] * out


def _merge_call(x, mod5, norm_w4, attn, rec, four, wbg, bbg, wao, wlo, wfo, wo, *, layer, row_fn, tm, name):
    n_tok = x.shape[0]

    def const(shape):
        nd = len(shape)
        return pl.BlockSpec((None,) + shape, lambda i: (layer,) + (0,) * nd,
                            pipeline_mode=pl.Buffered(1))

    def branch():
        return pl.BlockSpec((tm, ATTN_WIDTH), lambda i: (i, 0))

    return pl.pallas_call(
        _merge_kernel,
        out_shape=jax.ShapeDtypeStruct((n_tok, D_MODEL), f32),
        grid=(n_tok // tm,),
        in_specs=[
            pl.BlockSpec((tm, D_MODEL), lambda i: (i, 0)),
            _mod_spec(layer, 3, row_fn),
            _mod_spec(layer, 4, row_fn),
            _mod_spec(layer, 5, row_fn),
            _vec_spec(layer, 1),
            branch(), branch(), branch(),
            const((D_MODEL, N_BRANCH * D_MODEL)),
            const((1, N_BRANCH * D_MODEL)),
            const((ATTN_WIDTH, D_MODEL)),
            const((LRU_WIDTH, D_MODEL)),
            const((FOURIER_WIDTH, D_MODEL)),
            const((D_MODEL, D_MODEL)),
        ],
        out_specs=pl.BlockSpec((tm, D_MODEL), lambda i: (i, 0)),
        compiler_params=_cparams(("parallel",), 48),
        name=name,
    )(x, mod5, mod5, mod5, norm_w4, attn, rec, four, wbg, bbg, wao, wlo, wfo, wo)


def _rope_lane_tables(n_tokens):
    rows = n_tokens // GRID_W
    row = jnp.repeat(jnp.arange(rows), GRID_W).astype(f32)
    col = jnp.tile(jnp.arange(GRID_W), rows).astype(f32)
    inv = ROPE_BASE ** (-jnp.arange(0, AXIS_ROT, 2, dtype=f32) / AXIS_ROT)
    ang = jnp.stack([row[:, None] * inv, col[:, None] * inv], axis=1)
    cos, sin = jnp.cos(ang), jnp.sin(ang)
    cos_h = jnp.stack([cos, cos], axis=2).reshape(n_tokens, HEAD_DIM)
    sin_h = jnp.stack([-sin, sin], axis=2).reshape(n_tokens, HEAD_DIM)
    reps = LANES // HEAD_DIM
    return jnp.tile(cos_h, (1, reps)), jnp.tile(sin_h, (1, reps))


def _lru_dense_weights(lru_wa, lru_wi, lru_ba, lru_bi):
    per = LRU_BLOCKS // 2
    hw = LRU_HALF
    eye = jnp.eye(per, dtype=bf16)

    def dense(w, c):
        wh = w[:, :, c * per:(c + 1) * per].astype(bf16)
        d = wh[:, :, :, :, None, :] * eye[None, None, :, None, :, None]
        return d.reshape(DEPTH, 2, hw, hw)

    halves, biases = [], []
    for c in range(2):
        sl = slice(c * hw, (c + 1) * hw)
        da, di = dense(lru_wa, c), dense(lru_wi, c)
        halves.append(jnp.concatenate([da[:, 0], di[:, 0], da[:, 1], di[:, 1]], axis=-1))
        biases.append(jnp.concatenate(
            [lru_ba[:, 0, sl], lru_bi[:, 0, sl], lru_ba[:, 1, sl], lru_bi[:, 1, sl]], axis=-1))
    wd = jnp.stack(halves, axis=1)
    bias = jnp.stack(biases, axis=1)[:, :, None, :]
    return wd, bias


def kernel(x_prompt, x_sample, c, cache_k, cache_v, state_lru, c_ctx, w_ada, b_ada, norm_w, final_norm_w,
           ffn1_wg, ffn1_wu, ffn1_wd, ffn2_wg, ffn2_wu, ffn2_wd, w_in, w_branch_gate, b_branch_gate,
           attn_sink, w_attn_out, conv_w, conv_b, lru_wa, lru_ba, lru_wi, lru_bi, lru_lambda,
           w_lru_out, w_fourier_out, w_o):
    batch, seq, _ = x_prompt.shape
    dec_batch, dec_seq, _ = x_sample.shape
    past = cache_k.shape[2]
    assert 1 + dec_batch <= COND_ROWS

    cond = jnp.concatenate([c_ctx[None, :], c, jnp.zeros((COND_ROWS - 1 - dec_batch, D_MODEL), f32)], axis=0)
    mod = _ada_call(cond, w_ada, b_ada)
    mod5 = mod.reshape(DEPTH, COND_ROWS, N_SUB * 3, D_MODEL).transpose(0, 2, 1, 3)[:, :, :, None, :]
    norm_w4 = norm_w[:, :, None, :]
    final_w = final_norm_w[None, :]

    w_in_bf = w_in.astype(bf16)
    wbg_bf = w_branch_gate.astype(bf16)
    bbg = b_branch_gate[:, None, :]
    wao_bf = w_attn_out.astype(bf16)
    wlo_bf = w_lru_out.astype(bf16)
    wfo_bf = w_fourier_out.astype(bf16)
    wo_bf = w_o.astype(bf16)
    lru_wd, lru_bias = _lru_dense_weights(lru_wa, lru_wi, lru_ba, lru_bi)
    conv_b3 = conv_b[:, None, :]
    rope_tabs = _rope_lane_tables(dec_seq)
    cc, sc_ = _dft_tables(FOURIER_GW)
    csc = jnp.asarray(np.concatenate([cc, sc_], axis=1), f32).astype(bf16)
    dft = {}
    for n in (seq, dec_seq):
        cl, sl = _dft_tables(n)
        rows = n // 2 + FOURIER_PAD
        dft[n] = (jnp.asarray(cl[:rows], f32).astype(bf16), jnp.asarray(sl[:rows], f32).astype(bf16))
    rev = jnp.asarray(np.eye(FOURIER_REV)[::-1], f32).astype(bf16)
    cache_k4 = cache_k.reshape(dec_batch, DEPTH, past, KV_WIDTH)
    cache_v4 = cache_v.reshape(dec_batch, DEPTH, past, KV_WIDTH)
    h0_ctx = jnp.zeros((batch, 2, LRU_WIDTH), f32)

    tm_p, tm_s = 2048, 2048
    row_p = lambda i: 0
    row_s = lambda i: 1 + (i * tm_s) // dec_seq
    tmi = 1024
    row_pi = lambda i: 0
    row_si = lambda i: 1 + (i * tmi) // dec_seq

    xp = x_prompt.reshape(batch * seq, D_MODEL)
    xs = x_sample.reshape(dec_batch * dec_seq, D_MODEL)
    ks, vs, ss = [], [], []
    for l in range(DEPTH):
        last = l == DEPTH - 1
        sink = attn_sink[l]
        xp = _ffn_call(xp, mod5, norm_w4, ffn1_wg, ffn1_wu, ffn1_wd, None,
                       layer=l, sub=0, row_fn=row_p, tm=tm_p, name="ffn1_ctx")
        xs = _ffn_call(xs, mod5, norm_w4, ffn1_wg, ffn1_wu, ffn1_wd, None,
                       layer=l, sub=0, row_fn=row_s, tm=tm_s, name="ffn1_lat")
        q, k, v, xr, yr, xf = _inproj_call(xp, mod5, norm_w4, w_in_bf, None,
                                           layer=l, row_fn=row_pi, tm=tmi, seq=seq, name="inproj_ctx")
        ks.append(k.reshape(batch, seq, KV_HEADS, HEAD_DIM))
        vs.append(v.reshape(batch, seq, KV_HEADS, HEAD_DIM))
        attn = _ctx_attn_call(sink, q, k, v, batch=batch, seq=seq)
        rec, st = _lru_call(xr, yr, conv_w, conv_b3, lru_wd, lru_bias, lru_lambda, h0_ctx,
                            layer=l, batch=batch, seq=seq)
        ss.append(st)
        four = _fourier_call(xf, csc, *dft[seq], rev, batch=batch, seq=seq)
        xp = _merge_call(xp, mod5, norm_w4, attn, rec, four, wbg_bf, bbg, wao_bf, wlo_bf, wfo_bf, wo_bf,
                         layer=l, row_fn=row_pi, tm=tmi, name="merge_ctx")
        q, k, v, xr, yr, xf = _inproj_call(xs, mod5, norm_w4, w_in_bf, rope_tabs,
                                           layer=l, row_fn=row_si, tm=tmi, seq=dec_seq, name="inproj_lat")
        attn = _lat_attn_call(sink, q, k, v, cache_k4, cache_v4, layer=l, batch=dec_batch, seq=dec_seq)
        rec, _ = _lru_call(xr, yr, conv_w, conv_b3, lru_wd, lru_bias, lru_lambda, state_lru[:, l],
                           layer=l, batch=dec_batch, seq=dec_seq)
        four = _fourier_call(xf, csc, *dft[dec_seq], rev, batch=dec_batch, seq=dec_seq)
        xs = _merge_call(xs, mod5, norm_w4, attn, rec, four, wbg_bf, bbg, wao_bf, wlo_bf, wfo_bf, wo_bf,
                         layer=l, row_fn=row_si, tm=tmi, name="merge_lat")
        xp = _ffn_call(xp, mod5, norm_w4, ffn2_wg, ffn2_wu, ffn2_wd, final_w if last else None,
                       layer=l, sub=2, row_fn=row_p, tm=tm_p, name="ffn2_ctx")
        xs = _ffn_call(xs, mod5, norm_w4, ffn2_wg, ffn2_wu, ffn2_wd, final_w if last else None,
                       layer=l, sub=2, row_fn=row_s, tm=tm_s, name="ffn2_lat")

    y_prompt = xp.reshape(batch, seq, D_MODEL)
    y_sample = xs.reshape(dec_batch, dec_seq, D_MODEL)
    return (y_prompt, y_sample, jnp.stack(ks, axis=1), jnp.stack(vs, axis=1), jnp.stack(ss, axis=1))
```

```python
import functools
import math

import numpy as np
import jax
import jax.numpy as jnp
from jax import lax
from jax.experimental import pallas as pl
from jax.experimental.pallas import tpu as pltpu

f32 = jnp.float32
bf16 = jnp.bfloat16

D_MODEL = 1024
DEPTH = 2
GRID_W = 64
N_HEADS = 8
KV_HEADS = 2
HEAD_DIM = 64
Q_GROUP = N_HEADS // KV_HEADS
ATTN_WIDTH = N_HEADS * HEAD_DIM
KV_WIDTH = KV_HEADS * HEAD_DIM
WINDOW = 128
BLOCK = 128
AXIS_ROT = HEAD_DIM // 2
ROPE_BASE = 10000.0
LRU_WIDTH = 512
LRU_BLOCKS = 8
LRU_BW = LRU_WIDTH // LRU_BLOCKS
LRU_C = 8.0
CONV_W = 4
CONV_LEFT = 2
FOURIER_WIDTH = 512
FOURIER_GROUPS = 4
FOURIER_GW = FOURIER_WIDTH // FOURIER_GROUPS
D_FF = 2816
N_BRANCH = 3
N_SUB = 3
EPS = 1e-6
NEG = -1e30
IN_WIDTH = ATTN_WIDTH + 2 * KV_WIDTH + 2 * LRU_WIDTH + FOURIER_WIDTH

V7X_VMEM_BYTES = 64 * 1024 * 1024
SUBLANES = 8
LANES = 128
MXU_N = 256
assert 2 * KV_WIDTH == MXU_N
COND_ROWS = 8
LRU_HALF = LRU_WIDTH // 2
LRU_SEGMENTS = SUBLANES
PAD_ROWS = SUBLANES

_MIB = 1024 * 1024


def _cparams(sem, vmem_mib):
    del vmem_mib
    return pltpu.CompilerParams(dimension_semantics=sem, vmem_limit_bytes=V7X_VMEM_BYTES)


def _dot(a, b):
    return jnp.dot(a, b, preferred_element_type=f32)


def _sigmoid(x):
    return jax.nn.sigmoid(x)


def _modnorm(x, nw, shift, scale):
    y = x * lax.rsqrt(jnp.mean(x * x, axis=-1, keepdims=True) + EPS)
    return y * (nw * (1.0 + scale)) + shift


def _ada_kernel(cond_ref, w_ref, b_ref, o_ref):
    c = cond_ref[...]
    s = (c * _sigmoid(c)).astype(bf16)
    o_ref[...] = _dot(s, w_ref[...].astype(bf16)) + b_ref[...]


def _ada_call(cond, w_ada, b_ada):
    n_out = w_ada.shape[-1]
    tn = 1024
    return pl.pallas_call(
        _ada_kernel,
        out_shape=jax.ShapeDtypeStruct((DEPTH, COND_ROWS, n_out), f32),
        grid=(DEPTH, n_out // tn),
        in_specs=[
            pl.BlockSpec((COND_ROWS, D_MODEL), lambda l, j: (0, 0)),
            pl.BlockSpec((None, D_MODEL, tn), lambda l, j: (l, 0, j)),
            pl.BlockSpec((None, 1, tn), lambda l, j: (l, 0, j)),
        ],
        out_specs=pl.BlockSpec((None, COND_ROWS, tn), lambda l, j: (l, 0, j)),
        compiler_params=_cparams(("parallel", "parallel"), 24),
        name="adaln",
    )(cond, w_ada, b_ada.reshape(DEPTH, 1, n_out))


def _mod_spec(layer, slot, row_fn):
    return pl.BlockSpec((None, None, None, 1, D_MODEL),
                        lambda i, *_: (layer, slot, row_fn(i), 0, 0))


def _vec_spec(layer, sub):
    return pl.BlockSpec((None, None, 1, D_MODEL), lambda i, *_: (layer, sub, 0, 0))


def _ffn_kernel(*refs, n_ff, final):
    if final:
        (x_ref, sh_ref, sc_ref, gt_ref, nw_ref, wg_ref, wu_ref, wd_ref, fw_ref,
         o_ref, h_ref) = refs
    else:
        (x_ref, sh_ref, sc_ref, gt_ref, nw_ref, wg_ref, wu_ref, wd_ref,
         o_ref, h_ref) = refs
    j = pl.program_id(1)

    def down(h):
        g = _dot(h, wg_ref[...].astype(bf16))
        u = _dot(h, wu_ref[...].astype(bf16))
        a = ((g * _sigmoid(g)) * u).astype(bf16)
        return _dot(a, wd_ref[...].astype(bf16))

    @pl.when(j == 0)
    def _():
        h = _modnorm(x_ref[...], nw_ref[...], sh_ref[...], sc_ref[...]).astype(bf16)
        h_ref[...] = h
        o_ref[...] = down(h)

    @pl.when((j > 0) & (j < n_ff - 1))
    def _():
        o_ref[...] += down(h_ref[...])

    @pl.when(j == n_ff - 1)
    def _():
        y = x_ref[...] + (0.5 * gt_ref[...]) * (o_ref[...] + down(h_ref[...]))
        if final:
            y = y * lax.rsqrt(jnp.mean(y * y, axis=-1, keepdims=True) + EPS) * fw_ref[...]
        o_ref[...] = y


def _ffn_call(x, mod5, norm_w4, wg, wu, wd, final_w, *, layer, sub, row_fn, tm, name):
    n_tok = x.shape[0]
    tf = 256
    n_ff = D_FF // tf
    final = final_w is not None
    in_specs = [
        pl.BlockSpec((tm, D_MODEL), lambda i, j: (i, 0)),
        _mod_spec(layer, sub * 3 + 0, row_fn),
        _mod_spec(layer, sub * 3 + 1, row_fn),
        _mod_spec(layer, sub * 3 + 2, row_fn),
        _vec_spec(layer, sub),
        pl.BlockSpec((None, D_MODEL, tf), lambda i, j: (layer, 0, j)),
        pl.BlockSpec((None, D_MODEL, tf), lambda i, j: (layer, 0, j)),
        pl.BlockSpec((None, tf, D_MODEL), lambda i, j: (layer, j, 0)),
    ]
    args = [x, mod5, mod5, mod5, norm_w4, wg, wu, wd]
    if final:
        in_specs.append(pl.BlockSpec((1, D_MODEL), lambda i, j: (0, 0)))
        args.append(final_w)
    return pl.pallas_call(
        functools.partial(_ffn_kernel, n_ff=n_ff, final=final),
        out_shape=jax.ShapeDtypeStruct((n_tok, D_MODEL), f32),
        grid=(n_tok // tm, n_ff),
        in_specs=in_specs,
        out_specs=pl.BlockSpec((tm, D_MODEL), lambda i, j: (i, 0)),
        scratch_shapes=[pltpu.VMEM((tm, D_MODEL), bf16)],
        compiler_params=_cparams(("parallel", "arbitrary"), 52),
        name=name,
    )(*args)


def _swap16(x):
    n = x.shape[-1]
    lane = lax.broadcasted_iota(jnp.int32, x.shape, x.ndim - 1)
    first = (lane % AXIS_ROT) < (AXIS_ROT // 2)
    return jnp.where(first, pltpu.roll(x, n - AXIS_ROT // 2, x.ndim - 1),
                     pltpu.roll(x, AXIS_ROT // 2, x.ndim - 1))


def _inproj_kernel(*refs, rope):
    if rope:
        (x_ref, sh_ref, sc_ref, nw_ref, w_ref, cos_ref, sin_ref,
         q_ref, k_ref, v_ref, xr_ref, yr_ref, xf_ref) = refs
    else:
        (x_ref, sh_ref, sc_ref, nw_ref, w_ref,
         q_ref, k_ref, v_ref, xr_ref, yr_ref, xf_ref) = refs
    h = _modnorm(x_ref[...], nw_ref[...], sh_ref[...], sc_ref[...]).astype(bf16)
    if rope:
        cos = cos_ref[...]
        sin = sin_ref[...]

    def rot(t):
        return t * cos + _swap16(t) * sin if rope else t

    o = ATTN_WIDTH
    kv = _dot(h, w_ref[:, o:o + 2 * KV_WIDTH])
    k_ref[...] = rot(kv[:, :KV_WIDTH])
    v_ref[...] = kv[:, KV_WIDTH:]
    for c in range(ATTN_WIDTH // MXU_N):
        q = _dot(h, w_ref[:, c * MXU_N:(c + 1) * MXU_N])
        for t in range(MXU_N // LANES):
            c0 = c * MXU_N + t * LANES
            q_ref[:, c0:c0 + LANES] = rot(q[:, t * LANES:(t + 1) * LANES]).astype(bf16)
    o += 2 * KV_WIDTH
    xr_ref[...] = _dot(h, w_ref[:, o:o + LRU_WIDTH])
    o += LRU_WIDTH
    yr_ref[...] = _dot(h, w_ref[:, o:o + LRU_WIDTH])
    o += LRU_WIDTH
    xf_ref[...] = _dot(h, w_ref[:, o:o + FOURIER_WIDTH]).astype(bf16)


def _inproj_call(x, mod5, norm_w4, w_in_bf, rope_tabs, *, layer, row_fn, tm, seq, name):
    n_tok = x.shape[0]
    rope = rope_tabs is not None
    in_specs = [
        pl.BlockSpec((tm, D_MODEL), lambda i: (i, 0)),
        _mod_spec(layer, 3, row_fn),
        _mod_spec(layer, 4, row_fn),
        _vec_spec(layer, 1),
        pl.BlockSpec((None, D_MODEL, IN_WIDTH), lambda i: (layer, 0, 0)),
    ]
    args = [x, mod5, mod5, norm_w4, w_in_bf]
    if rope:
        per_seq = seq // tm
        in_specs += [pl.BlockSpec((tm, LANES), lambda i: (i % per_seq, 0))] * 2
        args += list(rope_tabs)
    widths = (ATTN_WIDTH, KV_WIDTH, KV_WIDTH, LRU_WIDTH, LRU_WIDTH, FOURIER_WIDTH)
    dtypes = (bf16, f32, f32, f32, f32, bf16)
    return pl.pallas_call(
        functools.partial(_inproj_kernel, rope=rope),
        out_shape=[jax.ShapeDtypeStruct((n_tok, w), d) for w, d in zip(widths, dtypes)],
        grid=(n_tok // tm,),
        in_specs=in_specs,
        out_specs=[pl.BlockSpec((tm, w), lambda i: (i, 0)) for w in widths],
        compiler_params=_cparams(("parallel",), 48),
        name=name,
    )(*args)


def _half_lane_variants(x):
    lane = lax.broadcasted_iota(jnp.int32, x.shape, 1)
    low = lane < HEAD_DIM
    sw = pltpu.roll(x, HEAD_DIM, 1)
    zero = jnp.zeros_like(x)
    head0 = (jnp.where(low, x, zero), jnp.where(low, zero, sw))
    head1 = (jnp.where(low, sw, zero), jnp.where(low, zero, x))
    return [tuple(v.astype(bf16) for v in h) for h in (head0, head1)]


def _sink_softmax(scores, sk):
    m = sk
    for s in scores:
        m = jnp.maximum(m, jnp.max(s, axis=-1, keepdims=True))
    denom = jnp.exp(sk - m)
    probs = []
    for s in scores:
        p = jnp.exp(s - m)
        denom = denom + jnp.sum(p, axis=-1, keepdims=True)
        probs.append(p.astype(bf16))
    return probs, denom


def _sink_heads(sink_ref, q_ref, kvar, vvar, bias_fns, o_ref):
    nq = q_ref.shape[0]
    top = lax.broadcasted_iota(jnp.int32, (2 * nq, 1), 0) < nq
    low = lax.broadcasted_iota(jnp.int32, (2 * nq, LANES), 1) < HEAD_DIM
    nt = (((1,), (1,)), ((), ()))
    for g in range(KV_HEADS):
        c0 = g * Q_GROUP * HEAD_DIM
        q2 = jnp.concatenate([q_ref[:, c0:c0 + LANES], q_ref[:, c0 + LANES:c0 + 2 * LANES]], axis=0)
        q2 = q2 * jnp.asarray(HEAD_DIM ** -0.5, bf16)
        parts = []
        for par in range(2):
            scores = []
            for piece, bias_fn in zip(kvar[g][par], bias_fns):
                s = lax.dot_general(q2, piece, nt, preferred_element_type=f32)
                scores.append(s if bias_fn is None else bias_fn(s))
            n_top = g * Q_GROUP + par
            sk = jnp.where(top, sink_ref[n_top], sink_ref[n_top + 2])
            parts.append(_sink_softmax(scores, sk))
        (p_e, d_e), (p_o, d_o) = parts
        o2 = None
        for probs, vals in ((p_e, vvar[g][0]), (p_o, vvar[g][1])):
            for p, v in zip(probs, vals):
                o2 = _dot(p, v) if o2 is None else o2 + _dot(p, v)
        o2 = o2 * jnp.where(low, 1.0 / d_e, 1.0 / d_o)
        o_ref[:, c0:c0 + LANES] = o2[:nq].astype(o_ref.dtype)
        o_ref[:, c0 + LANES:c0 + 2 * LANES] = o2[nq:].astype(o_ref.dtype)


CTX_ATTN_PER_STEP = 2


def _ctx_attn_kernel(sink_ref, q_ref, k_ref, v_ref, o_ref, *, seq):
    for s in range(CTX_ATTN_PER_STEP):
        rows = slice(s * seq, (s + 1) * seq)
        kvar = [[[v] for v in head] for head in _half_lane_variants(k_ref[rows, :])]
        vvar = [[[v] for v in head] for head in _half_lane_variants(v_ref[rows, :])]
        _sink_heads(sink_ref, q_ref.at[rows, :], kvar, vvar, [None], o_ref.at[rows, :])


def _ctx_attn_call(sink, q, k, v, *, batch, seq):
    per = CTX_ATTN_PER_STEP
    assert batch % per == 0
    return pl.pallas_call(
        functools.partial(_ctx_attn_kernel, seq=seq),
        out_shape=jax.ShapeDtypeStruct((batch * seq, ATTN_WIDTH), bf16),
        grid=(batch // per,),
        in_specs=[
            pl.BlockSpec(memory_space=pltpu.SMEM),
            pl.BlockSpec((per * seq, ATTN_WIDTH), lambda b: (b, 0)),
            pl.BlockSpec((per * seq, KV_WIDTH), lambda b: (b, 0)),
            pl.BlockSpec((per * seq, KV_WIDTH), lambda b: (b, 0)),
        ],
        out_specs=pl.BlockSpec((per * seq, ATTN_WIDTH), lambda b: (b, 0)),
        compiler_params=_cparams(("parallel",), 32),
        name="ctx_attn",
    )(sink, q, k, v)


ATTN_PREP_ROWS = 256


def _lat_attn_kernel(sink_ref, q_ref, k_ref, v_ref, kx_ref, vx_ref, o_ref, kvar_ref, vvar_ref,
                     *, n_blocks, past):
    n = pl.program_id(1)
    seq = n_blocks * BLOCK
    ctx0 = seq + 2 * BLOCK

    @pl.when(n == 0)
    def _():
        zeros = jnp.zeros((BLOCK, KV_WIDTH), bf16)
        for src, ctx, dst in ((k_ref, kx_ref, kvar_ref), (v_ref, vx_ref, vvar_ref)):
            for i in range(2 * KV_HEADS):
                dst[i, 0:BLOCK, :] = zeros
                dst[i, BLOCK + seq:ctx0, :] = zeros
            for r0 in range(0, seq + past, ATTN_PREP_ROWS):
                rows = min(ATTN_PREP_ROWS, seq + past - r0)
                x = src[r0:r0 + rows, :] if r0 < seq else ctx[r0 - seq:r0 - seq + rows, :]
                d0 = BLOCK + r0 if r0 < seq else ctx0 + r0 - seq
                variants = _half_lane_variants(x)
                for g in range(KV_HEADS):
                    for par in range(2):
                        dst[2 * g + par, d0:d0 + rows, :] = variants[g][par]

    assert WINDOW == BLOCK
    band0 = pl.multiple_of(n * BLOCK, BLOCK)

    def pieces(ref):
        return [[[ref[2 * g + par, pl.ds(band0, 3 * BLOCK), :], ref[2 * g + par, ctx0:ctx0 + past, :]]
                 for par in range(2)] for g in range(KV_HEADS)]

    a_idx = lax.broadcasted_iota(jnp.int32, (2 * BLOCK, BLOCK), 0) % BLOCK
    c_idx = lax.broadcasted_iota(jnp.int32, (2 * BLOCK, BLOCK), 1)
    bias_p = jnp.where((c_idx >= a_idx) & (n > 0), 0.0, NEG).astype(f32)
    bias_n = jnp.where((c_idx <= a_idx) & (n < n_blocks - 1), 0.0, NEG).astype(f32)

    def band_bias(s):
        return jnp.concatenate([s[:, :BLOCK] + bias_p, s[:, BLOCK:2 * BLOCK],
                                s[:, 2 * BLOCK:] + bias_n], axis=1)

    _sink_heads(sink_ref, q_ref, pieces(kvar_ref), pieces(vvar_ref), [band_bias, None], o_ref)


def _lat_attn_call(sink, q, k, v, cache_k4, cache_v4, *, layer, batch, seq):
    nb = seq // BLOCK
    past = cache_k4.shape[2]
    assert seq % ATTN_PREP_ROWS == 0
    whole = pl.BlockSpec((seq, KV_WIDTH), lambda b, n: (b, 0))
    ctx = pl.BlockSpec((None, None, past, KV_WIDTH), lambda b, n: (b, layer, 0, 0))
    return pl.pallas_call(
        functools.partial(_lat_attn_kernel, n_blocks=nb, past=past),
        out_shape=jax.ShapeDtypeStruct((batch * seq, ATTN_WIDTH), bf16),
        grid=(batch, nb),
        in_specs=[
            pl.BlockSpec(memory_space=pltpu.SMEM),
            pl.BlockSpec((BLOCK, ATTN_WIDTH), lambda b, n: (b * nb + n, 0)),
            whole, whole, ctx, ctx,
        ],
        out_specs=pl.BlockSpec((BLOCK, ATTN_WIDTH), lambda b, n: (b * nb + n, 0)),
        scratch_shapes=[pltpu.VMEM((2 * KV_HEADS, seq + 2 * BLOCK + past, KV_WIDTH), bf16)] * 2,
        compiler_params=_cparams(("parallel", "arbitrary"), 32),
        name="lat_attn",
    )(sink, q, k, v, cache_k4, cache_v4)


def _softplus(x):
    return jnp.maximum(x, 0.0) + jnp.log1p(jnp.exp(-jnp.abs(x)))


def _gelu_tanh(x):
    return 0.5 * x * (1.0 + jnp.tanh(math.sqrt(2.0 / math.pi) * (x + 0.044715 * (x * x * x))))


SCAN_SKEW = SUBLANES // 2


def _scan_layout(seq):
    seg = seq // LRU_SEGMENTS
    assert seg % SUBLANES == 0 and LRU_SEGMENTS % 2 == 0
    pitch = seg + SCAN_SKEW
    pair = 2 * seg
    return pitch, pair, (LRU_SEGMENTS // 2) * (pair + SUBLANES)


def _scan_row(t, pair):
    return t + SUBLANES * (t // pair)


def _lru_kernel(xr_ref, yr_ref, cw_ref, cb_ref, wd_ref, bias_ref, lam_ref, h0_ref,
                rec_ref, st_ref, xs_ref, af_ref, bf_ref, ab_ref, bb_ref, *, seq):
    seg = seq // LRU_SEGMENTS
    chunk = min(seq, 256)
    hw = LRU_HALF
    n_lt = hw // LANES
    pitch, pair, _ = _scan_layout(seq)
    piece = min(pair, chunk)
    assert pair % piece == 0

    for m in range(LRU_SEGMENTS // 2):
        g0 = m * (pair + SUBLANES) + pair
        for t in range(n_lt):
            for a_ref, b_ref in ((af_ref, bf_ref), (ab_ref, bb_ref)):
                a_ref[t, g0:g0 + SUBLANES, :] = jnp.ones((SUBLANES, LANES), f32)
                b_ref[t, g0:g0 + SUBLANES, :] = jnp.zeros((SUBLANES, LANES), f32)

    zeros_pad = jnp.zeros((PAD_ROWS, hw), f32)
    xs_ref[0:PAD_ROWS, :] = zeros_pad
    xs_ref[PAD_ROWS + seq:2 * PAD_ROWS + seq, :] = zeros_pad
    for c in range(seq // chunk):
        xs_ref[PAD_ROWS + c * chunk:PAD_ROWS + (c + 1) * chunk, :] = xr_ref[c * chunk:(c + 1) * chunk, :]

    neg_c_sp = [-LRU_C * _softplus(-lam_ref[d:d + 1, :]) for d in range(2)]
    a_refs = (af_ref, ab_ref)
    b_refs = (bf_ref, bb_ref)

    for c in range(seq // chunk):
        r0 = c * chunk
        xc = cb_ref[...]
        for j in range(CONV_W):
            s0 = PAD_ROWS + r0 + j - CONV_LEFT
            xc = xc + xs_ref[s0:s0 + chunk, :] * cw_ref[j:j + 1, :]
        y = _dot(xc.astype(bf16), wd_ref[...]) + bias_ref[...]
        for d in range(2):
            r = _sigmoid(y[:, (2 * d) * hw:(2 * d + 1) * hw])
            gi = _sigmoid(y[:, (2 * d + 1) * hw:(2 * d + 2) * hw])
            log_a = neg_c_sp[d] * r
            a = jnp.exp(log_a)
            z = jnp.tanh(log_a) * (-1.0 - a * a)
            b = jnp.where(z > 0.0, z * lax.rsqrt(z), 0.0) * (gi * xc)
            for p0 in range(0, chunk, piece):
                dst = _scan_row(r0 + p0, pair)
                for t in range(n_lt):
                    lanes = slice(t * LANES, (t + 1) * LANES)
                    a_refs[d][t, dst:dst + piece, :] = a[p0:p0 + piece, lanes]
                    b_refs[d][t, dst:dst + piece, :] = b[p0:p0 + piece, lanes]

    def step(j, carry):
        jf = pl.ds(j, LRU_SEGMENTS, stride=pitch)
        jb = pl.ds(pitch - 1 - j, LRU_SEGMENTS, stride=pitch)
        out = []
        for t in range(n_lt):
            hf, pf, hb, pb = carry[4 * t:4 * t + 4]
            a = af_ref[t, jf, :]
            hf = a * hf + bf_ref[t, jf, :]
            pf = a * pf
            bf_ref[t, jf, :] = hf
            af_ref[t, jf, :] = pf
            a = ab_ref[t, jb, :]
            hb = a * hb + bb_ref[t, jb, :]
            pb = a * pb
            bb_ref[t, jb, :] = hb
            ab_ref[t, jb, :] = pb
            out += [hf, pf, hb, pb]
        return tuple(out)

    z = jnp.zeros((LRU_SEGMENTS, LANES), f32)
    o = jnp.ones((LRU_SEGMENTS, LANES), f32)
    fin = lax.fori_loop(0, pitch, step, (z, o, z, o) * n_lt, unroll=2)

    sub = min(pair, 256)
    off = lax.broadcasted_iota(jnp.int32, (sub, LANES), 0)
    for t in range(n_lt):
        hf, pf, hb, pb = fin[4 * t:4 * t + 4]
        lanes = slice(t * LANES, (t + 1) * LANES)
        cf = h0_ref[0:1, lanes]
        carry_f = []
        for s in range(LRU_SEGMENTS):
            carry_f.append(cf)
            cf = hf[s:s + 1, :] + pf[s:s + 1, :] * cf
        cb = h0_ref[1:2, lanes]
        carry_b = [None] * LRU_SEGMENTS
        for s in reversed(range(LRU_SEGMENTS)):
            carry_b[s] = cb
            cb = hb[s:s + 1, :] + pb[s:s + 1, :] * cb
        st_ref[0:1, lanes] = cf
        st_ref[1:2, lanes] = cb
        for m in range(LRU_SEGMENTS // 2):
            for u in range(pair // sub):
                t0 = m * pair + u * sub
                rows = slice(t0, t0 + sub)
                src = slice(_scan_row(t0, pair), _scan_row(t0, pair) + sub)
                lo, hi = u * sub, (u + 1) * sub
                if hi <= pitch:
                    cfs, cbs = carry_f[2 * m], carry_b[2 * m]
                elif lo >= pitch:
                    cfs, cbs = carry_f[2 * m + 1], carry_b[2 * m + 1]
                else:
                    first = off < (pitch - lo)
                    cfs = jnp.where(first, carry_f[2 * m], carry_f[2 * m + 1])
                    cbs = jnp.where(first, carry_b[2 * m], carry_b[2 * m + 1])
                h = ((bf_ref[t, src, :] + af_ref[t, src, :] * cfs)
                     + (bb_ref[t, src, :] + ab_ref[t, src, :] * cbs))
                rec_ref[rows, lanes] = (h * _gelu_tanh(yr_ref[rows, lanes])).astype(rec_ref.dtype)


def _lru_call(xr, yr, conv_w, conv_b, wd, bias, lam, h0, *, layer, batch, seq):
    hw = LRU_HALF
    return pl.pallas_call(
        functools.partial(_lru_kernel, seq=seq),
        out_shape=[jax.ShapeDtypeStruct((batch * seq, LRU_WIDTH), bf16),
                   jax.ShapeDtypeStruct((batch, 2, LRU_WIDTH), f32)],
        grid=(batch, 2),
        in_specs=[
            pl.BlockSpec((seq, hw), lambda b, c: (b, c)),
            pl.BlockSpec((seq, hw), lambda b, c: (b, c)),
            pl.BlockSpec((None, CONV_W, hw), lambda b, c: (layer, 0, c)),
            pl.BlockSpec((None, 1, hw), lambda b, c: (layer, 0, c)),
            pl.BlockSpec((None, None, hw, 4 * hw), lambda b, c: (layer, c, 0, 0)),
            pl.BlockSpec((None, None, 1, 4 * hw), lambda b, c: (layer, c, 0, 0)),
            pl.BlockSpec((None, 2, hw), lambda b, c: (layer, 0, c)),
            pl.BlockSpec((None, 2, hw), lambda b, c: (b, 0, c)),
        ],
        out_specs=[pl.BlockSpec((seq, hw), lambda b, c: (b, c)),
                   pl.BlockSpec((None, 2, hw), lambda b, c: (b, 0, c))],
        scratch_shapes=([pltpu.VMEM((seq + 2 * PAD_ROWS, hw), f32)]
                        + [pltpu.VMEM((hw // LANES, _scan_layout(seq)[2], LANES), f32)] * 4),
        compiler_params=_cparams(("parallel", "parallel"), 40),
        name=f"lru_{seq}",
    )(xr, yr, conv_w, conv_b, wd, bias, lam, h0)


FOURIER_REV = 128
FOURIER_PAD = 16
FOURIER_STEP_ROWS = 2048


def _fourier_kernel(x_ref, csc_ref, ch_ref, sh_ref, rev_ref, o_ref, u_ref, *, seq, per_step, scale):
    half = seq // 2
    chunk = min(seq, 512)
    w = FOURIER_WIDTH
    for s in range(per_step):
        for c in range(seq // chunk):
            rows = slice(s * seq + c * chunk, s * seq + (c + 1) * chunk)
            for g in range(FOURIER_GROUPS):
                cols = slice(g * FOURIER_GW, (g + 1) * FOURIER_GW)
                dst = slice(s * w + g * FOURIER_GW, s * w + (g + 1) * FOURIER_GW)
                u = _dot(x_ref[rows, cols], csc_ref[...])
                u_ref[c * chunk:(c + 1) * chunk, dst] = u[:, :FOURIER_GW].astype(bf16)
                u_ref[seq + c * chunk:seq + (c + 1) * chunk, dst] = u[:, FOURIER_GW:].astype(bf16)
    a = _dot(ch_ref[...], u_ref[0:seq, :])
    b = _dot(sh_ref[...], u_ref[seq:2 * seq, :])
    top = ((a[0:half] - b[0:half]) * scale).astype(o_ref.dtype)
    z = ((a[1:half + 1] + b[1:half + 1]) * scale).astype(bf16)
    nb = half // FOURIER_REV
    for s in range(per_step):
        cols = slice(s * w, (s + 1) * w)
        o_ref[s * seq:s * seq + half, :] = top[:, cols]
        for k in range(nb):
            blk = z[(nb - 1 - k) * FOURIER_REV:(nb - k) * FOURIER_REV, cols]
            r0 = s * seq + half + k * FOURIER_REV
            o_ref[r0:r0 + FOURIER_REV, :] = _dot(rev_ref[...], blk).astype(o_ref.dtype)


def _dft_tables(n):
    k = np.arange(n, dtype=np.int64)
    ang = (2.0 * np.pi / n) * ((k[:, None] * k[None, :]) % n).astype(np.float64)
    return np.cos(ang), np.sin(ang)


def _fourier_call(xf, csc, ch, sh, rev, *, batch, seq):
    scale = 1.0 / math.sqrt(seq * FOURIER_GW)
    rows = seq // 2 + FOURIER_PAD
    per_step = max(1, min(batch, FOURIER_STEP_ROWS // seq))
    assert batch % per_step == 0

    def const(shape):
        return pl.BlockSpec(shape, lambda b: (0, 0), pipeline_mode=pl.Buffered(1))

    return pl.pallas_call(
        functools.partial(_fourier_kernel, seq=seq, per_step=per_step, scale=scale),
        out_shape=jax.ShapeDtypeStruct((batch * seq, FOURIER_WIDTH), bf16),
        grid=(batch // per_step,),
        in_specs=[
            pl.BlockSpec((per_step * seq, FOURIER_WIDTH), lambda b: (b, 0)),
            const((FOURIER_GW, 2 * FOURIER_GW)),
            const((rows, seq)),
            const((rows, seq)),
            const((FOURIER_REV, FOURIER_REV)),
        ],
        out_specs=pl.BlockSpec((per_step * seq, FOURIER_WIDTH), lambda b: (b, 0)),
        scratch_shapes=[pltpu.VMEM((2 * seq, per_step * FOURIER_WIDTH), bf16)],
        compiler_params=_cparams(("parallel",), 40),
        name=f"fourier_{seq}",
    )(xf, csc, ch, sh, rev)


MERGE_CAST_COLS = 512


def _merge_kernel(x_ref, sh_ref, sc_ref, gt_ref, nw_ref, at_ref, rc_ref, fr_ref,
                  wbg32_ref, bbg_ref, wao32_ref, wlo32_ref, wfo32_ref, wo32_ref, o_ref,
                  wbg_ref, wao_ref, wlo_ref, wfo_ref, wo_ref):
    @pl.when(pl.program_id(0) == 0)
    def _():
        for src, dst in ((wbg32_ref, wbg_ref), (wao32_ref, wao_ref), (wlo32_ref, wlo_ref),
                         (wfo32_ref, wfo_ref), (wo32_ref, wo_ref)):
            for c0 in range(0, src.shape[1], MERGE_CAST_COLS):
                dst[:, c0:c0 + MERGE_CAST_COLS] = src[:, c0:c0 + MERGE_CAST_COLS].astype(bf16)

    x = x_ref[...]
    h = _modnorm(x, nw_ref[...], sh_ref[...], sc_ref[...]).astype(bf16)
    merged = None
    for idx, (br_ref, w_ref) in enumerate(((at_ref, wao_ref), (rc_ref, wlo_ref), (fr_ref, wfo_ref))):
        cols = slice(idx * D_MODEL, (idx + 1) * D_MODEL)
        g = _sigmoid(_dot(h, wbg_ref[:, cols]) + bbg_ref[:, cols])
        y = g * _dot(br_ref[...], w_ref[...])
        merged = y if merged is None else merged + y
    out = _dot(merged.astype(bf16), wo_ref[...])
    o_ref[...] = x + gt_ref[...] * out


def _merge_call(x, mod5, norm_w4, attn, rec, four, wbg, bbg, wao, wlo, wfo, wo, *, layer, row_fn, tm, name):
    n_tok = x.shape[0]

    def const(shape):
        nd = len(shape)
        return pl.BlockSpec((None,) + shape, lambda i: (layer,) + (0,) * nd,
                            pipeline_mode=pl.Buffered(1))

    def branch():
        return pl.BlockSpec((tm, ATTN_WIDTH), lambda i: (i, 0))

    return pl.pallas_call(
        _merge_kernel,
        out_shape=jax.ShapeDtypeStruct((n_tok, D_MODEL), f32),
        grid=(n_tok // tm,),
        in_specs=[
            pl.BlockSpec((tm, D_MODEL), lambda i: (i, 0)),
            _mod_spec(layer, 3, row_fn),
            _mod_spec(layer, 4, row_fn),
            _mod_spec(layer, 5, row_fn),
            _vec_spec(layer, 1),
            branch(), branch(), branch(),
            const((D_MODEL, N_BRANCH * D_MODEL)),
            const((1, N_BRANCH * D_MODEL)),
            const((ATTN_WIDTH, D_MODEL)),
            const((LRU_WIDTH, D_MODEL)),
            const((FOURIER_WIDTH, D_MODEL)),
            const((D_MODEL, D_MODEL)),
        ],
        out_specs=pl.BlockSpec((tm, D_MODEL), lambda i: (i, 0)),
        scratch_shapes=[pltpu.VMEM((D_MODEL, N_BRANCH * D_MODEL), bf16),
                        pltpu.VMEM((ATTN_WIDTH, D_MODEL), bf16),
                        pltpu.VMEM((LRU_WIDTH, D_MODEL), bf16),
                        pltpu.VMEM((FOURIER_WIDTH, D_MODEL), bf16),
                        pltpu.VMEM((D_MODEL, D_MODEL), bf16)],
        compiler_params=_cparams(("arbitrary",), 48),
        name=name,
    )(x, mod5, mod5, mod5, norm_w4, attn, rec, four, wbg, bbg, wao, wlo, wfo, wo)


def _rope_lane_tables(n_tokens):
    rows = n_tokens // GRID_W
    row = jnp.repeat(jnp.arange(rows), GRID_W).astype(f32)
    col = jnp.tile(jnp.arange(GRID_W), rows).astype(f32)
    inv = ROPE_BASE ** (-jnp.arange(0, AXIS_ROT, 2, dtype=f32) / AXIS_ROT)
    ang = jnp.stack([row[:, None] * inv, col[:, None] * inv], axis=1)
    cos, sin = jnp.cos(ang), jnp.sin(ang)
    cos_h = jnp.stack([cos, cos], axis=2).reshape(n_tokens, HEAD_DIM)
    sin_h = jnp.stack([-sin, sin], axis=2).reshape(n_tokens, HEAD_DIM)
    reps = LANES // HEAD_DIM
    return jnp.tile(cos_h, (1, reps)), jnp.tile(sin_h, (1, reps))


def _lru_dense_weights(lru_wa, lru_wi, lru_ba, lru_bi):
    per = LRU_BLOCKS // 2
    hw = LRU_HALF
    eye = jnp.eye(per, dtype=bf16)

    def dense(w, c):
        wh = w[:, :, c * per:(c + 1) * per].astype(bf16)
        d = wh[:, :, :, :, None, :] * eye[None, None, :, None, :, None]
        return d.reshape(DEPTH, 2, hw, hw)

    halves, biases = [], []
    for c in range(2):
        sl = slice(c * hw, (c + 1) * hw)
        da, di = dense(lru_wa, c), dense(lru_wi, c)
        halves.append(jnp.concatenate([da[:, 0], di[:, 0], da[:, 1], di[:, 1]], axis=-1))
        biases.append(jnp.concatenate(
            [lru_ba[:, 0, sl], lru_bi[:, 0, sl], lru_ba[:, 1, sl], lru_bi[:, 1, sl]], axis=-1))
    wd = jnp.stack(halves, axis=1)
    bias = jnp.stack(biases, axis=1)[:, :, None, :]
    return wd, bias


def kernel(x_prompt, x_sample, c, cache_k, cache_v, state_lru, c_ctx, w_ada, b_ada, norm_w, final_norm_w,
           ffn1_wg, ffn1_wu, ffn1_wd, ffn2_wg, ffn2_wu, ffn2_wd, w_in, w_branch_gate, b_branch_gate,
           attn_sink, w_attn_out, conv_w, conv_b, lru_wa, lru_ba, lru_wi, lru_bi, lru_lambda,
           w_lru_out, w_fourier_out, w_o):
    batch, seq, _ = x_prompt.shape
    dec_batch, dec_seq, _ = x_sample.shape
    past = cache_k.shape[2]
    assert 1 + dec_batch <= COND_ROWS

    cond = jnp.concatenate([c_ctx[None, :], c, jnp.zeros((COND_ROWS - 1 - dec_batch, D_MODEL), f32)], axis=0)
    mod = _ada_call(cond, w_ada, b_ada)
    mod5 = mod.reshape(DEPTH, COND_ROWS, N_SUB * 3, D_MODEL).transpose(0, 2, 1, 3)[:, :, :, None, :]
    norm_w4 = norm_w[:, :, None, :]
    final_w = final_norm_w[None, :]

    w_in_bf = w_in.astype(bf16)
    bbg = b_branch_gate[:, None, :]
    merge_w = (w_branch_gate, bbg, w_attn_out, w_lru_out, w_fourier_out, w_o)
    lru_wd, lru_bias = _lru_dense_weights(lru_wa, lru_wi, lru_ba, lru_bi)
    conv_b3 = conv_b[:, None, :]
    rope_tabs = _rope_lane_tables(dec_seq)
    cc, sc_ = _dft_tables(FOURIER_GW)
    csc = jnp.asarray(np.concatenate([cc, sc_], axis=1), f32).astype(bf16)
    dft = {}
    for n in (seq, dec_seq):
        cl, sl = _dft_tables(n)
        rows = n // 2 + FOURIER_PAD
        dft[n] = (jnp.asarray(cl[:rows], f32).astype(bf16), jnp.asarray(sl[:rows], f32).astype(bf16))
    rev = jnp.asarray(np.eye(FOURIER_REV)[::-1], f32).astype(bf16)
    cache_k4 = cache_k.reshape(dec_batch, DEPTH, past, KV_WIDTH)
    cache_v4 = cache_v.reshape(dec_batch, DEPTH, past, KV_WIDTH)
    h0_ctx = jnp.zeros((batch, 2, LRU_WIDTH), f32)

    tm_p, tm_s = 2048, 2048
    row_p = lambda i: 0
    row_s = lambda i: 1 + (i * tm_s) // dec_seq
    tmi = 1024
    row_pi = lambda i: 0
    row_si = lambda i: 1 + (i * tmi) // dec_seq
    tmm = 512
    row_pm = lambda i: 0
    row_sm = lambda i: 1 + (i * tmm) // dec_seq

    xp = x_prompt.reshape(batch * seq, D_MODEL)
    xs = x_sample.reshape(dec_batch * dec_seq, D_MODEL)
    ks, vs, ss = [], [], []
    for l in range(DEPTH):
        last = l == DEPTH - 1
        sink = attn_sink[l]
        xp = _ffn_call(xp, mod5, norm_w4, ffn1_wg, ffn1_wu, ffn1_wd, None,
                       layer=l, sub=0, row_fn=row_p, tm=tm_p, name="ffn1_ctx")
        xs = _ffn_call(xs, mod5, norm_w4, ffn1_wg, ffn1_wu, ffn1_wd, None,
                       layer=l, sub=0, row_fn=row_s, tm=tm_s, name="ffn1_lat")
        q, k, v, xr, yr, xf = _inproj_call(xp, mod5, norm_w4, w_in_bf, None,
                                           layer=l, row_fn=row_pi, tm=tmi, seq=seq, name="inproj_ctx")
        ks.append(k.reshape(batch, seq, KV_HEADS, HEAD_DIM))
        vs.append(v.reshape(batch, seq, KV_HEADS, HEAD_DIM))
        attn = _ctx_attn_call(sink, q, k, v, batch=batch, seq=seq)
        rec, st = _lru_call(xr, yr, conv_w, conv_b3, lru_wd, lru_bias, lru_lambda, h0_ctx,
                            layer=l, batch=batch, seq=seq)
        ss.append(st)
        four = _fourier_call(xf, csc, *dft[seq], rev, batch=batch, seq=seq)
        xp = _merge_call(xp, mod5, norm_w4, attn, rec, four, *merge_w,
                         layer=l, row_fn=row_pm, tm=tmm, name="merge_ctx")
        q, k, v, xr, yr, xf = _inproj_call(xs, mod5, norm_w4, w_in_bf, rope_tabs,
                                           layer=l, row_fn=row_si, tm=tmi, seq=dec_seq, name="inproj_lat")
        attn = _lat_attn_call(sink, q, k, v, cache_k4, cache_v4, layer=l, batch=dec_batch, seq=dec_seq)
        rec, _ = _lru_call(xr, yr, conv_w, conv_b3, lru_wd, lru_bias, lru_lambda, state_lru[:, l],
                           layer=l, batch=dec_batch, seq=dec_seq)
        four = _fourier_call(xf, csc, *dft[dec_seq], rev, batch=dec_batch, seq=dec_seq)
        xs = _merge_call(xs, mod5, norm_w4, attn, rec, four, *merge_w,
                         layer=l, row_fn=row_sm, tm=tmm, name="merge_lat")
        xp = _ffn_call(xp, mod5, norm_w4, ffn2_wg, ffn2_wu, ffn2_wd, final_w if last else None,
                       layer=l, sub=2, row_fn=row_p, tm=tm_p, name="ffn2_ctx")
        xs = _ffn_call(xs, mod5, norm_w4, ffn2_wg, ffn2_wu, ffn2_wd, final_w if last else None,
                       layer=l, sub=2, row_fn=row_s, tm=tm_s, name="ffn2_lat")

    y_prompt = xp.reshape(batch, seq, D_MODEL)
    y_sample = xs.reshape(dec_batch, dec_seq, D_MODEL)
    return (y_prompt, y_sample, jnp.stack(ks, axis=1), jnp.stack(vs, axis=1), jnp.stack(ss, axis=1))
```

```python
import functools
import math

import numpy as np
import jax
import jax.numpy as jnp
from jax import lax
from jax.experimental import pallas as pl
from jax.experimental.pallas import tpu as pltpu

f32 = jnp.float32
bf16 = jnp.bfloat16

D_MODEL = 1024
DEPTH = 2
GRID_W = 64
N_HEADS = 8
KV_HEADS = 2
HEAD_DIM = 64
Q_GROUP = N_HEADS // KV_HEADS
ATTN_WIDTH = N_HEADS * HEAD_DIM
KV_WIDTH = KV_HEADS * HEAD_DIM
WINDOW = 128
BLOCK = 128
AXIS_ROT = HEAD_DIM // 2
ROPE_BASE = 10000.0
LRU_WIDTH = 512
LRU_BLOCKS = 8
LRU_BW = LRU_WIDTH // LRU_BLOCKS
LRU_C = 8.0
CONV_W = 4
CONV_LEFT = 2
FOURIER_WIDTH = 512
FOURIER_GROUPS = 4
FOURIER_GW = FOURIER_WIDTH // FOURIER_GROUPS
D_FF = 2816
N_BRANCH = 3
N_SUB = 3
EPS = 1e-6
NEG = -1e30
IN_WIDTH = ATTN_WIDTH + 2 * KV_WIDTH + 2 * LRU_WIDTH + FOURIER_WIDTH

V7X_VMEM_BYTES = 64 * 1024 * 1024
SUBLANES = 8
LANES = 128
MXU_N = 256
assert 2 * KV_WIDTH == MXU_N
COND_ROWS = 8
LRU_HALF = LRU_WIDTH // 2
LRU_SEGMENTS = SUBLANES
PAD_ROWS = SUBLANES

_MIB = 1024 * 1024


def _cparams(sem, vmem_mib):
    del vmem_mib
    return pltpu.CompilerParams(dimension_semantics=sem, vmem_limit_bytes=V7X_VMEM_BYTES)


def _dot(a, b):
    return jnp.dot(a, b, preferred_element_type=f32)


def _sigmoid(x):
    return jax.nn.sigmoid(x)


def _modnorm(x, nw, shift, scale):
    y = x * lax.rsqrt(jnp.mean(x * x, axis=-1, keepdims=True) + EPS)
    return y * (nw * (1.0 + scale)) + shift


def _ada_kernel(cond_ref, w_ref, b_ref, o_ref):
    c = cond_ref[...]
    s = (c * _sigmoid(c)).astype(bf16)
    o_ref[...] = _dot(s, w_ref[...].astype(bf16)) + b_ref[...]


def _ada_call(cond, w_ada, b_ada):
    n_out = w_ada.shape[-1]
    tn = 1024
    return pl.pallas_call(
        _ada_kernel,
        out_shape=jax.ShapeDtypeStruct((DEPTH, COND_ROWS, n_out), f32),
        grid=(DEPTH, n_out // tn),
        in_specs=[
            pl.BlockSpec((COND_ROWS, D_MODEL), lambda l, j: (0, 0)),
            pl.BlockSpec((None, D_MODEL, tn), lambda l, j: (l, 0, j)),
            pl.BlockSpec((None, 1, tn), lambda l, j: (l, 0, j)),
        ],
        out_specs=pl.BlockSpec((None, COND_ROWS, tn), lambda l, j: (l, 0, j)),
        compiler_params=_cparams(("parallel", "parallel"), 24),
        name="adaln",
    )(cond, w_ada, b_ada.reshape(DEPTH, 1, n_out))


def _mod_spec(layer, slot, row_fn):
    return pl.BlockSpec((None, None, None, 1, D_MODEL),
                        lambda i, *_: (layer, slot, row_fn(i), 0, 0))


def _vec_spec(layer, sub):
    return pl.BlockSpec((None, None, 1, D_MODEL), lambda i, *_: (layer, sub, 0, 0))


FFN_ROW_GROUPS = 4


def _ffn_kernel(*refs, n_ff, final):
    if final:
        (x_ref, sh_ref, sc_ref, gt_ref, nw_ref, wg_ref, wu_ref, wd_ref, fw_ref,
         o_ref, h_ref) = refs
    else:
        (x_ref, sh_ref, sc_ref, gt_ref, nw_ref, wg_ref, wu_ref, wd_ref,
         o_ref, h_ref) = refs
    j = pl.program_id(1)

    tm = x_ref.shape[0]
    groups = [slice(r * (tm // FFN_ROW_GROUPS), (r + 1) * (tm // FFN_ROW_GROUPS))
              for r in range(FFN_ROW_GROUPS)]

    def down(h, w):
        wg, wu, wd = w
        g = _dot(h, wg)
        u = _dot(h, wu)
        a = ((g * _sigmoid(g)) * u).astype(bf16)
        return _dot(a, wd)

    def weights():
        return (wg_ref[...].astype(bf16), wu_ref[...].astype(bf16), wd_ref[...].astype(bf16))

    @pl.when(j == 0)
    def _():
        w = weights()
        for rows in groups:
            h = _modnorm(x_ref[rows, :], nw_ref[...], sh_ref[...], sc_ref[...]).astype(bf16)
            h_ref[rows, :] = h
            o_ref[rows, :] = down(h, w)

    @pl.when((j > 0) & (j < n_ff - 1))
    def _():
        w = weights()
        for rows in groups:
            o_ref[rows, :] += down(h_ref[rows, :], w)

    @pl.when(j == n_ff - 1)
    def _():
        w = weights()
        for rows in groups:
            y = x_ref[rows, :] + (0.5 * gt_ref[...]) * (o_ref[rows, :] + down(h_ref[rows, :], w))
            if final:
                y = y * lax.rsqrt(jnp.mean(y * y, axis=-1, keepdims=True) + EPS) * fw_ref[...]
            o_ref[rows, :] = y


def _ffn_call(x, mod5, norm_w4, wg, wu, wd, final_w, *, layer, sub, row_fn, tm, name):
    n_tok = x.shape[0]
    tf = 256
    n_ff = D_FF // tf
    final = final_w is not None
    in_specs = [
        pl.BlockSpec((tm, D_MODEL), lambda i, j: (i, 0)),
        _mod_spec(layer, sub * 3 + 0, row_fn),
        _mod_spec(layer, sub * 3 + 1, row_fn),
        _mod_spec(layer, sub * 3 + 2, row_fn),
        _vec_spec(layer, sub),
        pl.BlockSpec((None, D_MODEL, tf), lambda i, j: (layer, 0, j)),
        pl.BlockSpec((None, D_MODEL, tf), lambda i, j: (layer, 0, j)),
        pl.BlockSpec((None, tf, D_MODEL), lambda i, j: (layer, j, 0)),
    ]
    args = [x, mod5, mod5, mod5, norm_w4, wg, wu, wd]
    if final:
        in_specs.append(pl.BlockSpec((1, D_MODEL), lambda i, j: (0, 0)))
        args.append(final_w)
    return pl.pallas_call(
        functools.partial(_ffn_kernel, n_ff=n_ff, final=final),
        out_shape=jax.ShapeDtypeStruct((n_tok, D_MODEL), f32),
        grid=(n_tok // tm, n_ff),
        in_specs=in_specs,
        out_specs=pl.BlockSpec((tm, D_MODEL), lambda i, j: (i, 0)),
        scratch_shapes=[pltpu.VMEM((tm, D_MODEL), bf16)],
        compiler_params=_cparams(("parallel", "arbitrary"), 52),
        name=name,
    )(*args)


def _swap16(x):
    n = x.shape[-1]
    lane = lax.broadcasted_iota(jnp.int32, x.shape, x.ndim - 1)
    first = (lane % AXIS_ROT) < (AXIS_ROT // 2)
    return jnp.where(first, pltpu.roll(x, n - AXIS_ROT // 2, x.ndim - 1),
                     pltpu.roll(x, AXIS_ROT // 2, x.ndim - 1))


def _inproj_kernel(*refs, rope):
    if rope:
        (x_ref, sh_ref, sc_ref, nw_ref, w_ref, cos_ref, sin_ref,
         q_ref, k_ref, v_ref, xr_ref, yr_ref, xf_ref) = refs
    else:
        (x_ref, sh_ref, sc_ref, nw_ref, w_ref,
         q_ref, k_ref, v_ref, xr_ref, yr_ref, xf_ref) = refs
    h = _modnorm(x_ref[...], nw_ref[...], sh_ref[...], sc_ref[...]).astype(bf16)
    if rope:
        cos = cos_ref[...]
        sin = sin_ref[...]

    def rot(t):
        return t * cos + _swap16(t) * sin if rope else t

    o = ATTN_WIDTH
    kv = _dot(h, w_ref[:, o:o + 2 * KV_WIDTH])
    k_ref[...] = rot(kv[:, :KV_WIDTH])
    v_ref[...] = kv[:, KV_WIDTH:]
    for c in range(ATTN_WIDTH // MXU_N):
        q = _dot(h, w_ref[:, c * MXU_N:(c + 1) * MXU_N])
        for t in range(MXU_N // LANES):
            c0 = c * MXU_N + t * LANES
            q_ref[:, c0:c0 + LANES] = rot(q[:, t * LANES:(t + 1) * LANES]).astype(bf16)
    o += 2 * KV_WIDTH
    xr_ref[...] = _dot(h, w_ref[:, o:o + LRU_WIDTH])
    o += LRU_WIDTH
    yr_ref[...] = _dot(h, w_ref[:, o:o + LRU_WIDTH])
    o += LRU_WIDTH
    xf_ref[...] = _dot(h, w_ref[:, o:o + FOURIER_WIDTH]).astype(bf16)


def _inproj_call(x, mod5, norm_w4, w_in_bf, rope_tabs, *, layer, row_fn, tm, seq, name):
    n_tok = x.shape[0]
    rope = rope_tabs is not None
    in_specs = [
        pl.BlockSpec((tm, D_MODEL), lambda i: (i, 0)),
        _mod_spec(layer, 3, row_fn),
        _mod_spec(layer, 4, row_fn),
        _vec_spec(layer, 1),
        pl.BlockSpec((None, D_MODEL, IN_WIDTH), lambda i: (layer, 0, 0)),
    ]
    args = [x, mod5, mod5, norm_w4, w_in_bf]
    if rope:
        per_seq = seq // tm
        in_specs += [pl.BlockSpec((tm, LANES), lambda i: (i % per_seq, 0))] * 2
        args += list(rope_tabs)
    widths = (ATTN_WIDTH, KV_WIDTH, KV_WIDTH, LRU_WIDTH, LRU_WIDTH, FOURIER_WIDTH)
    dtypes = (bf16, f32, f32, f32, f32, bf16)
    return pl.pallas_call(
        functools.partial(_inproj_kernel, rope=rope),
        out_shape=[jax.ShapeDtypeStruct((n_tok, w), d) for w, d in zip(widths, dtypes)],
        grid=(n_tok // tm,),
        in_specs=in_specs,
        out_specs=[pl.BlockSpec((tm, w), lambda i: (i, 0)) for w in widths],
        compiler_params=_cparams(("parallel",), 48),
        name=name,
    )(*args)


def _half_lane_variants(x):
    lane = lax.broadcasted_iota(jnp.int32, x.shape, 1)
    low = lane < HEAD_DIM
    sw = pltpu.roll(x, HEAD_DIM, 1)
    zero = jnp.zeros_like(x)
    head0 = (jnp.where(low, x, zero), jnp.where(low, zero, sw))
    head1 = (jnp.where(low, sw, zero), jnp.where(low, zero, x))
    return [tuple(v.astype(bf16) for v in h) for h in (head0, head1)]


def _sink_softmax(scores, sk):
    m = sk
    for s in scores:
        m = jnp.maximum(m, jnp.max(s, axis=-1, keepdims=True))
    denom = jnp.exp(sk - m)
    probs = []
    for s in scores:
        p = jnp.exp(s - m)
        denom = denom + jnp.sum(p, axis=-1, keepdims=True)
        probs.append(p.astype(bf16))
    return probs, denom


def _sink_heads(sink_ref, q_ref, kvar, vvar, bias_fns, o_ref):
    nq = q_ref.shape[0]
    top = lax.broadcasted_iota(jnp.int32, (2 * nq, 1), 0) < nq
    low = lax.broadcasted_iota(jnp.int32, (2 * nq, LANES), 1) < HEAD_DIM
    nt = (((1,), (1,)), ((), ()))
    for g in range(KV_HEADS):
        c0 = g * Q_GROUP * HEAD_DIM
        q2 = jnp.concatenate([q_ref[:, c0:c0 + LANES], q_ref[:, c0 + LANES:c0 + 2 * LANES]], axis=0)
        q2 = q2 * jnp.asarray(HEAD_DIM ** -0.5, bf16)
        parts = []
        for par in range(2):
            scores = []
            for piece, bias_fn in zip(kvar[g][par], bias_fns):
                s = lax.dot_general(q2, piece, nt, preferred_element_type=f32)
                scores.append(s if bias_fn is None else bias_fn(s))
            n_top = g * Q_GROUP + par
            sk = jnp.where(top, sink_ref[n_top], sink_ref[n_top + 2])
            parts.append(_sink_softmax(scores, sk))
        (p_e, d_e), (p_o, d_o) = parts
        o2 = None
        for probs, vals in ((p_e, vvar[g][0]), (p_o, vvar[g][1])):
            for p, v in zip(probs, vals):
                o2 = _dot(p, v) if o2 is None else o2 + _dot(p, v)
        o2 = o2 * jnp.where(low, 1.0 / d_e, 1.0 / d_o)
        o_ref[:, c0:c0 + LANES] = o2[:nq].astype(o_ref.dtype)
        o_ref[:, c0 + LANES:c0 + 2 * LANES] = o2[nq:].astype(o_ref.dtype)


CTX_ATTN_PER_STEP = 2


def _ctx_attn_kernel(sink_ref, q_ref, k_ref, v_ref, o_ref, *, seq):
    for s in range(CTX_ATTN_PER_STEP):
        rows = slice(s * seq, (s + 1) * seq)
        kvar = [[[v] for v in head] for head in _half_lane_variants(k_ref[rows, :])]
        vvar = [[[v] for v in head] for head in _half_lane_variants(v_ref[rows, :])]
        _sink_heads(sink_ref, q_ref.at[rows, :], kvar, vvar, [None], o_ref.at[rows, :])


def _ctx_attn_call(sink, q, k, v, *, batch, seq):
    per = CTX_ATTN_PER_STEP
    assert batch % per == 0
    return pl.pallas_call(
        functools.partial(_ctx_attn_kernel, seq=seq),
        out_shape=jax.ShapeDtypeStruct((batch * seq, ATTN_WIDTH), bf16),
        grid=(batch // per,),
        in_specs=[
            pl.BlockSpec(memory_space=pltpu.SMEM),
            pl.BlockSpec((per * seq, ATTN_WIDTH), lambda b: (b, 0)),
            pl.BlockSpec((per * seq, KV_WIDTH), lambda b: (b, 0)),
            pl.BlockSpec((per * seq, KV_WIDTH), lambda b: (b, 0)),
        ],
        out_specs=pl.BlockSpec((per * seq, ATTN_WIDTH), lambda b: (b, 0)),
        compiler_params=_cparams(("parallel",), 32),
        name="ctx_attn",
    )(sink, q, k, v)


ATTN_PREP_ROWS = 256


def _lat_attn_kernel(sink_ref, q_ref, k_ref, v_ref, kx_ref, vx_ref, o_ref, kvar_ref, vvar_ref,
                     *, n_blocks, past):
    n = pl.program_id(1)
    seq = n_blocks * BLOCK
    ctx0 = seq + 2 * BLOCK

    @pl.when(n == 0)
    def _():
        zeros = jnp.zeros((BLOCK, KV_WIDTH), bf16)
        for src, ctx, dst in ((k_ref, kx_ref, kvar_ref), (v_ref, vx_ref, vvar_ref)):
            for i in range(2 * KV_HEADS):
                dst[i, 0:BLOCK, :] = zeros
                dst[i, BLOCK + seq:ctx0, :] = zeros
            for r0 in range(0, seq + past, ATTN_PREP_ROWS):
                rows = min(ATTN_PREP_ROWS, seq + past - r0)
                x = src[r0:r0 + rows, :] if r0 < seq else ctx[r0 - seq:r0 - seq + rows, :]
                d0 = BLOCK + r0 if r0 < seq else ctx0 + r0 - seq
                variants = _half_lane_variants(x)
                for g in range(KV_HEADS):
                    for par in range(2):
                        dst[2 * g + par, d0:d0 + rows, :] = variants[g][par]

    assert WINDOW == BLOCK
    band0 = pl.multiple_of(n * BLOCK, BLOCK)

    def pieces(ref):
        return [[[ref[2 * g + par, pl.ds(band0, 3 * BLOCK), :], ref[2 * g + par, ctx0:ctx0 + past, :]]
                 for par in range(2)] for g in range(KV_HEADS)]

    a_idx = lax.broadcasted_iota(jnp.int32, (2 * BLOCK, BLOCK), 0) % BLOCK
    c_idx = lax.broadcasted_iota(jnp.int32, (2 * BLOCK, BLOCK), 1)
    bias_p = jnp.where((c_idx >= a_idx) & (n > 0), 0.0, NEG).astype(f32)
    bias_n = jnp.where((c_idx <= a_idx) & (n < n_blocks - 1), 0.0, NEG).astype(f32)

    def band_bias(s):
        return jnp.concatenate([s[:, :BLOCK] + bias_p, s[:, BLOCK:2 * BLOCK],
                                s[:, 2 * BLOCK:] + bias_n], axis=1)

    _sink_heads(sink_ref, q_ref, pieces(kvar_ref), pieces(vvar_ref), [band_bias, None], o_ref)


def _lat_attn_call(sink, q, k, v, cache_k4, cache_v4, *, layer, batch, seq):
    nb = seq // BLOCK
    past = cache_k4.shape[2]
    assert seq % ATTN_PREP_ROWS == 0
    whole = pl.BlockSpec((seq, KV_WIDTH), lambda b, n: (b, 0))
    ctx = pl.BlockSpec((None, None, past, KV_WIDTH), lambda b, n: (b, layer, 0, 0))
    return pl.pallas_call(
        functools.partial(_lat_attn_kernel, n_blocks=nb, past=past),
        out_shape=jax.ShapeDtypeStruct((batch * seq, ATTN_WIDTH), bf16),
        grid=(batch, nb),
        in_specs=[
            pl.BlockSpec(memory_space=pltpu.SMEM),
            pl.BlockSpec((BLOCK, ATTN_WIDTH), lambda b, n: (b * nb + n, 0)),
            whole, whole, ctx, ctx,
        ],
        out_specs=pl.BlockSpec((BLOCK, ATTN_WIDTH), lambda b, n: (b * nb + n, 0)),
        scratch_shapes=[pltpu.VMEM((2 * KV_HEADS, seq + 2 * BLOCK + past, KV_WIDTH), bf16)] * 2,
        compiler_params=_cparams(("parallel", "arbitrary"), 32),
        name="lat_attn",
    )(sink, q, k, v, cache_k4, cache_v4)


def _softplus(x):
    return jnp.maximum(x, 0.0) + jnp.log1p(jnp.exp(-jnp.abs(x)))


def _gelu_tanh(x):
    return 0.5 * x * (1.0 + jnp.tanh(math.sqrt(2.0 / math.pi) * (x + 0.044715 * (x * x * x))))


SCAN_SKEW = SUBLANES // 2


def _scan_layout(seq):
    seg = seq // LRU_SEGMENTS
    assert seg % SUBLANES == 0 and LRU_SEGMENTS % 2 == 0
    pitch = seg + SCAN_SKEW
    pair = 2 * seg
    return pitch, pair, (LRU_SEGMENTS // 2) * (pair + SUBLANES)


def _scan_row(t, pair):
    return t + SUBLANES * (t // pair)


def _lru_kernel(xr_ref, yr_ref, cw_ref, cb_ref, wd_ref, bias_ref, lam_ref, h0_ref,
                rec_ref, st_ref, xs_ref, af_ref, bf_ref, ab_ref, bb_ref, *, seq):
    seg = seq // LRU_SEGMENTS
    chunk = min(seq, 256)
    hw = LRU_HALF
    n_lt = hw // LANES
    pitch, pair, _ = _scan_layout(seq)
    piece = min(pair, chunk)
    assert pair % piece == 0

    for m in range(LRU_SEGMENTS // 2):
        g0 = m * (pair + SUBLANES) + pair
        for t in range(n_lt):
            for a_ref, b_ref in ((af_ref, bf_ref), (ab_ref, bb_ref)):
                a_ref[t, g0:g0 + SUBLANES, :] = jnp.ones((SUBLANES, LANES), f32)
                b_ref[t, g0:g0 + SUBLANES, :] = jnp.zeros((SUBLANES, LANES), f32)

    zeros_pad = jnp.zeros((PAD_ROWS, hw), f32)
    xs_ref[0:PAD_ROWS, :] = zeros_pad
    xs_ref[PAD_ROWS + seq:2 * PAD_ROWS + seq, :] = zeros_pad
    for c in range(seq // chunk):
        xs_ref[PAD_ROWS + c * chunk:PAD_ROWS + (c + 1) * chunk, :] = xr_ref[c * chunk:(c + 1) * chunk, :]

    neg_c_sp = [-LRU_C * _softplus(-lam_ref[d:d + 1, :]) for d in range(2)]
    a_refs = (af_ref, ab_ref)
    b_refs = (bf_ref, bb_ref)

    for c in range(seq // chunk):
        r0 = c * chunk
        xc = cb_ref[...]
        for j in range(CONV_W):
            s0 = PAD_ROWS + r0 + j - CONV_LEFT
            xc = xc + xs_ref[s0:s0 + chunk, :] * cw_ref[j:j + 1, :]
        y = _dot(xc.astype(bf16), wd_ref[...]) + bias_ref[...]
        for d in range(2):
            r = _sigmoid(y[:, (2 * d) * hw:(2 * d + 1) * hw])
            gi = _sigmoid(y[:, (2 * d + 1) * hw:(2 * d + 2) * hw])
            log_a = neg_c_sp[d] * r
            a = jnp.exp(log_a)
            z = jnp.tanh(log_a) * (-1.0 - a * a)
            b = jnp.where(z > 0.0, z * lax.rsqrt(z), 0.0) * (gi * xc)
            for p0 in range(0, chunk, piece):
                dst = _scan_row(r0 + p0, pair)
                for t in range(n_lt):
                    lanes = slice(t * LANES, (t + 1) * LANES)
                    a_refs[d][t, dst:dst + piece, :] = a[p0:p0 + piece, lanes]
                    b_refs[d][t, dst:dst + piece, :] = b[p0:p0 + piece, lanes]

    def step(j, carry):
        jf = pl.ds(j, LRU_SEGMENTS, stride=pitch)
        jb = pl.ds(pitch - 1 - j, LRU_SEGMENTS, stride=pitch)
        out = []
        for t in range(n_lt):
            hf, pf, hb, pb = carry[4 * t:4 * t + 4]
            a = af_ref[t, jf, :]
            hf = a * hf + bf_ref[t, jf, :]
            pf = a * pf
            bf_ref[t, jf, :] = hf
            af_ref[t, jf, :] = pf
            a = ab_ref[t, jb, :]
            hb = a * hb + bb_ref[t, jb, :]
            pb = a * pb
            bb_ref[t, jb, :] = hb
            ab_ref[t, jb, :] = pb
            out += [hf, pf, hb, pb]
        return tuple(out)

    z = jnp.zeros((LRU_SEGMENTS, LANES), f32)
    o = jnp.ones((LRU_SEGMENTS, LANES), f32)
    fin = lax.fori_loop(0, pitch, step, (z, o, z, o) * n_lt, unroll=2)

    sub = min(pair, 256)
    off = lax.broadcasted_iota(jnp.int32, (sub, LANES), 0)
    for t in range(n_lt):
        hf, pf, hb, pb = fin[4 * t:4 * t + 4]
        lanes = slice(t * LANES, (t + 1) * LANES)
        cf = h0_ref[0:1, lanes]
        carry_f = []
        for s in range(LRU_SEGMENTS):
            carry_f.append(cf)
            cf = hf[s:s + 1, :] + pf[s:s + 1, :] * cf
        cb = h0_ref[1:2, lanes]
        carry_b = [None] * LRU_SEGMENTS
        for s in reversed(range(LRU_SEGMENTS)):
            carry_b[s] = cb
            cb = hb[s:s + 1, :] + pb[s:s + 1, :] * cb
        st_ref[0:1, lanes] = cf
        st_ref[1:2, lanes] = cb
        for m in range(LRU_SEGMENTS // 2):
            for u in range(pair // sub):
                t0 = m * pair + u * sub
                rows = slice(t0, t0 + sub)
                src = slice(_scan_row(t0, pair), _scan_row(t0, pair) + sub)
                lo, hi = u * sub, (u + 1) * sub
                if hi <= pitch:
                    cfs, cbs = carry_f[2 * m], carry_b[2 * m]
                elif lo >= pitch:
                    cfs, cbs = carry_f[2 * m + 1], carry_b[2 * m + 1]
                else:
                    first = off < (pitch - lo)
                    cfs = jnp.where(first, carry_f[2 * m], carry_f[2 * m + 1])
                    cbs = jnp.where(first, carry_b[2 * m], carry_b[2 * m + 1])
                h = ((bf_ref[t, src, :] + af_ref[t, src, :] * cfs)
                     + (bb_ref[t, src, :] + ab_ref[t, src, :] * cbs))
                rec_ref[rows, lanes] = (h * _gelu_tanh(yr_ref[rows, lanes])).astype(rec_ref.dtype)


def _lru_call(xr, yr, conv_w, conv_b, wd, bias, lam, h0, *, layer, batch, seq):
    hw = LRU_HALF
    return pl.pallas_call(
        functools.partial(_lru_kernel, seq=seq),
        out_shape=[jax.ShapeDtypeStruct((batch * seq, LRU_WIDTH), bf16),
                   jax.ShapeDtypeStruct((batch, 2, LRU_WIDTH), f32)],
        grid=(batch, 2),
        in_specs=[
            pl.BlockSpec((seq, hw), lambda b, c: (b, c)),
            pl.BlockSpec((seq, hw), lambda b, c: (b, c)),
            pl.BlockSpec((None, CONV_W, hw), lambda b, c: (layer, 0, c)),
            pl.BlockSpec((None, 1, hw), lambda b, c: (layer, 0, c)),
            pl.BlockSpec((None, None, hw, 4 * hw), lambda b, c: (layer, c, 0, 0)),
            pl.BlockSpec((None, None, 1, 4 * hw), lambda b, c: (layer, c, 0, 0)),
            pl.BlockSpec((None, 2, hw), lambda b, c: (layer, 0, c)),
            pl.BlockSpec((None, 2, hw), lambda b, c: (b, 0, c)),
        ],
        out_specs=[pl.BlockSpec((seq, hw), lambda b, c: (b, c)),
                   pl.BlockSpec((None, 2, hw), lambda b, c: (b, 0, c))],
        scratch_shapes=([pltpu.VMEM((seq + 2 * PAD_ROWS, hw), f32)]
                        + [pltpu.VMEM((hw // LANES, _scan_layout(seq)[2], LANES), f32)] * 4),
        compiler_params=_cparams(("parallel", "parallel"), 40),
        name=f"lru_{seq}",
    )(xr, yr, conv_w, conv_b, wd, bias, lam, h0)


FOURIER_REV = 128
FOURIER_PAD = 16
FOURIER_STEP_ROWS = 2048


def _fourier_kernel(x_ref, csc_ref, ch_ref, sh_ref, rev_ref, o_ref, u_ref, *, seq, per_step, scale):
    half = seq // 2
    chunk = min(seq, 512)
    w = FOURIER_WIDTH
    for s in range(per_step):
        for c in range(seq // chunk):
            rows = slice(s * seq + c * chunk, s * seq + (c + 1) * chunk)
            for g in range(FOURIER_GROUPS):
                cols = slice(g * FOURIER_GW, (g + 1) * FOURIER_GW)
                dst = slice(s * w + g * FOURIER_GW, s * w + (g + 1) * FOURIER_GW)
                u = _dot(x_ref[rows, cols], csc_ref[...])
                u_ref[c * chunk:(c + 1) * chunk, dst] = u[:, :FOURIER_GW].astype(bf16)
                u_ref[seq + c * chunk:seq + (c + 1) * chunk, dst] = u[:, FOURIER_GW:].astype(bf16)
    a = _dot(ch_ref[...], u_ref[0:seq, :])
    b = _dot(sh_ref[...], u_ref[seq:2 * seq, :])
    top = ((a[0:half] - b[0:half]) * scale).astype(o_ref.dtype)
    z = ((a[1:half + 1] + b[1:half + 1]) * scale).astype(bf16)
    nb = half // FOURIER_REV
    for s in range(per_step):
        cols = slice(s * w, (s + 1) * w)
        o_ref[s * seq:s * seq + half, :] = top[:, cols]
        for k in range(nb):
            blk = z[(nb - 1 - k) * FOURIER_REV:(nb - k) * FOURIER_REV, cols]
            r0 = s * seq + half + k * FOURIER_REV
            o_ref[r0:r0 + FOURIER_REV, :] = _dot(rev_ref[...], blk).astype(o_ref.dtype)


def _dft_tables(n):
    k = np.arange(n, dtype=np.int64)
    ang = (2.0 * np.pi / n) * ((k[:, None] * k[None, :]) % n).astype(np.float64)
    return np.cos(ang), np.sin(ang)


def _fourier_call(xf, csc, ch, sh, rev, *, batch, seq):
    scale = 1.0 / math.sqrt(seq * FOURIER_GW)
    rows = seq // 2 + FOURIER_PAD
    per_step = max(1, min(batch, FOURIER_STEP_ROWS // seq))
    assert batch % per_step == 0

    def const(shape):
        return pl.BlockSpec(shape, lambda b: (0, 0), pipeline_mode=pl.Buffered(1))

    return pl.pallas_call(
        functools.partial(_fourier_kernel, seq=seq, per_step=per_step, scale=scale),
        out_shape=jax.ShapeDtypeStruct((batch * seq, FOURIER_WIDTH), bf16),
        grid=(batch // per_step,),
        in_specs=[
            pl.BlockSpec((per_step * seq, FOURIER_WIDTH), lambda b: (b, 0)),
            const((FOURIER_GW, 2 * FOURIER_GW)),
            const((rows, seq)),
            const((rows, seq)),
            const((FOURIER_REV, FOURIER_REV)),
        ],
        out_specs=pl.BlockSpec((per_step * seq, FOURIER_WIDTH), lambda b: (b, 0)),
        scratch_shapes=[pltpu.VMEM((2 * seq, per_step * FOURIER_WIDTH), bf16)],
        compiler_params=_cparams(("parallel",), 40),
        name=f"fourier_{seq}",
    )(xf, csc, ch, sh, rev)


MERGE_CAST_COLS = 512


def _merge_kernel(x_ref, sh_ref, sc_ref, gt_ref, nw_ref, at_ref, rc_ref, fr_ref,
                  wbg32_ref, bbg_ref, wao32_ref, wlo32_ref, wfo32_ref, wo32_ref, o_ref,
                  wbg_ref, wao_ref, wlo_ref, wfo_ref, wo_ref):
    @pl.when(pl.program_id(0) == 0)
    def _():
        for src, dst in ((wbg32_ref, wbg_ref), (wao32_ref, wao_ref), (wlo32_ref, wlo_ref),
                         (wfo32_ref, wfo_ref), (wo32_ref, wo_ref)):
            for c0 in range(0, src.shape[1], MERGE_CAST_COLS):
                dst[:, c0:c0 + MERGE_CAST_COLS] = src[:, c0:c0 + MERGE_CAST_COLS].astype(bf16)

    x = x_ref[...]
    h = _modnorm(x, nw_ref[...], sh_ref[...], sc_ref[...]).astype(bf16)
    merged = None
    for idx, (br_ref, w_ref) in enumerate(((at_ref, wao_ref), (rc_ref, wlo_ref), (fr_ref, wfo_ref))):
        cols = slice(idx * D_MODEL, (idx + 1) * D_MODEL)
        g = _sigmoid(_dot(h, wbg_ref[:, cols]) + bbg_ref[:, cols])
        y = g * _dot(br_ref[...], w_ref[...])
        merged = y if merged is None else merged + y
    out = _dot(merged.astype(bf16), wo_ref[...])
    o_ref[...] = x + gt_ref[...] * out


def _merge_call(x, mod5, norm_w4, attn, rec, four, wbg, bbg, wao, wlo, wfo, wo, *, layer, row_fn, tm, name):
    n_tok = x.shape[0]

    def const(shape):
        nd = len(shape)
        return pl.BlockSpec((None,) + shape, lambda i: (layer,) + (0,) * nd,
                            pipeline_mode=pl.Buffered(1))

    def branch():
        return pl.BlockSpec((tm, ATTN_WIDTH), lambda i: (i, 0))

    return pl.pallas_call(
        _merge_kernel,
        out_shape=jax.ShapeDtypeStruct((n_tok, D_MODEL), f32),
        grid=(n_tok // tm,),
        in_specs=[
            pl.BlockSpec((tm, D_MODEL), lambda i: (i, 0)),
            _mod_spec(layer, 3, row_fn),
            _mod_spec(layer, 4, row_fn),
            _mod_spec(layer, 5, row_fn),
            _vec_spec(layer, 1),
            branch(), branch(), branch(),
            const((D_MODEL, N_BRANCH * D_MODEL)),
            const((1, N_BRANCH * D_MODEL)),
            const((ATTN_WIDTH, D_MODEL)),
            const((LRU_WIDTH, D_MODEL)),
            const((FOURIER_WIDTH, D_MODEL)),
            const((D_MODEL, D_MODEL)),
        ],
        out_specs=pl.BlockSpec((tm, D_MODEL), lambda i: (i, 0)),
        scratch_shapes=[pltpu.VMEM((D_MODEL, N_BRANCH * D_MODEL), bf16),
                        pltpu.VMEM((ATTN_WIDTH, D_MODEL), bf16),
                        pltpu.VMEM((LRU_WIDTH, D_MODEL), bf16),
                        pltpu.VMEM((FOURIER_WIDTH, D_MODEL), bf16),
                        pltpu.VMEM((D_MODEL, D_MODEL), bf16)],
        compiler_params=_cparams(("arbitrary",), 48),
        name=name,
    )(x, mod5, mod5, mod5, norm_w4, attn, rec, four, wbg, bbg, wao, wlo, wfo, wo)


def _rope_lane_tables(n_tokens):
    rows = n_tokens // GRID_W
    row = jnp.repeat(jnp.arange(rows), GRID_W).astype(f32)
    col = jnp.tile(jnp.arange(GRID_W), rows).astype(f32)
    inv = ROPE_BASE ** (-jnp.arange(0, AXIS_ROT, 2, dtype=f32) / AXIS_ROT)
    ang = jnp.stack([row[:, None] * inv, col[:, None] * inv], axis=1)
    cos, sin = jnp.cos(ang), jnp.sin(ang)
    cos_h = jnp.stack([cos, cos], axis=2).reshape(n_tokens, HEAD_DIM)
    sin_h = jnp.stack([-sin, sin], axis=2).reshape(n_tokens, HEAD_DIM)
    reps = LANES // HEAD_DIM
    return jnp.tile(cos_h, (1, reps)), jnp.tile(sin_h, (1, reps))


def _lru_dense_weights(lru_wa, lru_wi, lru_ba, lru_bi):
    per = LRU_BLOCKS // 2
    hw = LRU_HALF
    eye = jnp.eye(per, dtype=bf16)

    def dense(w, c):
        wh = w[:, :, c * per:(c + 1) * per].astype(bf16)
        d = wh[:, :, :, :, None, :] * eye[None, None, :, None, :, None]
        return d.reshape(DEPTH, 2, hw, hw)

    halves, biases = [], []
    for c in range(2):
        sl = slice(c * hw, (c + 1) * hw)
        da, di = dense(lru_wa, c), dense(lru_wi, c)
        halves.append(jnp.concatenate([da[:, 0], di[:, 0], da[:, 1], di[:, 1]], axis=-1))
        biases.append(jnp.concatenate(
            [lru_ba[:, 0, sl], lru_bi[:, 0, sl], lru_ba[:, 1, sl], lru_bi[:, 1, sl]], axis=-1))
    wd = jnp.stack(halves, axis=1)
    bias = jnp.stack(biases, axis=1)[:, :, None, :]
    return wd, bias


def kernel(x_prompt, x_sample, c, cache_k, cache_v, state_lru, c_ctx, w_ada, b_ada, norm_w, final_norm_w,
           ffn1_wg, ffn1_wu, ffn1_wd, ffn2_wg, ffn2_wu, ffn2_wd, w_in, w_branch_gate, b_branch_gate,
           attn_sink, w_attn_out, conv_w, conv_b, lru_wa, lru_ba, lru_wi, lru_bi, lru_lambda,
           w_lru_out, w_fourier_out, w_o):
    batch, seq, _ = x_prompt.shape
    dec_batch, dec_seq, _ = x_sample.shape
    past = cache_k.shape[2]
    assert 1 + dec_batch <= COND_ROWS

    cond = jnp.concatenate([c_ctx[None, :], c, jnp.zeros((COND_ROWS - 1 - dec_batch, D_MODEL), f32)], axis=0)
    mod = _ada_call(cond, w_ada, b_ada)
    mod5 = mod.reshape(DEPTH, COND_ROWS, N_SUB * 3, D_MODEL).transpose(0, 2, 1, 3)[:, :, :, None, :]
    norm_w4 = norm_w[:, :, None, :]
    final_w = final_norm_w[None, :]

    w_in_bf = w_in.astype(bf16)
    bbg = b_branch_gate[:, None, :]
    merge_w = (w_branch_gate, bbg, w_attn_out, w_lru_out, w_fourier_out, w_o)
    lru_wd, lru_bias = _lru_dense_weights(lru_wa, lru_wi, lru_ba, lru_bi)
    conv_b3 = conv_b[:, None, :]
    rope_tabs = _rope_lane_tables(dec_seq)
    cc, sc_ = _dft_tables(FOURIER_GW)
    csc = jnp.asarray(np.concatenate([cc, sc_], axis=1), f32).astype(bf16)
    dft = {}
    for n in (seq, dec_seq):
        cl, sl = _dft_tables(n)
        rows = n // 2 + FOURIER_PAD
        dft[n] = (jnp.asarray(cl[:rows], f32).astype(bf16), jnp.asarray(sl[:rows], f32).astype(bf16))
    rev = jnp.asarray(np.eye(FOURIER_REV)[::-1], f32).astype(bf16)
    cache_k4 = cache_k.reshape(dec_batch, DEPTH, past, KV_WIDTH)
    cache_v4 = cache_v.reshape(dec_batch, DEPTH, past, KV_WIDTH)
    h0_ctx = jnp.zeros((batch, 2, LRU_WIDTH), f32)

    tm_p, tm_s = 2048, 2048
    row_p = lambda i: 0
    row_s = lambda i: 1 + (i * tm_s) // dec_seq
    tmi = 1024
    row_pi = lambda i: 0
    row_si = lambda i: 1 + (i * tmi) // dec_seq
    tmm = 512
    row_pm = lambda i: 0
    row_sm = lambda i: 1 + (i * tmm) // dec_seq

    xp = x_prompt.reshape(batch * seq, D_MODEL)
    xs = x_sample.reshape(dec_batch * dec_seq, D_MODEL)
    ks, vs, ss = [], [], []
    for l in range(DEPTH):
        last = l == DEPTH - 1
        sink = attn_sink[l]
        xp = _ffn_call(xp, mod5, norm_w4, ffn1_wg, ffn1_wu, ffn1_wd, None,
                       layer=l, sub=0, row_fn=row_p, tm=tm_p, name="ffn1_ctx")
        xs = _ffn_call(xs, mod5, norm_w4, ffn1_wg, ffn1_wu, ffn1_wd, None,
                       layer=l, sub=0, row_fn=row_s, tm=tm_s, name="ffn1_lat")
        q, k, v, xr, yr, xf = _inproj_call(xp, mod5, norm_w4, w_in_bf, None,
                                           layer=l, row_fn=row_pi, tm=tmi, seq=seq, name="inproj_ctx")
        ks.append(k.reshape(batch, seq, KV_HEADS, HEAD_DIM))
        vs.append(v.reshape(batch, seq, KV_HEADS, HEAD_DIM))
        attn = _ctx_attn_call(sink, q, k, v, batch=batch, seq=seq)
        rec, st = _lru_call(xr, yr, conv_w, conv_b3, lru_wd, lru_bias, lru_lambda, h0_ctx,
                            layer=l, batch=batch, seq=seq)
        ss.append(st)
        four = _fourier_call(xf, csc, *dft[seq], rev, batch=batch, seq=seq)
        xp = _merge_call(xp, mod5, norm_w4, attn, rec, four, *merge_w,
                         layer=l, row_fn=row_pm, tm=tmm, name="merge_ctx")
        q, k, v, xr, yr, xf = _inproj_call(xs, mod5, norm_w4, w_in_bf, rope_tabs,
                                           layer=l, row_fn=row_si, tm=tmi, seq=dec_seq, name="inproj_lat")
        attn = _lat_attn_call(sink, q, k, v, cache_k4, cache_v4, layer=l, batch=dec_batch, seq=dec_seq)
        rec, _ = _lru_call(xr, yr, conv_w, conv_b3, lru_wd, lru_bias, lru_lambda, state_lru[:, l],
                           layer=l, batch=dec_batch, seq=dec_seq)
        four = _fourier_call(xf, csc, *dft[dec_seq], rev, batch=dec_batch, seq=dec_seq)
        xs = _merge_call(xs, mod5, norm_w4, attn, rec, four, *merge_w,
                         layer=l, row_fn=row_sm, tm=tmm, name="merge_lat")
        xp = _ffn_call(xp, mod5, norm_w4, ffn2_wg, ffn2_wu, ffn2_wd, final_w if last else None,
                       layer=l, sub=2, row_fn=row_p, tm=tm_p, name="ffn2_ctx")
        xs = _ffn_call(xs, mod5, norm_w4, ffn2_wg, ffn2_wu, ffn2_wd, final_w if last else None,
                       layer=l, sub=2, row_fn=row_s, tm=tm_s, name="ffn2_lat")

    y_prompt = xp.reshape(batch, seq, D_MODEL)
    y_sample = xs.reshape(dec_batch, dec_seq, D_MODEL)
    return (y_prompt, y_sample, jnp.stack(ks, axis=1), jnp.stack(vs, axis=1), jnp.stack(ss, axis=1))
```

```python
import functools
import math

import numpy as np
import jax
import jax.numpy as jnp
from jax import lax
from jax.experimental import pallas as pl
from jax.experimental.pallas import tpu as pltpu

f32 = jnp.float32
bf16 = jnp.bfloat16

D_MODEL = 1024
DEPTH = 2
GRID_W = 64
N_HEADS = 8
KV_HEADS = 2
HEAD_DIM = 64
Q_GROUP = N_HEADS // KV_HEADS
ATTN_WIDTH = N_HEADS * HEAD_DIM
KV_WIDTH = KV_HEADS * HEAD_DIM
WINDOW = 128
BLOCK = 128
AXIS_ROT = HEAD_DIM // 2
ROPE_BASE = 10000.0
LRU_WIDTH = 512
LRU_BLOCKS = 8
LRU_BW = LRU_WIDTH // LRU_BLOCKS
LRU_C = 8.0
CONV_W = 4
CONV_LEFT = 2
FOURIER_WIDTH = 512
FOURIER_GROUPS = 4
FOURIER_GW = FOURIER_WIDTH // FOURIER_GROUPS
D_FF = 2816
N_BRANCH = 3
N_SUB = 3
EPS = 1e-6
NEG = -1e30
IN_WIDTH = ATTN_WIDTH + 2 * KV_WIDTH + 2 * LRU_WIDTH + FOURIER_WIDTH

V7X_VMEM_BYTES = 64 * 1024 * 1024
SUBLANES = 8
LANES = 128
MXU_N = 256
assert 2 * KV_WIDTH == MXU_N
COND_ROWS = 8
LRU_HALF = LRU_WIDTH // 2
LRU_SEGMENTS = SUBLANES
PAD_ROWS = SUBLANES

_MIB = 1024 * 1024


def _cparams(sem, vmem_mib):
    del vmem_mib
    return pltpu.CompilerParams(dimension_semantics=sem, vmem_limit_bytes=V7X_VMEM_BYTES)


def _dot(a, b):
    return jnp.dot(a, b, preferred_element_type=f32)


def _sigmoid(x):
    return jax.nn.sigmoid(x)


def _modnorm(x, nw, shift, scale):
    y = x * lax.rsqrt(jnp.mean(x * x, axis=-1, keepdims=True) + EPS)
    return y * (nw * (1.0 + scale)) + shift


def _ada_kernel(cond_ref, w_ref, b_ref, o_ref):
    c = cond_ref[...]
    s = (c * _sigmoid(c)).astype(bf16)
    o_ref[...] = _dot(s, w_ref[...].astype(bf16)) + b_ref[...]


def _ada_call(cond, w_ada, b_ada):
    n_out = w_ada.shape[-1]
    tn = 1024
    return pl.pallas_call(
        _ada_kernel,
        out_shape=jax.ShapeDtypeStruct((DEPTH, COND_ROWS, n_out), f32),
        grid=(DEPTH, n_out // tn),
        in_specs=[
            pl.BlockSpec((COND_ROWS, D_MODEL), lambda l, j: (0, 0)),
            pl.BlockSpec((None, D_MODEL, tn), lambda l, j: (l, 0, j)),
            pl.BlockSpec((None, 1, tn), lambda l, j: (l, 0, j)),
        ],
        out_specs=pl.BlockSpec((None, COND_ROWS, tn), lambda l, j: (l, 0, j)),
        compiler_params=_cparams(("parallel", "parallel"), 24),
        name="adaln",
    )(cond, w_ada, b_ada.reshape(DEPTH, 1, n_out))


def _mod_spec(layer, slot, row_fn):
    return pl.BlockSpec((None, None, None, 1, D_MODEL),
                        lambda i, *_: (layer, slot, row_fn(i), 0, 0))


def _vec_spec(layer, sub):
    return pl.BlockSpec((None, None, 1, D_MODEL), lambda i, *_: (layer, sub, 0, 0))


FFN_ROW_GROUPS = 4


def _ffn_kernel(*refs, n_ff, final):
    if final:
        (x_ref, sh_ref, sc_ref, gt_ref, nw_ref, wg_ref, wu_ref, wd_ref, fw_ref,
         o_ref, h_ref) = refs
    else:
        (x_ref, sh_ref, sc_ref, gt_ref, nw_ref, wg_ref, wu_ref, wd_ref,
         o_ref, h_ref) = refs
    j = pl.program_id(1)

    tm = x_ref.shape[0]
    groups = [slice(r * (tm // FFN_ROW_GROUPS), (r + 1) * (tm // FFN_ROW_GROUPS))
              for r in range(FFN_ROW_GROUPS)]

    def down(h, w):
        wg, wu, wd = w
        g = _dot(h, wg)
        u = _dot(h, wu)
        a = ((g * _sigmoid(g)) * u).astype(bf16)
        return _dot(a, wd)

    def weights():
        return (wg_ref[...].astype(bf16), wu_ref[...].astype(bf16), wd_ref[...].astype(bf16))

    @pl.when(j == 0)
    def _():
        w = weights()
        for rows in groups:
            h = _modnorm(x_ref[rows, :], nw_ref[...], sh_ref[...], sc_ref[...]).astype(bf16)
            h_ref[rows, :] = h
            o_ref[rows, :] = down(h, w)

    @pl.when((j > 0) & (j < n_ff - 1))
    def _():
        w = weights()
        for rows in groups:
            o_ref[rows, :] += down(h_ref[rows, :], w)

    @pl.when(j == n_ff - 1)
    def _():
        w = weights()
        for rows in groups:
            y = x_ref[rows, :] + (0.5 * gt_ref[...]) * (o_ref[rows, :] + down(h_ref[rows, :], w))
            if final:
                y = y * lax.rsqrt(jnp.mean(y * y, axis=-1, keepdims=True) + EPS) * fw_ref[...]
            o_ref[rows, :] = y


def _ffn_call(x, mod5, norm_w4, wg, wu, wd, final_w, *, layer, sub, row_fn, tm, name):
    n_tok = x.shape[0]
    tf = 256
    n_ff = D_FF // tf
    final = final_w is not None
    in_specs = [
        pl.BlockSpec((tm, D_MODEL), lambda i, j: (i, 0)),
        _mod_spec(layer, sub * 3 + 0, row_fn),
        _mod_spec(layer, sub * 3 + 1, row_fn),
        _mod_spec(layer, sub * 3 + 2, row_fn),
        _vec_spec(layer, sub),
        pl.BlockSpec((None, D_MODEL, tf), lambda i, j: (layer, 0, j)),
        pl.BlockSpec((None, D_MODEL, tf), lambda i, j: (layer, 0, j)),
        pl.BlockSpec((None, tf, D_MODEL), lambda i, j: (layer, j, 0)),
    ]
    args = [x, mod5, mod5, mod5, norm_w4, wg, wu, wd]
    if final:
        in_specs.append(pl.BlockSpec((1, D_MODEL), lambda i, j: (0, 0)))
        args.append(final_w)
    return pl.pallas_call(
        functools.partial(_ffn_kernel, n_ff=n_ff, final=final),
        out_shape=jax.ShapeDtypeStruct((n_tok, D_MODEL), f32),
        grid=(n_tok // tm, n_ff),
        in_specs=in_specs,
        out_specs=pl.BlockSpec((tm, D_MODEL), lambda i, j: (i, 0)),
        scratch_shapes=[pltpu.VMEM((tm, D_MODEL), bf16)],
        compiler_params=_cparams(("parallel", "arbitrary"), 52),
        name=name,
    )(*args)


def _swap16(x):
    n = x.shape[-1]
    lane = lax.broadcasted_iota(jnp.int32, x.shape, x.ndim - 1)
    first = (lane % AXIS_ROT) < (AXIS_ROT // 2)
    return jnp.where(first, pltpu.roll(x, n - AXIS_ROT // 2, x.ndim - 1),
                     pltpu.roll(x, AXIS_ROT // 2, x.ndim - 1))


INPROJ_ROW_GROUPS = 2


def _inproj_kernel(*refs, rope):
    if rope:
        (x_ref, sh_ref, sc_ref, nw_ref, w32_ref, cos_ref, sin_ref,
         q_ref, k_ref, v_ref, xr_ref, yr_ref, xf_ref, w_ref) = refs
    else:
        (x_ref, sh_ref, sc_ref, nw_ref, w32_ref,
         q_ref, k_ref, v_ref, xr_ref, yr_ref, xf_ref, w_ref) = refs

    @pl.when(pl.program_id(0) == 0)
    def _():
        for c0 in range(0, IN_WIDTH, MXU_N):
            w_ref[:, c0:c0 + MXU_N] = w32_ref[:, c0:c0 + MXU_N].astype(bf16)

    tm = x_ref.shape[0]
    for r in range(INPROJ_ROW_GROUPS):
        rows = slice(r * (tm // INPROJ_ROW_GROUPS), (r + 1) * (tm // INPROJ_ROW_GROUPS))
        h = _modnorm(x_ref[rows, :], nw_ref[...], sh_ref[...], sc_ref[...]).astype(bf16)
        if rope:
            cos = cos_ref[rows, :]
            sin = sin_ref[rows, :]

        def rot(t):
            return t * cos + _swap16(t) * sin if rope else t

        o = ATTN_WIDTH
        kv = _dot(h, w_ref[:, o:o + 2 * KV_WIDTH])
        k_ref[rows, :] = rot(kv[:, :KV_WIDTH])
        v_ref[rows, :] = kv[:, KV_WIDTH:]
        for c in range(ATTN_WIDTH // MXU_N):
            q = _dot(h, w_ref[:, c * MXU_N:(c + 1) * MXU_N])
            for t in range(MXU_N // LANES):
                c0 = c * MXU_N + t * LANES
                q_ref[rows, c0:c0 + LANES] = rot(q[:, t * LANES:(t + 1) * LANES]).astype(bf16)
        o += 2 * KV_WIDTH
        xr_ref[rows, :] = _dot(h, w_ref[:, o:o + LRU_WIDTH])
        o += LRU_WIDTH
        yr_ref[rows, :] = _dot(h, w_ref[:, o:o + LRU_WIDTH])
        o += LRU_WIDTH
        xf_ref[rows, :] = _dot(h, w_ref[:, o:o + FOURIER_WIDTH]).astype(bf16)


def _inproj_call(x, mod5, norm_w4, w_in, rope_tabs, *, layer, row_fn, tm, seq, name):
    n_tok = x.shape[0]
    rope = rope_tabs is not None
    in_specs = [
        pl.BlockSpec((tm, D_MODEL), lambda i: (i, 0)),
        _mod_spec(layer, 3, row_fn),
        _mod_spec(layer, 4, row_fn),
        _vec_spec(layer, 1),
        pl.BlockSpec((None, D_MODEL, IN_WIDTH), lambda i: (layer, 0, 0), pipeline_mode=pl.Buffered(1)),
    ]
    args = [x, mod5, mod5, norm_w4, w_in]
    if rope:
        per_seq = seq // tm
        in_specs += [pl.BlockSpec((tm, LANES), lambda i: (i % per_seq, 0))] * 2
        args += list(rope_tabs)
    widths = (ATTN_WIDTH, KV_WIDTH, KV_WIDTH, LRU_WIDTH, LRU_WIDTH, FOURIER_WIDTH)
    dtypes = (bf16, f32, f32, f32, f32, bf16)
    return pl.pallas_call(
        functools.partial(_inproj_kernel, rope=rope),
        out_shape=[jax.ShapeDtypeStruct((n_tok, w), d) for w, d in zip(widths, dtypes)],
        grid=(n_tok // tm,),
        in_specs=in_specs,
        out_specs=[pl.BlockSpec((tm, w), lambda i: (i, 0)) for w in widths],
        scratch_shapes=[pltpu.VMEM((D_MODEL, IN_WIDTH), bf16)],
        compiler_params=_cparams(("arbitrary",), 48),
        name=name,
    )(*args)


def _half_lane_variants(x):
    lane = lax.broadcasted_iota(jnp.int32, x.shape, 1)
    low = lane < HEAD_DIM
    sw = pltpu.roll(x, HEAD_DIM, 1)
    zero = jnp.zeros_like(x)
    head0 = (jnp.where(low, x, zero), jnp.where(low, zero, sw))
    head1 = (jnp.where(low, sw, zero), jnp.where(low, zero, x))
    return [tuple(v.astype(bf16) for v in h) for h in (head0, head1)]


def _sink_softmax(scores, sk):
    m = sk
    for s in scores:
        m = jnp.maximum(m, jnp.max(s, axis=-1, keepdims=True))
    denom = jnp.exp(sk - m)
    probs = []
    for s in scores:
        p = jnp.exp(s - m)
        denom = denom + jnp.sum(p, axis=-1, keepdims=True)
        probs.append(p.astype(bf16))
    return probs, denom


def _sink_heads(sink_ref, q_ref, kvar, vvar, bias_fns, o_ref):
    nq = q_ref.shape[0]
    top = lax.broadcasted_iota(jnp.int32, (2 * nq, 1), 0) < nq
    low = lax.broadcasted_iota(jnp.int32, (2 * nq, LANES), 1) < HEAD_DIM
    nt = (((1,), (1,)), ((), ()))
    for g in range(KV_HEADS):
        c0 = g * Q_GROUP * HEAD_DIM
        q2 = jnp.concatenate([q_ref[:, c0:c0 + LANES], q_ref[:, c0 + LANES:c0 + 2 * LANES]], axis=0)
        q2 = q2 * jnp.asarray(HEAD_DIM ** -0.5, bf16)
        parts = []
        for par in range(2):
            scores = []
            for piece, bias_fn in zip(kvar[g][par], bias_fns):
                s = lax.dot_general(q2, piece, nt, preferred_element_type=f32)
                scores.append(s if bias_fn is None else bias_fn(s))
            n_top = g * Q_GROUP + par
            sk = jnp.where(top, sink_ref[n_top], sink_ref[n_top + 2])
            parts.append(_sink_softmax(scores, sk))
        (p_e, d_e), (p_o, d_o) = parts
        o2 = None
        for probs, vals in ((p_e, vvar[g][0]), (p_o, vvar[g][1])):
            for p, v in zip(probs, vals):
                o2 = _dot(p, v) if o2 is None else o2 + _dot(p, v)
        o2 = o2 * jnp.where(low, 1.0 / d_e, 1.0 / d_o)
        o_ref[:, c0:c0 + LANES] = o2[:nq].astype(o_ref.dtype)
        o_ref[:, c0 + LANES:c0 + 2 * LANES] = o2[nq:].astype(o_ref.dtype)


CTX_ATTN_PER_STEP = 2


def _ctx_attn_kernel(sink_ref, q_ref, k_ref, v_ref, o_ref, *, seq):
    for s in range(CTX_ATTN_PER_STEP):
        rows = slice(s * seq, (s + 1) * seq)
        kvar = [[[v] for v in head] for head in _half_lane_variants(k_ref[rows, :])]
        vvar = [[[v] for v in head] for head in _half_lane_variants(v_ref[rows, :])]
        _sink_heads(sink_ref, q_ref.at[rows, :], kvar, vvar, [None], o_ref.at[rows, :])


def _ctx_attn_call(sink, q, k, v, *, batch, seq):
    per = CTX_ATTN_PER_STEP
    assert batch % per == 0
    return pl.pallas_call(
        functools.partial(_ctx_attn_kernel, seq=seq),
        out_shape=jax.ShapeDtypeStruct((batch * seq, ATTN_WIDTH), bf16),
        grid=(batch // per,),
        in_specs=[
            pl.BlockSpec(memory_space=pltpu.SMEM),
            pl.BlockSpec((per * seq, ATTN_WIDTH), lambda b: (b, 0)),
            pl.BlockSpec((per * seq, KV_WIDTH), lambda b: (b, 0)),
            pl.BlockSpec((per * seq, KV_WIDTH), lambda b: (b, 0)),
        ],
        out_specs=pl.BlockSpec((per * seq, ATTN_WIDTH), lambda b: (b, 0)),
        compiler_params=_cparams(("parallel",), 32),
        name="ctx_attn",
    )(sink, q, k, v)


ATTN_PREP_ROWS = 256


def _lat_attn_kernel(sink_ref, q_ref, k_ref, v_ref, kx_ref, vx_ref, o_ref, kvar_ref, vvar_ref,
                     *, n_blocks, past):
    n = pl.program_id(1)
    seq = n_blocks * BLOCK
    ctx0 = seq + 2 * BLOCK

    @pl.when(n == 0)
    def _():
        zeros = jnp.zeros((BLOCK, KV_WIDTH), bf16)
        for src, ctx, dst in ((k_ref, kx_ref, kvar_ref), (v_ref, vx_ref, vvar_ref)):
            for i in range(2 * KV_HEADS):
                dst[i, 0:BLOCK, :] = zeros
                dst[i, BLOCK + seq:ctx0, :] = zeros
            for r0 in range(0, seq + past, ATTN_PREP_ROWS):
                rows = min(ATTN_PREP_ROWS, seq + past - r0)
                x = src[r0:r0 + rows, :] if r0 < seq else ctx[r0 - seq:r0 - seq + rows, :]
                d0 = BLOCK + r0 if r0 < seq else ctx0 + r0 - seq
                variants = _half_lane_variants(x)
                for g in range(KV_HEADS):
                    for par in range(2):
                        dst[2 * g + par, d0:d0 + rows, :] = variants[g][par]

    assert WINDOW == BLOCK
    band0 = pl.multiple_of(n * BLOCK, BLOCK)

    def pieces(ref):
        return [[[ref[2 * g + par, pl.ds(band0, 3 * BLOCK), :], ref[2 * g + par, ctx0:ctx0 + past, :]]
                 for par in range(2)] for g in range(KV_HEADS)]

    a_idx = lax.broadcasted_iota(jnp.int32, (2 * BLOCK, BLOCK), 0) % BLOCK
    c_idx = lax.broadcasted_iota(jnp.int32, (2 * BLOCK, BLOCK), 1)
    bias_p = jnp.where((c_idx >= a_idx) & (n > 0), 0.0, NEG).astype(f32)
    bias_n = jnp.where((c_idx <= a_idx) & (n < n_blocks - 1), 0.0, NEG).astype(f32)

    def band_bias(s):
        return jnp.concatenate([s[:, :BLOCK] + bias_p, s[:, BLOCK:2 * BLOCK],
                                s[:, 2 * BLOCK:] + bias_n], axis=1)

    _sink_heads(sink_ref, q_ref, pieces(kvar_ref), pieces(vvar_ref), [band_bias, None], o_ref)


def _lat_attn_call(sink, q, k, v, cache_k4, cache_v4, *, layer, batch, seq):
    nb = seq // BLOCK
    past = cache_k4.shape[2]
    assert seq % ATTN_PREP_ROWS == 0
    whole = pl.BlockSpec((seq, KV_WIDTH), lambda b, n: (b, 0))
    ctx = pl.BlockSpec((None, None, past, KV_WIDTH), lambda b, n: (b, layer, 0, 0))
    return pl.pallas_call(
        functools.partial(_lat_attn_kernel, n_blocks=nb, past=past),
        out_shape=jax.ShapeDtypeStruct((batch * seq, ATTN_WIDTH), bf16),
        grid=(batch, nb),
        in_specs=[
            pl.BlockSpec(memory_space=pltpu.SMEM),
            pl.BlockSpec((BLOCK, ATTN_WIDTH), lambda b, n: (b * nb + n, 0)),
            whole, whole, ctx, ctx,
        ],
        out_specs=pl.BlockSpec((BLOCK, ATTN_WIDTH), lambda b, n: (b * nb + n, 0)),
        scratch_shapes=[pltpu.VMEM((2 * KV_HEADS, seq + 2 * BLOCK + past, KV_WIDTH), bf16)] * 2,
        compiler_params=_cparams(("parallel", "arbitrary"), 32),
        name="lat_attn",
    )(sink, q, k, v, cache_k4, cache_v4)


def _softplus(x):
    return jnp.maximum(x, 0.0) + jnp.log1p(jnp.exp(-jnp.abs(x)))


def _gelu_tanh(x):
    return 0.5 * x * (1.0 + jnp.tanh(math.sqrt(2.0 / math.pi) * (x + 0.044715 * (x * x * x))))


SCAN_SKEW = SUBLANES // 2


def _scan_layout(seq):
    seg = seq // LRU_SEGMENTS
    assert seg % SUBLANES == 0 and LRU_SEGMENTS % 2 == 0
    pitch = seg + SCAN_SKEW
    pair = 2 * seg
    return pitch, pair, (LRU_SEGMENTS // 2) * (pair + SUBLANES)


def _scan_row(t, pair):
    return t + SUBLANES * (t // pair)


def _lru_kernel(xr_ref, yr_ref, cw_ref, cb_ref, wd_ref, bias_ref, lam_ref, h0_ref,
                rec_ref, st_ref, xs_ref, af_ref, bf_ref, ab_ref, bb_ref, *, seq):
    seg = seq // LRU_SEGMENTS
    chunk = min(seq, 256)
    hw = LRU_HALF
    n_lt = hw // LANES
    pitch, pair, _ = _scan_layout(seq)
    piece = min(pair, chunk)
    assert pair % piece == 0

    for m in range(LRU_SEGMENTS // 2):
        g0 = m * (pair + SUBLANES) + pair
        for t in range(n_lt):
            for a_ref, b_ref in ((af_ref, bf_ref), (ab_ref, bb_ref)):
                a_ref[t, g0:g0 + SUBLANES, :] = jnp.ones((SUBLANES, LANES), f32)
                b_ref[t, g0:g0 + SUBLANES, :] = jnp.zeros((SUBLANES, LANES), f32)

    zeros_pad = jnp.zeros((PAD_ROWS, hw), f32)
    xs_ref[0:PAD_ROWS, :] = zeros_pad
    xs_ref[PAD_ROWS + seq:2 * PAD_ROWS + seq, :] = zeros_pad
    for c in range(seq // chunk):
        xs_ref[PAD_ROWS + c * chunk:PAD_ROWS + (c + 1) * chunk, :] = xr_ref[c * chunk:(c + 1) * chunk, :]

    neg_c_sp = [-LRU_C * _softplus(-lam_ref[d:d + 1, :]) for d in range(2)]
    a_refs = (af_ref, ab_ref)
    b_refs = (bf_ref, bb_ref)

    for c in range(seq // chunk):
        r0 = c * chunk
        xc = cb_ref[...]
        for j in range(CONV_W):
            s0 = PAD_ROWS + r0 + j - CONV_LEFT
            xc = xc + xs_ref[s0:s0 + chunk, :] * cw_ref[j:j + 1, :]
        y = _dot(xc.astype(bf16), wd_ref[...]) + bias_ref[...]
        for d in range(2):
            r = _sigmoid(y[:, (2 * d) * hw:(2 * d + 1) * hw])
            gi = _sigmoid(y[:, (2 * d + 1) * hw:(2 * d + 2) * hw])
            log_a = neg_c_sp[d] * r
            a = jnp.exp(log_a)
            z = jnp.tanh(log_a) * (-1.0 - a * a)
            b = jnp.where(z > 0.0, z * lax.rsqrt(z), 0.0) * (gi * xc)
            for p0 in range(0, chunk, piece):
                dst = _scan_row(r0 + p0, pair)
                for t in range(n_lt):
                    lanes = slice(t * LANES, (t + 1) * LANES)
                    a_refs[d][t, dst:dst + piece, :] = a[p0:p0 + piece, lanes]
                    b_refs[d][t, dst:dst + piece, :] = b[p0:p0 + piece, lanes]

    def step(j, carry):
        jf = pl.ds(j, LRU_SEGMENTS, stride=pitch)
        jb = pl.ds(pitch - 1 - j, LRU_SEGMENTS, stride=pitch)
        out = []
        for t in range(n_lt):
            hf, pf, hb, pb = carry[4 * t:4 * t + 4]
            a = af_ref[t, jf, :]
            hf = a * hf + bf_ref[t, jf, :]
            pf = a * pf
            bf_ref[t, jf, :] = hf
            af_ref[t, jf, :] = pf
            a = ab_ref[t, jb, :]
            hb = a * hb + bb_ref[t, jb, :]
            pb = a * pb
            bb_ref[t, jb, :] = hb
            ab_ref[t, jb, :] = pb
            out += [hf, pf, hb, pb]
        return tuple(out)

    z = jnp.zeros((LRU_SEGMENTS, LANES), f32)
    o = jnp.ones((LRU_SEGMENTS, LANES), f32)
    fin = lax.fori_loop(0, pitch, step, (z, o, z, o) * n_lt, unroll=2)

    sub = min(pair, 256)
    off = lax.broadcasted_iota(jnp.int32, (sub, LANES), 0)
    for t in range(n_lt):
        hf, pf, hb, pb = fin[4 * t:4 * t + 4]
        lanes = slice(t * LANES, (t + 1) * LANES)
        cf = h0_ref[0:1, lanes]
        carry_f = []
        for s in range(LRU_SEGMENTS):
            carry_f.append(cf)
            cf = hf[s:s + 1, :] + pf[s:s + 1, :] * cf
        cb = h0_ref[1:2, lanes]
        carry_b = [None] * LRU_SEGMENTS
        for s in reversed(range(LRU_SEGMENTS)):
            carry_b[s] = cb
            cb = hb[s:s + 1, :] + pb[s:s + 1, :] * cb
        st_ref[0:1, lanes] = cf
        st_ref[1:2, lanes] = cb
        for m in range(LRU_SEGMENTS // 2):
            for u in range(pair // sub):
                t0 = m * pair + u * sub
                rows = slice(t0, t0 + sub)
                src = slice(_scan_row(t0, pair), _scan_row(t0, pair) + sub)
                lo, hi = u * sub, (u + 1) * sub
                if hi <= pitch:
                    cfs, cbs = carry_f[2 * m], carry_b[2 * m]
                elif lo >= pitch:
                    cfs, cbs = carry_f[2 * m + 1], carry_b[2 * m + 1]
                else:
                    first = off < (pitch - lo)
                    cfs = jnp.where(first, carry_f[2 * m], carry_f[2 * m + 1])
                    cbs = jnp.where(first, carry_b[2 * m], carry_b[2 * m + 1])
                h = ((bf_ref[t, src, :] + af_ref[t, src, :] * cfs)
                     + (bb_ref[t, src, :] + ab_ref[t, src, :] * cbs))
                rec_ref[rows, lanes] = (h * _gelu_tanh(yr_ref[rows, lanes])).astype(rec_ref.dtype)


def _lru_call(xr, yr, conv_w, conv_b, wd, bias, lam, h0, *, layer, batch, seq):
    hw = LRU_HALF
    return pl.pallas_call(
        functools.partial(_lru_kernel, seq=seq),
        out_shape=[jax.ShapeDtypeStruct((batch * seq, LRU_WIDTH), bf16),
                   jax.ShapeDtypeStruct((batch, 2, LRU_WIDTH), f32)],
        grid=(batch, 2),
        in_specs=[
            pl.BlockSpec((seq, hw), lambda b, c: (b, c)),
            pl.BlockSpec((seq, hw), lambda b, c: (b, c)),
            pl.BlockSpec((None, CONV_W, hw), lambda b, c: (layer, 0, c)),
            pl.BlockSpec((None, 1, hw), lambda b, c: (layer, 0, c)),
            pl.BlockSpec((None, None, hw, 4 * hw), lambda b, c: (layer, c, 0, 0)),
            pl.BlockSpec((None, None, 1, 4 * hw), lambda b, c: (layer, c, 0, 0)),
            pl.BlockSpec((None, 2, hw), lambda b, c: (layer, 0, c)),
            pl.BlockSpec((None, 2, hw), lambda b, c: (b, 0, c)),
        ],
        out_specs=[pl.BlockSpec((seq, hw), lambda b, c: (b, c)),
                   pl.BlockSpec((None, 2, hw), lambda b, c: (b, 0, c))],
        scratch_shapes=([pltpu.VMEM((seq + 2 * PAD_ROWS, hw), f32)]
                        + [pltpu.VMEM((hw // LANES, _scan_layout(seq)[2], LANES), f32)] * 4),
        compiler_params=_cparams(("parallel", "parallel"), 40),
        name=f"lru_{seq}",
    )(xr, yr, conv_w, conv_b, wd, bias, lam, h0)


FOURIER_REV = 128
FOURIER_PAD = 16
FOURIER_STEP_ROWS = 2048


def _fourier_kernel(x_ref, csc_ref, ch_ref, sh_ref, rev_ref, o_ref, u_ref, *, seq, per_step, scale):
    half = seq // 2
    chunk = min(seq, 512)
    w = FOURIER_WIDTH
    for s in range(per_step):
        for c in range(seq // chunk):
            rows = slice(s * seq + c * chunk, s * seq + (c + 1) * chunk)
            for g in range(FOURIER_GROUPS):
                cols = slice(g * FOURIER_GW, (g + 1) * FOURIER_GW)
                dst = slice(s * w + g * FOURIER_GW, s * w + (g + 1) * FOURIER_GW)
                u = _dot(x_ref[rows, cols], csc_ref[...])
                u_ref[c * chunk:(c + 1) * chunk, dst] = u[:, :FOURIER_GW].astype(bf16)
                u_ref[seq + c * chunk:seq + (c + 1) * chunk, dst] = u[:, FOURIER_GW:].astype(bf16)
    a = _dot(ch_ref[...], u_ref[0:seq, :])
    b = _dot(sh_ref[...], u_ref[seq:2 * seq, :])
    top = ((a[0:half] - b[0:half]) * scale).astype(o_ref.dtype)
    z = ((a[1:half + 1] + b[1:half + 1]) * scale).astype(bf16)
    nb = half // FOURIER_REV
    for s in range(per_step):
        cols = slice(s * w, (s + 1) * w)
        o_ref[s * seq:s * seq + half, :] = top[:, cols]
        for k in range(nb):
            blk = z[(nb - 1 - k) * FOURIER_REV:(nb - k) * FOURIER_REV, cols]
            r0 = s * seq + half + k * FOURIER_REV
            o_ref[r0:r0 + FOURIER_REV, :] = _dot(rev_ref[...], blk).astype(o_ref.dtype)


def _dft_tables(n):
    k = np.arange(n, dtype=np.int64)
    ang = (2.0 * np.pi / n) * ((k[:, None] * k[None, :]) % n).astype(np.float64)
    return np.cos(ang), np.sin(ang)


def _fourier_call(xf, csc, ch, sh, rev, *, batch, seq):
    scale = 1.0 / math.sqrt(seq * FOURIER_GW)
    rows = seq // 2 + FOURIER_PAD
    per_step = max(1, min(batch, FOURIER_STEP_ROWS // seq))
    assert batch % per_step == 0

    def const(shape):
        return pl.BlockSpec(shape, lambda b: (0, 0), pipeline_mode=pl.Buffered(1))

    return pl.pallas_call(
        functools.partial(_fourier_kernel, seq=seq, per_step=per_step, scale=scale),
        out_shape=jax.ShapeDtypeStruct((batch * seq, FOURIER_WIDTH), bf16),
        grid=(batch // per_step,),
        in_specs=[
            pl.BlockSpec((per_step * seq, FOURIER_WIDTH), lambda b: (b, 0)),
            const((FOURIER_GW, 2 * FOURIER_GW)),
            const((rows, seq)),
            const((rows, seq)),
            const((FOURIER_REV, FOURIER_REV)),
        ],
        out_specs=pl.BlockSpec((per_step * seq, FOURIER_WIDTH), lambda b: (b, 0)),
        scratch_shapes=[pltpu.VMEM((2 * seq, per_step * FOURIER_WIDTH), bf16)],
        compiler_params=_cparams(("parallel",), 40),
        name=f"fourier_{seq}",
    )(xf, csc, ch, sh, rev)


MERGE_CAST_COLS = 512
MERGE_ROW_GROUPS = 2


def _merge_kernel(x_ref, sh_ref, sc_ref, gt_ref, nw_ref, at_ref, rc_ref, fr_ref,
                  wbg32_ref, bbg_ref, wao32_ref, wlo32_ref, wfo32_ref, wo32_ref, o_ref,
                  wbg_ref, wao_ref, wlo_ref, wfo_ref, wo_ref):
    @pl.when(pl.program_id(0) == 0)
    def _():
        for src, dst in ((wbg32_ref, wbg_ref), (wao32_ref, wao_ref), (wlo32_ref, wlo_ref),
                         (wfo32_ref, wfo_ref), (wo32_ref, wo_ref)):
            for c0 in range(0, src.shape[1], MERGE_CAST_COLS):
                dst[:, c0:c0 + MERGE_CAST_COLS] = src[:, c0:c0 + MERGE_CAST_COLS].astype(bf16)

    tm = x_ref.shape[0]
    for r in range(MERGE_ROW_GROUPS):
        rows = slice(r * (tm // MERGE_ROW_GROUPS), (r + 1) * (tm // MERGE_ROW_GROUPS))
        x = x_ref[rows, :]
        h = _modnorm(x, nw_ref[...], sh_ref[...], sc_ref[...]).astype(bf16)
        merged = None
        for idx, (br_ref, w_ref) in enumerate(((at_ref, wao_ref), (rc_ref, wlo_ref), (fr_ref, wfo_ref))):
            cols = slice(idx * D_MODEL, (idx + 1) * D_MODEL)
            g = _sigmoid(_dot(h, wbg_ref[:, cols]) + bbg_ref[:, cols])
            y = g * _dot(br_ref[rows, :], w_ref[...])
            merged = y if merged is None else merged + y
        out = _dot(merged.astype(bf16), wo_ref[...])
        o_ref[rows, :] = x + gt_ref[...] * out


def _merge_call(x, mod5, norm_w4, attn, rec, four, wbg, bbg, wao, wlo, wfo, wo, *, layer, row_fn, tm, name):
    n_tok = x.shape[0]

    def const(shape):
        nd = len(shape)
        return pl.BlockSpec((None,) + shape, lambda i: (layer,) + (0,) * nd,
                            pipeline_mode=pl.Buffered(1))

    def branch():
        return pl.BlockSpec((tm, ATTN_WIDTH), lambda i: (i, 0))

    return pl.pallas_call(
        _merge_kernel,
        out_shape=jax.ShapeDtypeStruct((n_tok, D_MODEL), f32),
        grid=(n_tok // tm,),
        in_specs=[
            pl.BlockSpec((tm, D_MODEL), lambda i: (i, 0)),
            _mod_spec(layer, 3, row_fn),
            _mod_spec(layer, 4, row_fn),
            _mod_spec(layer, 5, row_fn),
            _vec_spec(layer, 1),
            branch(), branch(), branch(),
            const((D_MODEL, N_BRANCH * D_MODEL)),
            const((1, N_BRANCH * D_MODEL)),
            const((ATTN_WIDTH, D_MODEL)),
            const((LRU_WIDTH, D_MODEL)),
            const((FOURIER_WIDTH, D_MODEL)),
            const((D_MODEL, D_MODEL)),
        ],
        out_specs=pl.BlockSpec((tm, D_MODEL), lambda i: (i, 0)),
        scratch_shapes=[pltpu.VMEM((D_MODEL, N_BRANCH * D_MODEL), bf16),
                        pltpu.VMEM((ATTN_WIDTH, D_MODEL), bf16),
                        pltpu.VMEM((LRU_WIDTH, D_MODEL), bf16),
                        pltpu.VMEM((FOURIER_WIDTH, D_MODEL), bf16),
                        pltpu.VMEM((D_MODEL, D_MODEL), bf16)],
        compiler_params=_cparams(("arbitrary",), 48),
        name=name,
    )(x, mod5, mod5, mod5, norm_w4, attn, rec, four, wbg, bbg, wao, wlo, wfo, wo)


def _rope_lane_tables(n_tokens):
    rows = n_tokens // GRID_W
    row = jnp.repeat(jnp.arange(rows), GRID_W).astype(f32)
    col = jnp.tile(jnp.arange(GRID_W), rows).astype(f32)
    inv = ROPE_BASE ** (-jnp.arange(0, AXIS_ROT, 2, dtype=f32) / AXIS_ROT)
    ang = jnp.stack([row[:, None] * inv, col[:, None] * inv], axis=1)
    cos, sin = jnp.cos(ang), jnp.sin(ang)
    cos_h = jnp.stack([cos, cos], axis=2).reshape(n_tokens, HEAD_DIM)
    sin_h = jnp.stack([-sin, sin], axis=2).reshape(n_tokens, HEAD_DIM)
    reps = LANES // HEAD_DIM
    return jnp.tile(cos_h, (1, reps)), jnp.tile(sin_h, (1, reps))


def _lru_dense_weights(lru_wa, lru_wi, lru_ba, lru_bi):
    per = LRU_BLOCKS // 2
    hw = LRU_HALF
    eye = jnp.eye(per, dtype=bf16)

    def dense(w, c):
        wh = w[:, :, c * per:(c + 1) * per].astype(bf16)
        d = wh[:, :, :, :, None, :] * eye[None, None, :, None, :, None]
        return d.reshape(DEPTH, 2, hw, hw)

    halves, biases = [], []
    for c in range(2):
        sl = slice(c * hw, (c + 1) * hw)
        da, di = dense(lru_wa, c), dense(lru_wi, c)
        halves.append(jnp.concatenate([da[:, 0], di[:, 0], da[:, 1], di[:, 1]], axis=-1))
        biases.append(jnp.concatenate(
            [lru_ba[:, 0, sl], lru_bi[:, 0, sl], lru_ba[:, 1, sl], lru_bi[:, 1, sl]], axis=-1))
    wd = jnp.stack(halves, axis=1)
    bias = jnp.stack(biases, axis=1)[:, :, None, :]
    return wd, bias


def kernel(x_prompt, x_sample, c, cache_k, cache_v, state_lru, c_ctx, w_ada, b_ada, norm_w, final_norm_w,
           ffn1_wg, ffn1_wu, ffn1_wd, ffn2_wg, ffn2_wu, ffn2_wd, w_in, w_branch_gate, b_branch_gate,
           attn_sink, w_attn_out, conv_w, conv_b, lru_wa, lru_ba, lru_wi, lru_bi, lru_lambda,
           w_lru_out, w_fourier_out, w_o):
    batch, seq, _ = x_prompt.shape
    dec_batch, dec_seq, _ = x_sample.shape
    past = cache_k.shape[2]
    assert 1 + dec_batch <= COND_ROWS

    cond = jnp.concatenate([c_ctx[None, :], c, jnp.zeros((COND_ROWS - 1 - dec_batch, D_MODEL), f32)], axis=0)
    mod = _ada_call(cond, w_ada, b_ada)
    mod5 = mod.reshape(DEPTH, COND_ROWS, N_SUB * 3, D_MODEL).transpose(0, 2, 1, 3)[:, :, :, None, :]
    norm_w4 = norm_w[:, :, None, :]
    final_w = final_norm_w[None, :]

    bbg = b_branch_gate[:, None, :]
    merge_w = (w_branch_gate, bbg, w_attn_out, w_lru_out, w_fourier_out, w_o)
    lru_wd, lru_bias = _lru_dense_weights(lru_wa, lru_wi, lru_ba, lru_bi)
    conv_b3 = conv_b[:, None, :]
    rope_tabs = _rope_lane_tables(dec_seq)
    cc, sc_ = _dft_tables(FOURIER_GW)
    csc = jnp.asarray(np.concatenate([cc, sc_], axis=1), f32).astype(bf16)
    dft = {}
    for n in (seq, dec_seq):
        cl, sl = _dft_tables(n)
        rows = n // 2 + FOURIER_PAD
        dft[n] = (jnp.asarray(cl[:rows], f32).astype(bf16), jnp.asarray(sl[:rows], f32).astype(bf16))
    rev = jnp.asarray(np.eye(FOURIER_REV)[::-1], f32).astype(bf16)
    cache_k4 = cache_k.reshape(dec_batch, DEPTH, past, KV_WIDTH)
    cache_v4 = cache_v.reshape(dec_batch, DEPTH, past, KV_WIDTH)
    h0_ctx = jnp.zeros((batch, 2, LRU_WIDTH), f32)

    tm_p, tm_s = 2048, 2048
    row_p = lambda i: 0
    row_s = lambda i: 1 + (i * tm_s) // dec_seq
    tmi = 1024
    row_pi = lambda i: 0
    row_si = lambda i: 1 + (i * tmi) // dec_seq
    tmm = 512
    row_pm = lambda i: 0
    row_sm = lambda i: 1 + (i * tmm) // dec_seq

    xp = x_prompt.reshape(batch * seq, D_MODEL)
    xs = x_sample.reshape(dec_batch * dec_seq, D_MODEL)
    ks, vs, ss = [], [], []
    for l in range(DEPTH):
        last = l == DEPTH - 1
        sink = attn_sink[l]
        xp = _ffn_call(xp, mod5, norm_w4, ffn1_wg, ffn1_wu, ffn1_wd, None,
                       layer=l, sub=0, row_fn=row_p, tm=tm_p, name="ffn1_ctx")
        xs = _ffn_call(xs, mod5, norm_w4, ffn1_wg, ffn1_wu, ffn1_wd, None,
                       layer=l, sub=0, row_fn=row_s, tm=tm_s, name="ffn1_lat")
        q, k, v, xr, yr, xf = _inproj_call(xp, mod5, norm_w4, w_in, None,
                                           layer=l, row_fn=row_pi, tm=tmi, seq=seq, name="inproj_ctx")
        ks.append(k.reshape(batch, seq, KV_HEADS, HEAD_DIM))
        vs.append(v.reshape(batch, seq, KV_HEADS, HEAD_DIM))
        attn = _ctx_attn_call(sink, q, k, v, batch=batch, seq=seq)
        rec, st = _lru_call(xr, yr, conv_w, conv_b3, lru_wd, lru_bias, lru_lambda, h0_ctx,
                            layer=l, batch=batch, seq=seq)
        ss.append(st)
        four = _fourier_call(xf, csc, *dft[seq], rev, batch=batch, seq=seq)
        xp = _merge_call(xp, mod5, norm_w4, attn, rec, four, *merge_w,
                         layer=l, row_fn=row_pm, tm=tmm, name="merge_ctx")
        q, k, v, xr, yr, xf = _inproj_call(xs, mod5, norm_w4, w_in, rope_tabs,
                                           layer=l, row_fn=row_si, tm=tmi, seq=dec_seq, name="inproj_lat")
        attn = _lat_attn_call(sink, q, k, v, cache_k4, cache_v4, layer=l, batch=dec_batch, seq=dec_seq)
        rec, _ = _lru_call(xr, yr, conv_w, conv_b3, lru_wd, lru_bias, lru_lambda, state_lru[:, l],
                           layer=l, batch=dec_batch, seq=dec_seq)
        four = _fourier_call(xf, csc, *dft[dec_seq], rev, batch=dec_batch, seq=dec_seq)
        xs = _merge_call(xs, mod5, norm_w4, attn, rec, four, *merge_w,
                         layer=l, row_fn=row_sm, tm=tmm, name="merge_lat")
        xp = _ffn_call(xp, mod5, norm_w4, ffn2_wg, ffn2_wu, ffn2_wd, final_w if last else None,
                       layer=l, sub=2, row_fn=row_p, tm=tm_p, name="ffn2_ctx")
        xs = _ffn_call(xs, mod5, norm_w4, ffn2_wg, ffn2_wu, ffn2_wd, final_w if last else None,
                       layer=l, sub=2, row_fn=row_s, tm=tm_s, name="ffn2_lat")

    y_prompt = xp.reshape(batch, seq, D_MODEL)
    y_sample = xs.reshape(dec_batch, dec_seq, D_MODEL)
    return (y_prompt, y_sample, jnp.stack(ks, axis=1), jnp.stack(vs, axis=1), jnp.stack(ss, axis=1))
```

```python
import functools
import math

import numpy as np
import jax
import jax.numpy as jnp
from jax import lax
from jax.experimental import pallas as pl
from jax.experimental.pallas import tpu as pltpu

f32 = jnp.float32
bf16 = jnp.bfloat16

D_MODEL = 1024
DEPTH = 2
GRID_W = 64
N_HEADS = 8
KV_HEADS = 2
HEAD_DIM = 64
Q_GROUP = N_HEADS // KV_HEADS
ATTN_WIDTH = N_HEADS * HEAD_DIM
KV_WIDTH = KV_HEADS * HEAD_DIM
WINDOW = 128
BLOCK = 128
AXIS_ROT = HEAD_DIM // 2
ROPE_BASE = 10000.0
LRU_WIDTH = 512
LRU_BLOCKS = 8
LRU_BW = LRU_WIDTH // LRU_BLOCKS
LRU_C = 8.0
CONV_W = 4
CONV_LEFT = 2
FOURIER_WIDTH = 512
FOURIER_GROUPS = 4
FOURIER_GW = FOURIER_WIDTH // FOURIER_GROUPS
D_FF = 2816
N_BRANCH = 3
N_SUB = 3
EPS = 1e-6
NEG = -1e30
IN_WIDTH = ATTN_WIDTH + 2 * KV_WIDTH + 2 * LRU_WIDTH + FOURIER_WIDTH

V7X_VMEM_BYTES = 64 * 1024 * 1024
SUBLANES = 8
LANES = 128
MXU_N = 256
assert 2 * KV_WIDTH == MXU_N
COND_ROWS = 8
LRU_HALF = LRU_WIDTH // 2
LRU_SEGMENTS = SUBLANES
PAD_ROWS = SUBLANES

FFN_TOKEN_TILE = 2048
INPROJ_TOKEN_TILE = 1024
MERGE_TOKEN_TILE = 512

_MIB = 1024 * 1024


def _cparams(sem):
    return pltpu.CompilerParams(dimension_semantics=sem, vmem_limit_bytes=V7X_VMEM_BYTES)


def _dot(a, b):
    return jnp.dot(a, b, preferred_element_type=f32)


def _sigmoid(x):
    return jax.nn.sigmoid(x)


def _modnorm(x, nw, shift, scale):
    y = x * lax.rsqrt(jnp.mean(x * x, axis=-1, keepdims=True) + EPS)
    return y * (nw * (1.0 + scale)) + shift


def _ada_kernel(cond_ref, w_ref, b_ref, o_ref):
    c = cond_ref[...]
    s = (c * _sigmoid(c)).astype(bf16)
    o_ref[...] = _dot(s, w_ref[...].astype(bf16)) + b_ref[...]


def _ada_call(cond, w_ada, b_ada):
    n_out = w_ada.shape[-1]
    tn = 1024
    return pl.pallas_call(
        _ada_kernel,
        out_shape=jax.ShapeDtypeStruct((DEPTH, COND_ROWS, n_out), f32),
        grid=(DEPTH, n_out // tn),
        in_specs=[
            pl.BlockSpec((COND_ROWS, D_MODEL), lambda l, j: (0, 0)),
            pl.BlockSpec((None, D_MODEL, tn), lambda l, j: (l, 0, j)),
            pl.BlockSpec((None, 1, tn), lambda l, j: (l, 0, j)),
        ],
        out_specs=pl.BlockSpec((None, COND_ROWS, tn), lambda l, j: (l, 0, j)),
        compiler_params=_cparams(("parallel", "parallel")),
        name="adaln",
    )(cond, w_ada, b_ada.reshape(DEPTH, 1, n_out))


def _mod_spec(layer, slot, row_fn):
    return pl.BlockSpec((None, None, None, 1, D_MODEL),
                        lambda i, *_: (layer, slot, row_fn(i), 0, 0))


def _vec_spec(layer, sub):
    return pl.BlockSpec((None, None, 1, D_MODEL), lambda i, *_: (layer, sub, 0, 0))


FFN_ROW_GROUPS = 4


def _ffn_kernel(*refs, n_ff, final):
    if final:
        (x_ref, sh_ref, sc_ref, gt_ref, nw_ref, wg_ref, wu_ref, wd_ref, fw_ref,
         o_ref, h_ref) = refs
    else:
        (x_ref, sh_ref, sc_ref, gt_ref, nw_ref, wg_ref, wu_ref, wd_ref,
         o_ref, h_ref) = refs
    j = pl.program_id(1)

    tm = x_ref.shape[0]
    groups = [slice(r * (tm // FFN_ROW_GROUPS), (r + 1) * (tm // FFN_ROW_GROUPS))
              for r in range(FFN_ROW_GROUPS)]

    def down(h, w):
        wg, wu, wd = w
        g = _dot(h, wg)
        u = _dot(h, wu)
        a = ((g * _sigmoid(g)) * u).astype(bf16)
        return _dot(a, wd)

    def weights():
        return (wg_ref[...].astype(bf16), wu_ref[...].astype(bf16), wd_ref[...].astype(bf16))

    @pl.when(j == 0)
    def _():
        w = weights()
        for rows in groups:
            h = _modnorm(x_ref[rows, :], nw_ref[...], sh_ref[...], sc_ref[...]).astype(bf16)
            h_ref[rows, :] = h
            o_ref[rows, :] = down(h, w)

    @pl.when((j > 0) & (j < n_ff - 1))
    def _():
        w = weights()
        for rows in groups:
            o_ref[rows, :] += down(h_ref[rows, :], w)

    @pl.when(j == n_ff - 1)
    def _():
        w = weights()
        for rows in groups:
            y = x_ref[rows, :] + (0.5 * gt_ref[...]) * (o_ref[rows, :] + down(h_ref[rows, :], w))
            if final:
                y = y * lax.rsqrt(jnp.mean(y * y, axis=-1, keepdims=True) + EPS) * fw_ref[...]
            o_ref[rows, :] = y


def _ffn_call(x, mod5, norm_w4, wg, wu, wd, final_w, *, layer, sub, row_fn, tm, name):
    n_tok = x.shape[0]
    tf = 256
    n_ff = D_FF // tf
    final = final_w is not None
    in_specs = [
        pl.BlockSpec((tm, D_MODEL), lambda i, j: (i, 0)),
        _mod_spec(layer, sub * 3 + 0, row_fn),
        _mod_spec(layer, sub * 3 + 1, row_fn),
        _mod_spec(layer, sub * 3 + 2, row_fn),
        _vec_spec(layer, sub),
        pl.BlockSpec((None, D_MODEL, tf), lambda i, j: (layer, 0, j)),
        pl.BlockSpec((None, D_MODEL, tf), lambda i, j: (layer, 0, j)),
        pl.BlockSpec((None, tf, D_MODEL), lambda i, j: (layer, j, 0)),
    ]
    args = [x, mod5, mod5, mod5, norm_w4, wg, wu, wd]
    if final:
        in_specs.append(pl.BlockSpec((1, D_MODEL), lambda i, j: (0, 0)))
        args.append(final_w)
    return pl.pallas_call(
        functools.partial(_ffn_kernel, n_ff=n_ff, final=final),
        out_shape=jax.ShapeDtypeStruct((n_tok, D_MODEL), f32),
        grid=(n_tok // tm, n_ff),
        in_specs=in_specs,
        out_specs=pl.BlockSpec((tm, D_MODEL), lambda i, j: (i, 0)),
        scratch_shapes=[pltpu.VMEM((tm, D_MODEL), bf16)],
        compiler_params=_cparams(("parallel", "arbitrary")),
        name=name,
    )(*args)


def _swap16(x):
    n = x.shape[-1]
    lane = lax.broadcasted_iota(jnp.int32, x.shape, x.ndim - 1)
    first = (lane % AXIS_ROT) < (AXIS_ROT // 2)
    return jnp.where(first, pltpu.roll(x, n - AXIS_ROT // 2, x.ndim - 1),
                     pltpu.roll(x, AXIS_ROT // 2, x.ndim - 1))


INPROJ_ROW_GROUPS = 2


def _inproj_kernel(*refs, rope):
    if rope:
        (x_ref, sh_ref, sc_ref, nw_ref, w32_ref, cos_ref, sin_ref,
         q_ref, k_ref, v_ref, xr_ref, yr_ref, xf_ref, w_ref) = refs
    else:
        (x_ref, sh_ref, sc_ref, nw_ref, w32_ref,
         q_ref, k_ref, v_ref, xr_ref, yr_ref, xf_ref, w_ref) = refs

    @pl.when(pl.program_id(0) == 0)
    def _():
        for c0 in range(0, IN_WIDTH, MXU_N):
            w_ref[:, c0:c0 + MXU_N] = w32_ref[:, c0:c0 + MXU_N].astype(bf16)

    tm = x_ref.shape[0]
    for r in range(INPROJ_ROW_GROUPS):
        rows = slice(r * (tm // INPROJ_ROW_GROUPS), (r + 1) * (tm // INPROJ_ROW_GROUPS))
        h = _modnorm(x_ref[rows, :], nw_ref[...], sh_ref[...], sc_ref[...]).astype(bf16)
        if rope:
            cos = cos_ref[rows, :]
            sin = sin_ref[rows, :]

        def rot(t):
            return t * cos + _swap16(t) * sin if rope else t

        o = ATTN_WIDTH
        kv = _dot(h, w_ref[:, o:o + 2 * KV_WIDTH])
        k_ref[rows, :] = rot(kv[:, :KV_WIDTH])
        v_ref[rows, :] = kv[:, KV_WIDTH:]
        for c in range(ATTN_WIDTH // MXU_N):
            q = _dot(h, w_ref[:, c * MXU_N:(c + 1) * MXU_N])
            for t in range(MXU_N // LANES):
                c0 = c * MXU_N + t * LANES
                q_ref[rows, c0:c0 + LANES] = rot(q[:, t * LANES:(t + 1) * LANES]).astype(bf16)
        o += 2 * KV_WIDTH
        xr_ref[rows, :] = _dot(h, w_ref[:, o:o + LRU_WIDTH])
        o += LRU_WIDTH
        yr_ref[rows, :] = _dot(h, w_ref[:, o:o + LRU_WIDTH])
        o += LRU_WIDTH
        xf_ref[rows, :] = _dot(h, w_ref[:, o:o + FOURIER_WIDTH]).astype(bf16)


def _inproj_call(x, mod5, norm_w4, w_in, rope_tabs, *, layer, row_fn, tm, seq, name):
    n_tok = x.shape[0]
    rope = rope_tabs is not None
    in_specs = [
        pl.BlockSpec((tm, D_MODEL), lambda i: (i, 0)),
        _mod_spec(layer, 3, row_fn),
        _mod_spec(layer, 4, row_fn),
        _vec_spec(layer, 1),
        pl.BlockSpec((None, D_MODEL, IN_WIDTH), lambda i: (layer, 0, 0), pipeline_mode=pl.Buffered(1)),
    ]
    args = [x, mod5, mod5, norm_w4, w_in]
    if rope:
        per_seq = seq // tm
        in_specs += [pl.BlockSpec((tm, LANES), lambda i: (i % per_seq, 0))] * 2
        args += list(rope_tabs)
    widths = (ATTN_WIDTH, KV_WIDTH, KV_WIDTH, LRU_WIDTH, LRU_WIDTH, FOURIER_WIDTH)
    dtypes = (bf16, f32, f32, f32, f32, bf16)
    return pl.pallas_call(
        functools.partial(_inproj_kernel, rope=rope),
        out_shape=[jax.ShapeDtypeStruct((n_tok, w), d) for w, d in zip(widths, dtypes)],
        grid=(n_tok // tm,),
        in_specs=in_specs,
        out_specs=[pl.BlockSpec((tm, w), lambda i: (i, 0)) for w in widths],
        scratch_shapes=[pltpu.VMEM((D_MODEL, IN_WIDTH), bf16)],
        compiler_params=_cparams(("arbitrary",)),
        name=name,
    )(*args)


def _half_lane_variants(x):
    lane = lax.broadcasted_iota(jnp.int32, x.shape, 1)
    low = lane < HEAD_DIM
    sw = pltpu.roll(x, HEAD_DIM, 1)
    zero = jnp.zeros_like(x)
    head0 = (jnp.where(low, x, zero), jnp.where(low, zero, sw))
    head1 = (jnp.where(low, sw, zero), jnp.where(low, zero, x))
    return [tuple(v.astype(bf16) for v in h) for h in (head0, head1)]


def _sink_softmax(scores, sk):
    m = sk
    for s in scores:
        m = jnp.maximum(m, jnp.max(s, axis=-1, keepdims=True))
    denom = jnp.exp(sk - m)
    probs = []
    for s in scores:
        p = jnp.exp(s - m)
        denom = denom + jnp.sum(p, axis=-1, keepdims=True)
        probs.append(p.astype(bf16))
    return probs, denom


def _sink_heads(sink_ref, q_ref, kvar, vvar, bias_fns, o_ref):
    nq = q_ref.shape[0]
    top = lax.broadcasted_iota(jnp.int32, (2 * nq, 1), 0) < nq
    low = lax.broadcasted_iota(jnp.int32, (2 * nq, LANES), 1) < HEAD_DIM
    nt = (((1,), (1,)), ((), ()))
    for g in range(KV_HEADS):
        c0 = g * Q_GROUP * HEAD_DIM
        q2 = jnp.concatenate([q_ref[:, c0:c0 + LANES], q_ref[:, c0 + LANES:c0 + 2 * LANES]], axis=0)
        q2 = q2 * jnp.asarray(HEAD_DIM ** -0.5, bf16)
        parts = []
        for par in range(2):
            scores = []
            for piece, bias_fn in zip(kvar[g][par], bias_fns):
                s = lax.dot_general(q2, piece, nt, preferred_element_type=f32)
                scores.append(s if bias_fn is None else bias_fn(s))
            n_top = g * Q_GROUP + par
            sk = jnp.where(top, sink_ref[n_top], sink_ref[n_top + 2])
            parts.append(_sink_softmax(scores, sk))
        (p_e, d_e), (p_o, d_o) = parts
        o2 = None
        for probs, vals in ((p_e, vvar[g][0]), (p_o, vvar[g][1])):
            for p, v in zip(probs, vals):
                o2 = _dot(p, v) if o2 is None else o2 + _dot(p, v)
        o2 = o2 * jnp.where(low, 1.0 / d_e, 1.0 / d_o)
        o_ref[:, c0:c0 + LANES] = o2[:nq].astype(o_ref.dtype)
        o_ref[:, c0 + LANES:c0 + 2 * LANES] = o2[nq:].astype(o_ref.dtype)


CTX_ATTN_PER_STEP = 2


def _ctx_attn_kernel(sink_ref, q_ref, k_ref, v_ref, o_ref, *, seq):
    for s in range(CTX_ATTN_PER_STEP):
        rows = slice(s * seq, (s + 1) * seq)
        kvar = [[[v] for v in head] for head in _half_lane_variants(k_ref[rows, :])]
        vvar = [[[v] for v in head] for head in _half_lane_variants(v_ref[rows, :])]
        _sink_heads(sink_ref, q_ref.at[rows, :], kvar, vvar, [None], o_ref.at[rows, :])


def _ctx_attn_call(sink, q, k, v, *, batch, seq):
    per = CTX_ATTN_PER_STEP
    assert batch % per == 0
    return pl.pallas_call(
        functools.partial(_ctx_attn_kernel, seq=seq),
        out_shape=jax.ShapeDtypeStruct((batch * seq, ATTN_WIDTH), bf16),
        grid=(batch // per,),
        in_specs=[
            pl.BlockSpec(memory_space=pltpu.SMEM),
            pl.BlockSpec((per * seq, ATTN_WIDTH), lambda b: (b, 0)),
            pl.BlockSpec((per * seq, KV_WIDTH), lambda b: (b, 0)),
            pl.BlockSpec((per * seq, KV_WIDTH), lambda b: (b, 0)),
        ],
        out_specs=pl.BlockSpec((per * seq, ATTN_WIDTH), lambda b: (b, 0)),
        compiler_params=_cparams(("parallel",)),
        name="ctx_attn",
    )(sink, q, k, v)


ATTN_PREP_ROWS = 256


LAT_ATTN_PER_STEP = 2


def _lat_attn_kernel(sink_ref, q_ref, k_ref, v_ref, kx_ref, vx_ref, o_ref, kvar_ref, vvar_ref,
                     *, n_blocks, past):
    n = pl.program_id(1)
    seq = n_blocks * BLOCK
    ctx0 = seq + 2 * BLOCK

    @pl.when(n == 0)
    def _():
        zeros = jnp.zeros((BLOCK, KV_WIDTH), bf16)
        for src, ctx, dst in ((k_ref, kx_ref, kvar_ref), (v_ref, vx_ref, vvar_ref)):
            for i in range(2 * KV_HEADS):
                dst[i, 0:BLOCK, :] = zeros
                dst[i, BLOCK + seq:ctx0, :] = zeros
            for r0 in range(0, seq + past, ATTN_PREP_ROWS):
                rows = min(ATTN_PREP_ROWS, seq + past - r0)
                x = src[r0:r0 + rows, :] if r0 < seq else ctx[r0 - seq:r0 - seq + rows, :]
                d0 = BLOCK + r0 if r0 < seq else ctx0 + r0 - seq
                variants = _half_lane_variants(x)
                for g in range(KV_HEADS):
                    for par in range(2):
                        dst[2 * g + par, d0:d0 + rows, :] = variants[g][par]

    assert WINDOW == BLOCK
    a_idx = lax.broadcasted_iota(jnp.int32, (2 * BLOCK, BLOCK), 0) % BLOCK
    c_idx = lax.broadcasted_iota(jnp.int32, (2 * BLOCK, BLOCK), 1)
    for sub in range(LAT_ATTN_PER_STEP):
        blk = n * LAT_ATTN_PER_STEP + sub
        band0 = pl.multiple_of(blk * BLOCK, BLOCK)

        def pieces(ref, band0=band0):
            return [[[ref[2 * g + par, pl.ds(band0, 3 * BLOCK), :], ref[2 * g + par, ctx0:ctx0 + past, :]]
                     for par in range(2)] for g in range(KV_HEADS)]

        bias_p = jnp.where((c_idx >= a_idx) & (blk > 0), 0.0, NEG).astype(f32)
        bias_n = jnp.where((c_idx <= a_idx) & (blk < n_blocks - 1), 0.0, NEG).astype(f32)

        def band_bias(s, bias_p=bias_p, bias_n=bias_n):
            return jnp.concatenate([s[:, :BLOCK] + bias_p, s[:, BLOCK:2 * BLOCK],
                                    s[:, 2 * BLOCK:] + bias_n], axis=1)

        rows = slice(sub * BLOCK, (sub + 1) * BLOCK)
        _sink_heads(sink_ref, q_ref.at[rows, :], pieces(kvar_ref), pieces(vvar_ref),
                    [band_bias, None], o_ref.at[rows, :])


def _lat_attn_call(sink, q, k, v, cache_k4, cache_v4, *, layer, batch, seq):
    nb = seq // BLOCK
    past = cache_k4.shape[2]
    assert seq % ATTN_PREP_ROWS == 0
    per = LAT_ATTN_PER_STEP
    assert nb % per == 0
    steps = nb // per
    whole = pl.BlockSpec((seq, KV_WIDTH), lambda b, n: (b, 0))
    ctx = pl.BlockSpec((None, None, past, KV_WIDTH), lambda b, n: (b, layer, 0, 0))
    return pl.pallas_call(
        functools.partial(_lat_attn_kernel, n_blocks=nb, past=past),
        out_shape=jax.ShapeDtypeStruct((batch * seq, ATTN_WIDTH), bf16),
        grid=(batch, steps),
        in_specs=[
            pl.BlockSpec(memory_space=pltpu.SMEM),
            pl.BlockSpec((per * BLOCK, ATTN_WIDTH), lambda b, n: (b * steps + n, 0)),
            whole, whole, ctx, ctx,
        ],
        out_specs=pl.BlockSpec((per * BLOCK, ATTN_WIDTH), lambda b, n: (b * steps + n, 0)),
        scratch_shapes=[pltpu.VMEM((2 * KV_HEADS, seq + 2 * BLOCK + past, KV_WIDTH), bf16)] * 2,
        compiler_params=_cparams(("parallel", "arbitrary")),
        name="lat_attn",
    )(sink, q, k, v, cache_k4, cache_v4)


def _softplus(x):
    return jnp.maximum(x, 0.0) + jnp.log1p(jnp.exp(-jnp.abs(x)))


def _gelu_tanh(x):
    return 0.5 * x * (1.0 + jnp.tanh(math.sqrt(2.0 / math.pi) * (x + 0.044715 * (x * x * x))))


SCAN_SKEW = SUBLANES // 2


def _scan_layout(seq):
    seg = seq // LRU_SEGMENTS
    assert seg % SUBLANES == 0 and LRU_SEGMENTS % 2 == 0
    pitch = seg + SCAN_SKEW
    pair = 2 * seg
    return pitch, pair, (LRU_SEGMENTS // 2) * (pair + SUBLANES)


def _scan_row(t, pair):
    return t + SUBLANES * (t // pair)


def _lru_kernel(xr_ref, yr_ref, cw_ref, cb_ref, wd_ref, bias_ref, lam_ref, h0_ref,
                rec_ref, st_ref, xs_ref, af_ref, bf_ref, ab_ref, bb_ref, *, seq):
    seg = seq // LRU_SEGMENTS
    chunk = min(seq, 256)
    hw = LRU_HALF
    n_lt = hw // LANES
    pitch, pair, _ = _scan_layout(seq)
    piece = min(pair, chunk)
    assert pair % piece == 0

    for m in range(LRU_SEGMENTS // 2):
        g0 = m * (pair + SUBLANES) + pair
        for t in range(n_lt):
            for a_ref, b_ref in ((af_ref, bf_ref), (ab_ref, bb_ref)):
                a_ref[t, g0:g0 + SUBLANES, :] = jnp.ones((SUBLANES, LANES), f32)
                b_ref[t, g0:g0 + SUBLANES, :] = jnp.zeros((SUBLANES, LANES), f32)

    zeros_pad = jnp.zeros((PAD_ROWS, hw), f32)
    xs_ref[0:PAD_ROWS, :] = zeros_pad
    xs_ref[PAD_ROWS + seq:2 * PAD_ROWS + seq, :] = zeros_pad
    for c in range(seq // chunk):
        xs_ref[PAD_ROWS + c * chunk:PAD_ROWS + (c + 1) * chunk, :] = xr_ref[c * chunk:(c + 1) * chunk, :]

    neg_c_sp = [-LRU_C * _softplus(-lam_ref[d:d + 1, :]) for d in range(2)]
    a_refs = (af_ref, ab_ref)
    b_refs = (bf_ref, bb_ref)

    for c in range(seq // chunk):
        r0 = c * chunk
        xc = cb_ref[...]
        for j in range(CONV_W):
            s0 = PAD_ROWS + r0 + j - CONV_LEFT
            xc = xc + xs_ref[s0:s0 + chunk, :] * cw_ref[j:j + 1, :]
        y = _dot(xc.astype(bf16), wd_ref[...]) + bias_ref[...]
        for d in range(2):
            r = _sigmoid(y[:, (2 * d) * hw:(2 * d + 1) * hw])
            gi = _sigmoid(y[:, (2 * d + 1) * hw:(2 * d + 2) * hw])
            log_a = neg_c_sp[d] * r
            a = jnp.exp(log_a)
            z = jnp.tanh(log_a) * (-1.0 - a * a)
            b = jnp.where(z > 0.0, z * lax.rsqrt(z), 0.0) * (gi * xc)
            for p0 in range(0, chunk, piece):
                dst = _scan_row(r0 + p0, pair)
                for t in range(n_lt):
                    lanes = slice(t * LANES, (t + 1) * LANES)
                    a_refs[d][t, dst:dst + piece, :] = a[p0:p0 + piece, lanes]
                    b_refs[d][t, dst:dst + piece, :] = b[p0:p0 + piece, lanes]

    def step(j, carry):
        jf = pl.ds(j, LRU_SEGMENTS, stride=pitch)
        jb = pl.ds(pitch - 1 - j, LRU_SEGMENTS, stride=pitch)
        out = []
        for t in range(n_lt):
            hf, pf, hb, pb = carry[4 * t:4 * t + 4]
            a = af_ref[t, jf, :]
            hf = a * hf + bf_ref[t, jf, :]
            pf = a * pf
            bf_ref[t, jf, :] = hf
            af_ref[t, jf, :] = pf
            a = ab_ref[t, jb, :]
            hb = a * hb + bb_ref[t, jb, :]
            pb = a * pb
            bb_ref[t, jb, :] = hb
            ab_ref[t, jb, :] = pb
            out += [hf, pf, hb, pb]
        return tuple(out)

    z = jnp.zeros((LRU_SEGMENTS, LANES), f32)
    o = jnp.ones((LRU_SEGMENTS, LANES), f32)
    fin = lax.fori_loop(0, pitch, step, (z, o, z, o) * n_lt, unroll=2)

    sub = min(pair, 256)
    off = lax.broadcasted_iota(jnp.int32, (sub, LANES), 0)
    for t in range(n_lt):
        hf, pf, hb, pb = fin[4 * t:4 * t + 4]
        lanes = slice(t * LANES, (t + 1) * LANES)
        cf = h0_ref[0:1, lanes]
        carry_f = []
        for s in range(LRU_SEGMENTS):
            carry_f.append(cf)
            cf = hf[s:s + 1, :] + pf[s:s + 1, :] * cf
        cb = h0_ref[1:2, lanes]
        carry_b = [None] * LRU_SEGMENTS
        for s in reversed(range(LRU_SEGMENTS)):
            carry_b[s] = cb
            cb = hb[s:s + 1, :] + pb[s:s + 1, :] * cb
        st_ref[0:1, lanes] = cf
        st_ref[1:2, lanes] = cb
        for m in range(LRU_SEGMENTS // 2):
            for u in range(pair // sub):
                t0 = m * pair + u * sub
                rows = slice(t0, t0 + sub)
                src = slice(_scan_row(t0, pair), _scan_row(t0, pair) + sub)
                lo, hi = u * sub, (u + 1) * sub
                if hi <= pitch:
                    cfs, cbs = carry_f[2 * m], carry_b[2 * m]
                elif lo >= pitch:
                    cfs, cbs = carry_f[2 * m + 1], carry_b[2 * m + 1]
                else:
                    first = off < (pitch - lo)
                    cfs = jnp.where(first, carry_f[2 * m], carry_f[2 * m + 1])
                    cbs = jnp.where(first, carry_b[2 * m], carry_b[2 * m + 1])
                h = ((bf_ref[t, src, :] + af_ref[t, src, :] * cfs)
                     + (bb_ref[t, src, :] + ab_ref[t, src, :] * cbs))
                rec_ref[rows, lanes] = (h * _gelu_tanh(yr_ref[rows, lanes])).astype(rec_ref.dtype)


def _lru_call(xr, yr, conv_w, conv_b, wd, bias, lam, h0, *, layer, batch, seq):
    hw = LRU_HALF
    return pl.pallas_call(
        functools.partial(_lru_kernel, seq=seq),
        out_shape=[jax.ShapeDtypeStruct((batch * seq, LRU_WIDTH), bf16),
                   jax.ShapeDtypeStruct((batch, 2, LRU_WIDTH), f32)],
        grid=(batch, 2),
        in_specs=[
            pl.BlockSpec((seq, hw), lambda b, c: (b, c)),
            pl.BlockSpec((seq, hw), lambda b, c: (b, c)),
            pl.BlockSpec((None, CONV_W, hw), lambda b, c: (layer, 0, c)),
            pl.BlockSpec((None, 1, hw), lambda b, c: (layer, 0, c)),
            pl.BlockSpec((None, None, hw, 4 * hw), lambda b, c: (layer, c, 0, 0)),
            pl.BlockSpec((None, None, 1, 4 * hw), lambda b, c: (layer, c, 0, 0)),
            pl.BlockSpec((None, 2, hw), lambda b, c: (layer, 0, c)),
            pl.BlockSpec((None, 2, hw), lambda b, c: (b, 0, c)),
        ],
        out_specs=[pl.BlockSpec((seq, hw), lambda b, c: (b, c)),
                   pl.BlockSpec((None, 2, hw), lambda b, c: (b, 0, c))],
        scratch_shapes=([pltpu.VMEM((seq + 2 * PAD_ROWS, hw), f32)]
                        + [pltpu.VMEM((hw // LANES, _scan_layout(seq)[2], LANES), f32)] * 4),
        compiler_params=_cparams(("parallel", "parallel")),
        name=f"lru_{seq}",
    )(xr, yr, conv_w, conv_b, wd, bias, lam, h0)


FOURIER_REV = 128
FOURIER_PAD = 16
FOURIER_STEP_ROWS = 2048


def _fourier_kernel(x_ref, csc_ref, ch_ref, sh_ref, rev_ref, o_ref, u_ref, *, seq, per_step, scale):
    half = seq // 2
    chunk = min(seq, 512)
    w = FOURIER_WIDTH
    for s in range(per_step):
        for c in range(seq // chunk):
            rows = slice(s * seq + c * chunk, s * seq + (c + 1) * chunk)
            for g in range(FOURIER_GROUPS):
                cols = slice(g * FOURIER_GW, (g + 1) * FOURIER_GW)
                dst = slice(s * w + g * FOURIER_GW, s * w + (g + 1) * FOURIER_GW)
                u = _dot(x_ref[rows, cols], csc_ref[...])
                u_ref[c * chunk:(c + 1) * chunk, dst] = u[:, :FOURIER_GW].astype(bf16)
                u_ref[seq + c * chunk:seq + (c + 1) * chunk, dst] = u[:, FOURIER_GW:].astype(bf16)
    a = _dot(ch_ref[...], u_ref[0:seq, :])
    b = _dot(sh_ref[...], u_ref[seq:2 * seq, :])
    top = ((a[0:half] - b[0:half]) * scale).astype(o_ref.dtype)
    z = ((a[1:half + 1] + b[1:half + 1]) * scale).astype(bf16)
    nb = half // FOURIER_REV
    for s in range(per_step):
        cols = slice(s * w, (s + 1) * w)
        o_ref[s * seq:s * seq + half, :] = top[:, cols]
        for k in range(nb):
            blk = z[(nb - 1 - k) * FOURIER_REV:(nb - k) * FOURIER_REV, cols]
            r0 = s * seq + half + k * FOURIER_REV
            o_ref[r0:r0 + FOURIER_REV, :] = _dot(rev_ref[...], blk).astype(o_ref.dtype)


def _dft_tables(n):
    k = np.arange(n, dtype=np.int64)
    ang = (2.0 * np.pi / n) * ((k[:, None] * k[None, :]) % n).astype(np.float64)
    return np.cos(ang), np.sin(ang)


def _fourier_call(xf, csc, ch, sh, rev, *, batch, seq):
    scale = 1.0 / math.sqrt(seq * FOURIER_GW)
    rows = seq // 2 + FOURIER_PAD
    per_step = max(1, min(batch, FOURIER_STEP_ROWS // seq))
    assert batch % per_step == 0

    def const(shape):
        return pl.BlockSpec(shape, lambda b: (0, 0), pipeline_mode=pl.Buffered(1))

    return pl.pallas_call(
        functools.partial(_fourier_kernel, seq=seq, per_step=per_step, scale=scale),
        out_shape=jax.ShapeDtypeStruct((batch * seq, FOURIER_WIDTH), bf16),
        grid=(batch // per_step,),
        in_specs=[
            pl.BlockSpec((per_step * seq, FOURIER_WIDTH), lambda b: (b, 0)),
            const((FOURIER_GW, 2 * FOURIER_GW)),
            const((rows, seq)),
            const((rows, seq)),
            const((FOURIER_REV, FOURIER_REV)),
        ],
        out_specs=pl.BlockSpec((per_step * seq, FOURIER_WIDTH), lambda b: (b, 0)),
        scratch_shapes=[pltpu.VMEM((2 * seq, per_step * FOURIER_WIDTH), bf16)],
        compiler_params=_cparams(("parallel",)),
        name=f"fourier_{seq}",
    )(xf, csc, ch, sh, rev)


MERGE_CAST_COLS = 512


MERGE_ROW_GROUPS = 2


def _merge_kernel(x_ref, sh_ref, sc_ref, gt_ref, nw_ref, at_ref, rc_ref, fr_ref,
                  wbg32_ref, bbg_ref, wao32_ref, wlo32_ref, wfo32_ref, wo32_ref, o_ref,
                  wbg_ref, wao_ref, wlo_ref, wfo_ref, wo_ref):
    @pl.when(pl.program_id(0) == 0)
    def _():
        for src, dst in ((wbg32_ref, wbg_ref), (wao32_ref, wao_ref), (wlo32_ref, wlo_ref),
                         (wfo32_ref, wfo_ref), (wo32_ref, wo_ref)):
            for c0 in range(0, src.shape[1], MERGE_CAST_COLS):
                dst[:, c0:c0 + MERGE_CAST_COLS] = src[:, c0:c0 + MERGE_CAST_COLS].astype(bf16)

    tm = x_ref.shape[0]
    for r in range(MERGE_ROW_GROUPS):
        rows = slice(r * (tm // MERGE_ROW_GROUPS), (r + 1) * (tm // MERGE_ROW_GROUPS))
        x = x_ref[rows, :]
        h = _modnorm(x, nw_ref[...], sh_ref[...], sc_ref[...]).astype(bf16)
        merged = None
        for idx, (br_ref, w_ref) in enumerate(((at_ref, wao_ref), (rc_ref, wlo_ref), (fr_ref, wfo_ref))):
            cols = slice(idx * D_MODEL, (idx + 1) * D_MODEL)
            g = _sigmoid(_dot(h, wbg_ref[:, cols]) + bbg_ref[:, cols])
            y = g * _dot(br_ref[rows, :], w_ref[...])
            merged = y if merged is None else merged + y
        out = _dot(merged.astype(bf16), wo_ref[...])
        o_ref[rows, :] = x + gt_ref[...] * out


def _merge_call(x, mod5, norm_w4, attn, rec, four, wbg, bbg, wao, wlo, wfo, wo, *, layer, row_fn, tm, name):
    n_tok = x.shape[0]

    def const(shape):
        nd = len(shape)
        return pl.BlockSpec((None,) + shape, lambda i: (layer,) + (0,) * nd,
                            pipeline_mode=pl.Buffered(1))

    def branch():
        return pl.BlockSpec((tm, ATTN_WIDTH), lambda i: (i, 0))

    return pl.pallas_call(
        _merge_kernel,
        out_shape=jax.ShapeDtypeStruct((n_tok, D_MODEL), f32),
        grid=(n_tok // tm,),
        in_specs=[
            pl.BlockSpec((tm, D_MODEL), lambda i: (i, 0)),
            _mod_spec(layer, 3, row_fn),
            _mod_spec(layer, 4, row_fn),
            _mod_spec(layer, 5, row_fn),
            _vec_spec(layer, 1),
            branch(), branch(), branch(),
            const((D_MODEL, N_BRANCH * D_MODEL)),
            const((1, N_BRANCH * D_MODEL)),
            const((ATTN_WIDTH, D_MODEL)),
            const((LRU_WIDTH, D_MODEL)),
            const((FOURIER_WIDTH, D_MODEL)),
            const((D_MODEL, D_MODEL)),
        ],
        out_specs=pl.BlockSpec((tm, D_MODEL), lambda i: (i, 0)),
        scratch_shapes=[pltpu.VMEM((D_MODEL, N_BRANCH * D_MODEL), bf16),
                        pltpu.VMEM((ATTN_WIDTH, D_MODEL), bf16),
                        pltpu.VMEM((LRU_WIDTH, D_MODEL), bf16),
                        pltpu.VMEM((FOURIER_WIDTH, D_MODEL), bf16),
                        pltpu.VMEM((D_MODEL, D_MODEL), bf16)],
        compiler_params=_cparams(("arbitrary",)),
        name=name,
    )(x, mod5, mod5, mod5, norm_w4, attn, rec, four, wbg, bbg, wao, wlo, wfo, wo)


def _rope_lane_tables(n_tokens):
    rows = n_tokens // GRID_W
    row = jnp.repeat(jnp.arange(rows), GRID_W).astype(f32)
    col = jnp.tile(jnp.arange(GRID_W), rows).astype(f32)
    inv = ROPE_BASE ** (-jnp.arange(0, AXIS_ROT, 2, dtype=f32) / AXIS_ROT)
    ang = jnp.stack([row[:, None] * inv, col[:, None] * inv], axis=1)
    cos, sin = jnp.cos(ang), jnp.sin(ang)
    cos_h = jnp.stack([cos, cos], axis=2).reshape(n_tokens, HEAD_DIM)
    sin_h = jnp.stack([-sin, sin], axis=2).reshape(n_tokens, HEAD_DIM)
    reps = LANES // HEAD_DIM
    return jnp.tile(cos_h, (1, reps)), jnp.tile(sin_h, (1, reps))


def _lru_dense_weights(lru_wa, lru_wi, lru_ba, lru_bi):
    per = LRU_BLOCKS // 2
    hw = LRU_HALF
    eye = jnp.eye(per, dtype=bf16)

    def dense(w, c):
        wh = w[:, :, c * per:(c + 1) * per].astype(bf16)
        d = wh[:, :, :, :, None, :] * eye[None, None, :, None, :, None]
        return d.reshape(DEPTH, 2, hw, hw)

    halves, biases = [], []
    for c in range(2):
        sl = slice(c * hw, (c + 1) * hw)
        da, di = dense(lru_wa, c), dense(lru_wi, c)
        halves.append(jnp.concatenate([da[:, 0], di[:, 0], da[:, 1], di[:, 1]], axis=-1))
        biases.append(jnp.concatenate(
            [lru_ba[:, 0, sl], lru_bi[:, 0, sl], lru_ba[:, 1, sl], lru_bi[:, 1, sl]], axis=-1))
    wd = jnp.stack(halves, axis=1)
    bias = jnp.stack(biases, axis=1)[:, :, None, :]
    return wd, bias


def kernel(x_prompt, x_sample, c, cache_k, cache_v, state_lru, c_ctx, w_ada, b_ada, norm_w, final_norm_w,
           ffn1_wg, ffn1_wu, ffn1_wd, ffn2_wg, ffn2_wu, ffn2_wd, w_in, w_branch_gate, b_branch_gate,
           attn_sink, w_attn_out, conv_w, conv_b, lru_wa, lru_ba, lru_wi, lru_bi, lru_lambda,
           w_lru_out, w_fourier_out, w_o):
    batch, seq, _ = x_prompt.shape
    dec_batch, dec_seq, _ = x_sample.shape
    past = cache_k.shape[2]
    assert 1 + dec_batch <= COND_ROWS

    cond = jnp.concatenate([c_ctx[None, :], c, jnp.zeros((COND_ROWS - 1 - dec_batch, D_MODEL), f32)], axis=0)
    mod = _ada_call(cond, w_ada, b_ada)
    mod5 = mod.reshape(DEPTH, COND_ROWS, N_SUB * 3, D_MODEL).transpose(0, 2, 1, 3)[:, :, :, None, :]
    norm_w4 = norm_w[:, :, None, :]
    final_w = final_norm_w[None, :]

    bbg = b_branch_gate[:, None, :]
    merge_w = (w_branch_gate, bbg, w_attn_out, w_lru_out, w_fourier_out, w_o)
    lru_wd, lru_bias = _lru_dense_weights(lru_wa, lru_wi, lru_ba, lru_bi)
    conv_b3 = conv_b[:, None, :]
    rope_tabs = _rope_lane_tables(dec_seq)
    cc, sc_ = _dft_tables(FOURIER_GW)
    csc = jnp.asarray(np.concatenate([cc, sc_], axis=1), f32).astype(bf16)
    dft = {}
    for n in (seq, dec_seq):
        cl, sl = _dft_tables(n)
        rows = n // 2 + FOURIER_PAD
        dft[n] = (jnp.asarray(cl[:rows], f32).astype(bf16), jnp.asarray(sl[:rows], f32).astype(bf16))
    rev = jnp.asarray(np.eye(FOURIER_REV)[::-1], f32).astype(bf16)
    cache_k4 = cache_k.reshape(dec_batch, DEPTH, past, KV_WIDTH)
    cache_v4 = cache_v.reshape(dec_batch, DEPTH, past, KV_WIDTH)
    h0_ctx = jnp.zeros((batch, 2, LRU_WIDTH), f32)

    tm_p = tm_s = FFN_TOKEN_TILE
    tmi = INPROJ_TOKEN_TILE
    tmm = MERGE_TOKEN_TILE
    for tile in (tm_p, tmi, tmm):
        assert (batch * seq) % tile == 0 and dec_seq % tile == 0
    row_p = row_pi = row_pm = lambda i: 0
    row_s = lambda i: 1 + (i * tm_s) // dec_seq
    row_si = lambda i: 1 + (i * tmi) // dec_seq
    row_sm = lambda i: 1 + (i * tmm) // dec_seq

    xp = x_prompt.reshape(batch * seq, D_MODEL)
    xs = x_sample.reshape(dec_batch * dec_seq, D_MODEL)
    ks, vs, ss = [], [], []
    for l in range(DEPTH):
        last = l == DEPTH - 1
        sink = attn_sink[l]
        xp = _ffn_call(xp, mod5, norm_w4, ffn1_wg, ffn1_wu, ffn1_wd, None,
                       layer=l, sub=0, row_fn=row_p, tm=tm_p, name="ffn1_ctx")
        xs = _ffn_call(xs, mod5, norm_w4, ffn1_wg, ffn1_wu, ffn1_wd, None,
                       layer=l, sub=0, row_fn=row_s, tm=tm_s, name="ffn1_lat")
        q, k, v, xr, yr, xf = _inproj_call(xp, mod5, norm_w4, w_in, None,
                                           layer=l, row_fn=row_pi, tm=tmi, seq=seq, name="inproj_ctx")
        ks.append(k.reshape(batch, seq, KV_HEADS, HEAD_DIM))
        vs.append(v.reshape(batch, seq, KV_HEADS, HEAD_DIM))
        attn = _ctx_attn_call(sink, q, k, v, batch=batch, seq=seq)
        rec, st = _lru_call(xr, yr, conv_w, conv_b3, lru_wd, lru_bias, lru_lambda, h0_ctx,
                            layer=l, batch=batch, seq=seq)
        ss.append(st)
        four = _fourier_call(xf, csc, *dft[seq], rev, batch=batch, seq=seq)
        xp = _merge_call(xp, mod5, norm_w4, attn, rec, four, *merge_w,
                         layer=l, row_fn=row_pm, tm=tmm, name="merge_ctx")
        q, k, v, xr, yr, xf = _inproj_call(xs, mod5, norm_w4, w_in, rope_tabs,
                                           layer=l, row_fn=row_si, tm=tmi, seq=dec_seq, name="inproj_lat")
        attn = _lat_attn_call(sink, q, k, v, cache_k4, cache_v4, layer=l, batch=dec_batch, seq=dec_seq)
        rec, _ = _lru_call(xr, yr, conv_w, conv_b3, lru_wd, lru_bias, lru_lambda, state_lru[:, l],
                           layer=l, batch=dec_batch, seq=dec_seq)
        four = _fourier_call(xf, csc, *dft[dec_seq], rev, batch=dec_batch, seq=dec_seq)
        xs = _merge_call(xs, mod5, norm_w4, attn, rec, four, *merge_w,
                         layer=l, row_fn=row_sm, tm=tmm, name="merge_lat")
        xp = _ffn_call(xp, mod5, norm_w4, ffn2_wg, ffn2_wu, ffn2_wd, final_w if last else None,
                       layer=l, sub=2, row_fn=row_p, tm=tm_p, name="ffn2_ctx")
        xs = _ffn_call(xs, mod5, norm_w4, ffn2_wg, ffn2_wu, ffn2_wd, final_w if last else None,
                       layer=l, sub=2, row_fn=row_s, tm=tm_s, name="ffn2_lat")

    y_prompt = xp.reshape(batch, seq, D_MODEL)
    y_sample = xs.reshape(dec_batch, dec_seq, D_MODEL)
    return (y_prompt, y_sample, jnp.stack(ks, axis=1), jnp.stack(vs, axis=1), jnp.stack(ss, axis=1))
```

```python
import functools
import math

import numpy as np
import jax
import jax.numpy as jnp
from jax import lax
from jax.experimental import pallas as pl
from jax.experimental.pallas import tpu as pltpu

f32 = jnp.float32
bf16 = jnp.bfloat16

D_MODEL = 1024
DEPTH = 2
GRID_W = 64
N_HEADS = 8
KV_HEADS = 2
HEAD_DIM = 64
Q_GROUP = N_HEADS // KV_HEADS
ATTN_WIDTH = N_HEADS * HEAD_DIM
KV_WIDTH = KV_HEADS * HEAD_DIM
WINDOW = 128
BLOCK = 128
AXIS_ROT = HEAD_DIM // 2
ROPE_BASE = 10000.0
LRU_WIDTH = 512
LRU_BLOCKS = 8
LRU_BW = LRU_WIDTH // LRU_BLOCKS
LRU_C = 8.0
CONV_W = 4
CONV_LEFT = 2
FOURIER_WIDTH = 512
FOURIER_GROUPS = 4
FOURIER_GW = FOURIER_WIDTH // FOURIER_GROUPS
D_FF = 2816
N_BRANCH = 3
N_SUB = 3
EPS = 1e-6
NEG = -1e30
IN_WIDTH = ATTN_WIDTH + 2 * KV_WIDTH + 2 * LRU_WIDTH + FOURIER_WIDTH

V7X_VMEM_BYTES = 64 * 1024 * 1024
SUBLANES = 8
LANES = 128
MXU_N = 256
assert 2 * KV_WIDTH == MXU_N
COND_ROWS = 8
LRU_HALF = LRU_WIDTH // 2
LRU_SEGMENTS = SUBLANES
PAD_ROWS = SUBLANES

FFN_TOKEN_TILE = 2048
INPROJ_TOKEN_TILE = 1024
MERGE_TOKEN_TILE = 512


def _cparams(sem):
    return pltpu.CompilerParams(dimension_semantics=sem, vmem_limit_bytes=V7X_VMEM_BYTES)


def _dot(a, b):
    return jnp.dot(a, b, preferred_element_type=f32)


def _sigmoid(x):
    return jax.nn.sigmoid(x)


def _modnorm(x, nw, shift, scale):
    y = x * lax.rsqrt(jnp.mean(x * x, axis=-1, keepdims=True) + EPS)
    return y * (nw * (1.0 + scale)) + shift


def _ada_kernel(cond_ref, w_ref, b_ref, o_ref):
    c = cond_ref[...]
    s = (c * _sigmoid(c)).astype(bf16)
    o_ref[...] = _dot(s, w_ref[...].astype(bf16)) + b_ref[...]


def _ada_call(cond, w_ada, b_ada):
    n_out = w_ada.shape[-1]
    tn = 1024
    return pl.pallas_call(
        _ada_kernel,
        out_shape=jax.ShapeDtypeStruct((DEPTH, COND_ROWS, n_out), f32),
        grid=(DEPTH, n_out // tn),
        in_specs=[
            pl.BlockSpec((COND_ROWS, D_MODEL), lambda l, j: (0, 0)),
            pl.BlockSpec((None, D_MODEL, tn), lambda l, j: (l, 0, j)),
            pl.BlockSpec((None, 1, tn), lambda l, j: (l, 0, j)),
        ],
        out_specs=pl.BlockSpec((None, COND_ROWS, tn), lambda l, j: (l, 0, j)),
        compiler_params=_cparams(("parallel", "parallel")),
        name="adaln",
    )(cond, w_ada, b_ada.reshape(DEPTH, 1, n_out))


def _mod_spec(layer, slot, row_fn):
    return pl.BlockSpec((None, None, None, 1, D_MODEL),
                        lambda i, *_: (layer, slot, row_fn(i), 0, 0))


def _vec_spec(layer, sub):
    return pl.BlockSpec((None, None, 1, D_MODEL), lambda i, *_: (layer, sub, 0, 0))


FFN_ROW_GROUPS = 4


def _ffn_kernel(*refs, n_ff, final):
    if final:
        (x_ref, sh_ref, sc_ref, gt_ref, nw_ref, wg_ref, wu_ref, wd_ref, fw_ref,
         o_ref, h_ref) = refs
    else:
        (x_ref, sh_ref, sc_ref, gt_ref, nw_ref, wg_ref, wu_ref, wd_ref,
         o_ref, h_ref) = refs
    j = pl.program_id(1)

    tm = x_ref.shape[0]
    groups = [slice(r * (tm // FFN_ROW_GROUPS), (r + 1) * (tm // FFN_ROW_GROUPS))
              for r in range(FFN_ROW_GROUPS)]

    def down(h, w):
        wg, wu, wd = w
        g = _dot(h, wg)
        u = _dot(h, wu)
        a = ((g * _sigmoid(g)) * u).astype(bf16)
        return _dot(a, wd)

    def weights():
        return (wg_ref[...].astype(bf16), wu_ref[...].astype(bf16), wd_ref[...].astype(bf16))

    @pl.when(j == 0)
    def _():
        w = weights()
        for rows in groups:
            h = _modnorm(x_ref[rows, :], nw_ref[...], sh_ref[...], sc_ref[...]).astype(bf16)
            h_ref[rows, :] = h
            o_ref[rows, :] = down(h, w)

    @pl.when((j > 0) & (j < n_ff - 1))
    def _():
        w = weights()
        for rows in groups:
            o_ref[rows, :] += down(h_ref[rows, :], w)

    @pl.when(j == n_ff - 1)
    def _():
        w = weights()
        for rows in groups:
            y = x_ref[rows, :] + (0.5 * gt_ref[...]) * (o_ref[rows, :] + down(h_ref[rows, :], w))
            if final:
                y = y * lax.rsqrt(jnp.mean(y * y, axis=-1, keepdims=True) + EPS) * fw_ref[...]
            o_ref[rows, :] = y


def _ffn_call(x, mod5, norm_w4, wg, wu, wd, final_w, *, layer, sub, row_fn, tm, name):
    n_tok = x.shape[0]
    tf = 256
    n_ff = D_FF // tf
    final = final_w is not None
    in_specs = [
        pl.BlockSpec((tm, D_MODEL), lambda i, j: (i, 0)),
        _mod_spec(layer, sub * 3 + 0, row_fn),
        _mod_spec(layer, sub * 3 + 1, row_fn),
        _mod_spec(layer, sub * 3 + 2, row_fn),
        _vec_spec(layer, sub),
        pl.BlockSpec((None, D_MODEL, tf), lambda i, j: (layer, 0, j)),
        pl.BlockSpec((None, D_MODEL, tf), lambda i, j: (layer, 0, j)),
        pl.BlockSpec((None, tf, D_MODEL), lambda i, j: (layer, j, 0)),
    ]
    args = [x, mod5, mod5, mod5, norm_w4, wg, wu, wd]
    if final:
        in_specs.append(pl.BlockSpec((1, D_MODEL), lambda i, j: (0, 0)))
        args.append(final_w)
    return pl.pallas_call(
        functools.partial(_ffn_kernel, n_ff=n_ff, final=final),
        out_shape=jax.ShapeDtypeStruct((n_tok, D_MODEL), f32),
        grid=(n_tok // tm, n_ff),
        in_specs=in_specs,
        out_specs=pl.BlockSpec((tm, D_MODEL), lambda i, j: (i, 0)),
        scratch_shapes=[pltpu.VMEM((tm, D_MODEL), bf16)],
        compiler_params=_cparams(("parallel", "arbitrary")),
        name=name,
    )(*args)


def _swap16(x):
    n = x.shape[-1]
    lane = lax.broadcasted_iota(jnp.int32, x.shape, x.ndim - 1)
    first = (lane % AXIS_ROT) < (AXIS_ROT // 2)
    return jnp.where(first, pltpu.roll(x, n - AXIS_ROT // 2, x.ndim - 1),
                     pltpu.roll(x, AXIS_ROT // 2, x.ndim - 1))


INPROJ_ROW_GROUPS = 2


def _inproj_kernel(*refs, rope):
    if rope:
        (x_ref, sh_ref, sc_ref, nw_ref, w32_ref, cos_ref, sin_ref,
         q_ref, k_ref, v_ref, xr_ref, yr_ref, xf_ref, w_ref) = refs
    else:
        (x_ref, sh_ref, sc_ref, nw_ref, w32_ref,
         q_ref, k_ref, v_ref, xr_ref, yr_ref, xf_ref, w_ref) = refs

    @pl.when(pl.program_id(0) == 0)
    def _():
        for c0 in range(0, IN_WIDTH, MXU_N):
            w_ref[:, c0:c0 + MXU_N] = w32_ref[:, c0:c0 + MXU_N].astype(bf16)

    tm = x_ref.shape[0]
    for r in range(INPROJ_ROW_GROUPS):
        rows = slice(r * (tm // INPROJ_ROW_GROUPS), (r + 1) * (tm // INPROJ_ROW_GROUPS))
        h = _modnorm(x_ref[rows, :], nw_ref[...], sh_ref[...], sc_ref[...]).astype(bf16)
        if rope:
            cos = cos_ref[rows, :]
            sin = sin_ref[rows, :]

        def rot(t):
            return t * cos + _swap16(t) * sin if rope else t

        o = ATTN_WIDTH
        kv = _dot(h, w_ref[:, o:o + 2 * KV_WIDTH])
        k_ref[rows, :] = rot(kv[:, :KV_WIDTH])
        v_ref[rows, :] = kv[:, KV_WIDTH:]
        for c in range(ATTN_WIDTH // MXU_N):
            q = _dot(h, w_ref[:, c * MXU_N:(c + 1) * MXU_N])
            for t in range(MXU_N // LANES):
                c0 = c * MXU_N + t * LANES
                q_ref[rows, c0:c0 + LANES] = rot(q[:, t * LANES:(t + 1) * LANES]).astype(bf16)
        o += 2 * KV_WIDTH
        xr_ref[rows, :] = _dot(h, w_ref[:, o:o + LRU_WIDTH])
        o += LRU_WIDTH
        yr_ref[rows, :] = _dot(h, w_ref[:, o:o + LRU_WIDTH])
        o += LRU_WIDTH
        xf_ref[rows, :] = _dot(h, w_ref[:, o:o + FOURIER_WIDTH]).astype(bf16)


def _inproj_call(x, mod5, norm_w4, w_in, rope_tabs, *, layer, row_fn, tm, seq, name):
    n_tok = x.shape[0]
    rope = rope_tabs is not None
    in_specs = [
        pl.BlockSpec((tm, D_MODEL), lambda i: (i, 0)),
        _mod_spec(layer, 3, row_fn),
        _mod_spec(layer, 4, row_fn),
        _vec_spec(layer, 1),
        pl.BlockSpec((None, D_MODEL, IN_WIDTH), lambda i: (layer, 0, 0), pipeline_mode=pl.Buffered(1)),
    ]
    args = [x, mod5, mod5, norm_w4, w_in]
    if rope:
        per_seq = seq // tm
        in_specs += [pl.BlockSpec((tm, LANES), lambda i: (i % per_seq, 0))] * 2
        args += list(rope_tabs)
    widths = (ATTN_WIDTH, KV_WIDTH, KV_WIDTH, LRU_WIDTH, LRU_WIDTH, FOURIER_WIDTH)
    dtypes = (bf16, f32, f32, f32, f32, bf16)
    return pl.pallas_call(
        functools.partial(_inproj_kernel, rope=rope),
        out_shape=[jax.ShapeDtypeStruct((n_tok, w), d) for w, d in zip(widths, dtypes)],
        grid=(n_tok // tm,),
        in_specs=in_specs,
        out_specs=[pl.BlockSpec((tm, w), lambda i: (i, 0)) for w in widths],
        scratch_shapes=[pltpu.VMEM((D_MODEL, IN_WIDTH), bf16)],
        compiler_params=_cparams(("arbitrary",)),
        name=name,
    )(*args)


def _half_lane_variants(x):
    lane = lax.broadcasted_iota(jnp.int32, x.shape, 1)
    low = lane < HEAD_DIM
    sw = pltpu.roll(x, HEAD_DIM, 1)
    zero = jnp.zeros_like(x)
    head0 = (jnp.where(low, x, zero), jnp.where(low, zero, sw))
    head1 = (jnp.where(low, sw, zero), jnp.where(low, zero, x))
    return [tuple(v.astype(bf16) for v in h) for h in (head0, head1)]


def _sink_softmax(scores, sk):
    m = sk
    for s in scores:
        m = jnp.maximum(m, jnp.max(s, axis=-1, keepdims=True))
    denom = jnp.exp(sk - m)
    probs = []
    for s in scores:
        p = jnp.exp(s - m)
        denom = denom + jnp.sum(p, axis=-1, keepdims=True)
        probs.append(p.astype(bf16))
    return probs, denom


def _sink_heads(sink_ref, q_ref, kvar, vvar, bias_fns, o_ref):
    nq = q_ref.shape[0]
    top = lax.broadcasted_iota(jnp.int32, (2 * nq, 1), 0) < nq
    low = lax.broadcasted_iota(jnp.int32, (2 * nq, LANES), 1) < HEAD_DIM
    nt = (((1,), (1,)), ((), ()))
    for g in range(KV_HEADS):
        c0 = g * Q_GROUP * HEAD_DIM
        q2 = jnp.concatenate([q_ref[:, c0:c0 + LANES], q_ref[:, c0 + LANES:c0 + 2 * LANES]], axis=0)
        q2 = q2 * jnp.asarray(HEAD_DIM ** -0.5, bf16)
        parts = []
        for par in range(2):
            scores = []
            for piece, bias_fn in zip(kvar[g][par], bias_fns):
                s = lax.dot_general(q2, piece, nt, preferred_element_type=f32)
                scores.append(s if bias_fn is None else bias_fn(s))
            n_top = g * Q_GROUP + par
            sk = jnp.where(top, sink_ref[n_top], sink_ref[n_top + 2])
            parts.append(_sink_softmax(scores, sk))
        (p_e, d_e), (p_o, d_o) = parts
        o2 = None
        for probs, vals in ((p_e, vvar[g][0]), (p_o, vvar[g][1])):
            for p, v in zip(probs, vals):
                o2 = _dot(p, v) if o2 is None else o2 + _dot(p, v)
        o2 = o2 * jnp.where(low, 1.0 / d_e, 1.0 / d_o)
        o_ref[:, c0:c0 + LANES] = o2[:nq].astype(o_ref.dtype)
        o_ref[:, c0 + LANES:c0 + 2 * LANES] = o2[nq:].astype(o_ref.dtype)


CTX_ATTN_PER_STEP = 2


def _ctx_attn_kernel(sink_ref, q_ref, k_ref, v_ref, o_ref, *, seq):
    for s in range(CTX_ATTN_PER_STEP):
        rows = slice(s * seq, (s + 1) * seq)
        kvar = [[[v] for v in head] for head in _half_lane_variants(k_ref[rows, :])]
        vvar = [[[v] for v in head] for head in _half_lane_variants(v_ref[rows, :])]
        _sink_heads(sink_ref, q_ref.at[rows, :], kvar, vvar, [None], o_ref.at[rows, :])


def _ctx_attn_call(sink, q, k, v, *, batch, seq):
    per = CTX_ATTN_PER_STEP
    assert batch % per == 0
    return pl.pallas_call(
        functools.partial(_ctx_attn_kernel, seq=seq),
        out_shape=jax.ShapeDtypeStruct((batch * seq, ATTN_WIDTH), bf16),
        grid=(batch // per,),
        in_specs=[
            pl.BlockSpec(memory_space=pltpu.SMEM),
            pl.BlockSpec((per * seq, ATTN_WIDTH), lambda b: (b, 0)),
            pl.BlockSpec((per * seq, KV_WIDTH), lambda b: (b, 0)),
            pl.BlockSpec((per * seq, KV_WIDTH), lambda b: (b, 0)),
        ],
        out_specs=pl.BlockSpec((per * seq, ATTN_WIDTH), lambda b: (b, 0)),
        compiler_params=_cparams(("parallel",)),
        name="ctx_attn",
    )(sink, q, k, v)


ATTN_PREP_ROWS = 256


LAT_ATTN_PER_STEP = 4


def _lat_attn_kernel(sink_ref, q_ref, k_ref, v_ref, kx_ref, vx_ref, o_ref, kvar_ref, vvar_ref,
                     *, n_blocks, past):
    n = pl.program_id(1)
    seq = n_blocks * BLOCK
    ctx0 = seq + 2 * BLOCK

    @pl.when(n == 0)
    def _():
        zeros = jnp.zeros((BLOCK, KV_WIDTH), bf16)
        for src, ctx, dst in ((k_ref, kx_ref, kvar_ref), (v_ref, vx_ref, vvar_ref)):
            for i in range(2 * KV_HEADS):
                dst[i, 0:BLOCK, :] = zeros
                dst[i, BLOCK + seq:ctx0, :] = zeros
            for r0 in range(0, seq + past, ATTN_PREP_ROWS):
                rows = min(ATTN_PREP_ROWS, seq + past - r0)
                x = src[r0:r0 + rows, :] if r0 < seq else ctx[r0 - seq:r0 - seq + rows, :]
                d0 = BLOCK + r0 if r0 < seq else ctx0 + r0 - seq
                variants = _half_lane_variants(x)
                for g in range(KV_HEADS):
                    for par in range(2):
                        dst[2 * g + par, d0:d0 + rows, :] = variants[g][par]

    assert WINDOW == BLOCK
    a_idx = lax.broadcasted_iota(jnp.int32, (2 * BLOCK, BLOCK), 0) % BLOCK
    c_idx = lax.broadcasted_iota(jnp.int32, (2 * BLOCK, BLOCK), 1)
    for sub in range(LAT_ATTN_PER_STEP):
        blk = n * LAT_ATTN_PER_STEP + sub
        band0 = pl.multiple_of(blk * BLOCK, BLOCK)

        def pieces(ref, band0=band0):
            return [[[ref[2 * g + par, pl.ds(band0, 3 * BLOCK), :], ref[2 * g + par, ctx0:ctx0 + past, :]]
                     for par in range(2)] for g in range(KV_HEADS)]

        bias_p = jnp.where((c_idx >= a_idx) & (blk > 0), 0.0, NEG).astype(f32)
        bias_n = jnp.where((c_idx <= a_idx) & (blk < n_blocks - 1), 0.0, NEG).astype(f32)

        def band_bias(s, bias_p=bias_p, bias_n=bias_n):
            return jnp.concatenate([s[:, :BLOCK] + bias_p, s[:, BLOCK:2 * BLOCK],
                                    s[:, 2 * BLOCK:] + bias_n], axis=1)

        rows = slice(sub * BLOCK, (sub + 1) * BLOCK)
        _sink_heads(sink_ref, q_ref.at[rows, :], pieces(kvar_ref), pieces(vvar_ref),
                    [band_bias, None], o_ref.at[rows, :])


def _lat_attn_call(sink, q, k, v, cache_k4, cache_v4, *, layer, batch, seq):
    nb = seq // BLOCK
    past = cache_k4.shape[2]
    assert seq % ATTN_PREP_ROWS == 0
    per = LAT_ATTN_PER_STEP
    assert nb % per == 0
    steps = nb // per
    whole = pl.BlockSpec((seq, KV_WIDTH), lambda b, n: (b, 0))
    ctx = pl.BlockSpec((None, None, past, KV_WIDTH), lambda b, n: (b, layer, 0, 0))
    return pl.pallas_call(
        functools.partial(_lat_attn_kernel, n_blocks=nb, past=past),
        out_shape=jax.ShapeDtypeStruct((batch * seq, ATTN_WIDTH), bf16),
        grid=(batch, steps),
        in_specs=[
            pl.BlockSpec(memory_space=pltpu.SMEM),
            pl.BlockSpec((per * BLOCK, ATTN_WIDTH), lambda b, n: (b * steps + n, 0)),
            whole, whole, ctx, ctx,
        ],
        out_specs=pl.BlockSpec((per * BLOCK, ATTN_WIDTH), lambda b, n: (b * steps + n, 0)),
        scratch_shapes=[pltpu.VMEM((2 * KV_HEADS, seq + 2 * BLOCK + past, KV_WIDTH), bf16)] * 2,
        compiler_params=_cparams(("parallel", "arbitrary")),
        name="lat_attn",
    )(sink, q, k, v, cache_k4, cache_v4)


def _softplus(x):
    return jnp.maximum(x, 0.0) + jnp.log1p(jnp.exp(-jnp.abs(x)))


def _gelu_tanh(x):
    c = math.sqrt(2.0 / math.pi)
    t = jnp.tanh(x * (c + (0.044715 * c) * (x * x)))
    hx = 0.5 * x
    return hx + hx * t


SCAN_SKEW = SUBLANES // 2


def _scan_layout(seq):
    seg = seq // LRU_SEGMENTS
    assert seg % SUBLANES == 0 and LRU_SEGMENTS % 2 == 0
    pitch = seg + SCAN_SKEW
    pair = 2 * seg
    return pitch, pair, (LRU_SEGMENTS // 2) * (pair + SUBLANES)


def _scan_row(t, pair):
    return t + SUBLANES * (t // pair)


def _lru_kernel(xr_ref, yr_ref, cw_ref, cb_ref, wd_ref, bias_ref, lam_ref, h0_ref,
                rec_ref, st_ref, xs_ref, af_ref, bf_ref, ab_ref, bb_ref, *, seq):
    seg = seq // LRU_SEGMENTS
    chunk = min(seq, 256)
    hw = LRU_HALF
    n_lt = hw // LANES
    pitch, pair, _ = _scan_layout(seq)
    piece = min(pair, chunk)
    assert pair % piece == 0

    for m in range(LRU_SEGMENTS // 2):
        g0 = m * (pair + SUBLANES) + pair
        for t in range(n_lt):
            for a_ref, b_ref in ((af_ref, bf_ref), (ab_ref, bb_ref)):
                a_ref[t, g0:g0 + SUBLANES, :] = jnp.ones((SUBLANES, LANES), f32)
                b_ref[t, g0:g0 + SUBLANES, :] = jnp.zeros((SUBLANES, LANES), f32)

    zeros_pad = jnp.zeros((PAD_ROWS, hw), f32)
    xs_ref[0:PAD_ROWS, :] = zeros_pad
    xs_ref[PAD_ROWS + seq:2 * PAD_ROWS + seq, :] = zeros_pad
    for c in range(seq // chunk):
        xs_ref[PAD_ROWS + c * chunk:PAD_ROWS + (c + 1) * chunk, :] = xr_ref[c * chunk:(c + 1) * chunk, :]

    neg_c_sp = [-LRU_C * _softplus(-lam_ref[d:d + 1, :]) for d in range(2)]
    a_refs = (af_ref, ab_ref)
    b_refs = (bf_ref, bb_ref)

    for c in range(seq // chunk):
        r0 = c * chunk
        xc = cb_ref[...]
        for j in range(CONV_W):
            s0 = PAD_ROWS + r0 + j - CONV_LEFT
            xc = xc + xs_ref[s0:s0 + chunk, :] * cw_ref[j:j + 1, :]
        y = _dot(xc.astype(bf16), wd_ref[...]) + bias_ref[...]
        for d in range(2):
            r = _sigmoid(y[:, (2 * d) * hw:(2 * d + 1) * hw])
            gi = _sigmoid(y[:, (2 * d + 1) * hw:(2 * d + 2) * hw])
            log_a = neg_c_sp[d] * r
            a = jnp.exp(log_a)
            z = jnp.tanh(log_a) * (-1.0 - a * a)
            b = jnp.where(z > 0.0, z * lax.rsqrt(z), 0.0) * (gi * xc)
            for p0 in range(0, chunk, piece):
                dst = _scan_row(r0 + p0, pair)
                for t in range(n_lt):
                    lanes = slice(t * LANES, (t + 1) * LANES)
                    a_refs[d][t, dst:dst + piece, :] = a[p0:p0 + piece, lanes]
                    b_refs[d][t, dst:dst + piece, :] = b[p0:p0 + piece, lanes]

    def step(j, carry):
        jf = pl.ds(j, LRU_SEGMENTS, stride=pitch)
        jb = pl.ds(pitch - 1 - j, LRU_SEGMENTS, stride=pitch)
        out = []
        for t in range(n_lt):
            hf, pf, hb, pb = carry[4 * t:4 * t + 4]
            a = af_ref[t, jf, :]
            hf = a * hf + bf_ref[t, jf, :]
            pf = a * pf
            bf_ref[t, jf, :] = hf
            af_ref[t, jf, :] = pf
            a = ab_ref[t, jb, :]
            hb = a * hb + bb_ref[t, jb, :]
            pb = a * pb
            bb_ref[t, jb, :] = hb
            ab_ref[t, jb, :] = pb
            out += [hf, pf, hb, pb]
        return tuple(out)

    z = jnp.zeros((LRU_SEGMENTS, LANES), f32)
    o = jnp.ones((LRU_SEGMENTS, LANES), f32)
    fin = lax.fori_loop(0, pitch, step, (z, o, z, o) * n_lt, unroll=2)

    sub = min(pair, 256)
    off = lax.broadcasted_iota(jnp.int32, (sub, LANES), 0)
    for t in range(n_lt):
        hf, pf, hb, pb = fin[4 * t:4 * t + 4]
        lanes = slice(t * LANES, (t + 1) * LANES)
        cf = h0_ref[0:1, lanes]
        carry_f = []
        for s in range(LRU_SEGMENTS):
            carry_f.append(cf)
            cf = hf[s:s + 1, :] + pf[s:s + 1, :] * cf
        cb = h0_ref[1:2, lanes]
        carry_b = [None] * LRU_SEGMENTS
        for s in reversed(range(LRU_SEGMENTS)):
            carry_b[s] = cb
            cb = hb[s:s + 1, :] + pb[s:s + 1, :] * cb
        st_ref[0:1, lanes] = cf
        st_ref[1:2, lanes] = cb
        for m in range(LRU_SEGMENTS // 2):
            for u in range(pair // sub):
                t0 = m * pair + u * sub
                rows = slice(t0, t0 + sub)
                src = slice(_scan_row(t0, pair), _scan_row(t0, pair) + sub)
                lo, hi = u * sub, (u + 1) * sub
                if hi <= pitch:
                    cfs, cbs = carry_f[2 * m], carry_b[2 * m]
                elif lo >= pitch:
                    cfs, cbs = carry_f[2 * m + 1], carry_b[2 * m + 1]
                else:
                    first = off < (pitch - lo)
                    cfs = jnp.where(first, carry_f[2 * m], carry_f[2 * m + 1])
                    cbs = jnp.where(first, carry_b[2 * m], carry_b[2 * m + 1])
                h = ((bf_ref[t, src, :] + af_ref[t, src, :] * cfs)
                     + (bb_ref[t, src, :] + ab_ref[t, src, :] * cbs))
                rec_ref[rows, lanes] = (h * _gelu_tanh(yr_ref[rows, lanes])).astype(rec_ref.dtype)


def _lru_call(xr, yr, conv_w, conv_b, wd, bias, lam, h0, *, layer, batch, seq):
    hw = LRU_HALF
    return pl.pallas_call(
        functools.partial(_lru_kernel, seq=seq),
        out_shape=[jax.ShapeDtypeStruct((batch * seq, LRU_WIDTH), bf16),
                   jax.ShapeDtypeStruct((batch, 2, LRU_WIDTH), f32)],
        grid=(batch, 2),
        in_specs=[
            pl.BlockSpec((seq, hw), lambda b, c: (b, c)),
            pl.BlockSpec((seq, hw), lambda b, c: (b, c)),
            pl.BlockSpec((None, CONV_W, hw), lambda b, c: (layer, 0, c)),
            pl.BlockSpec((None, 1, hw), lambda b, c: (layer, 0, c)),
            pl.BlockSpec((None, None, hw, 4 * hw), lambda b, c: (layer, c, 0, 0)),
            pl.BlockSpec((None, None, 1, 4 * hw), lambda b, c: (layer, c, 0, 0)),
            pl.BlockSpec((None, 2, hw), lambda b, c: (layer, 0, c)),
            pl.BlockSpec((None, 2, hw), lambda b, c: (b, 0, c)),
        ],
        out_specs=[pl.BlockSpec((seq, hw), lambda b, c: (b, c)),
                   pl.BlockSpec((None, 2, hw), lambda b, c: (b, 0, c))],
        scratch_shapes=([pltpu.VMEM((seq + 2 * PAD_ROWS, hw), f32)]
                        + [pltpu.VMEM((hw // LANES, _scan_layout(seq)[2], LANES), f32)] * 4),
        compiler_params=_cparams(("parallel", "parallel")),
        name=f"lru_{seq}",
    )(xr, yr, conv_w, conv_b, wd, bias, lam, h0)


FOURIER_REV = 128
FOURIER_PAD = 16
FOURIER_STEP_ROWS = 2048


def _fourier_kernel(x_ref, csc_ref, ch_ref, sh_ref, rev_ref, o_ref, u_ref, *, seq, per_step, scale):
    half = seq // 2
    chunk = min(seq, 512)
    w = FOURIER_WIDTH
    for s in range(per_step):
        for c in range(seq // chunk):
            rows = slice(s * seq + c * chunk, s * seq + (c + 1) * chunk)
            for g in range(FOURIER_GROUPS):
                cols = slice(g * FOURIER_GW, (g + 1) * FOURIER_GW)
                dst = slice(s * w + g * FOURIER_GW, s * w + (g + 1) * FOURIER_GW)
                u = _dot(x_ref[rows, cols], csc_ref[...])
                u_ref[c * chunk:(c + 1) * chunk, dst] = u[:, :FOURIER_GW].astype(bf16)
                u_ref[seq + c * chunk:seq + (c + 1) * chunk, dst] = u[:, FOURIER_GW:].astype(bf16)
    a = _dot(ch_ref[...], u_ref[0:seq, :])
    b = _dot(sh_ref[...], u_ref[seq:2 * seq, :])
    top = ((a[0:half] - b[0:half]) * scale).astype(o_ref.dtype)
    z = ((a[1:half + 1] + b[1:half + 1]) * scale).astype(bf16)
    nb = half // FOURIER_REV
    for s in range(per_step):
        cols = slice(s * w, (s + 1) * w)
        o_ref[s * seq:s * seq + half, :] = top[:, cols]
        for k in range(nb):
            blk = z[(nb - 1 - k) * FOURIER_REV:(nb - k) * FOURIER_REV, cols]
            r0 = s * seq + half + k * FOURIER_REV
            o_ref[r0:r0 + FOURIER_REV, :] = _dot(rev_ref[...], blk).astype(o_ref.dtype)


def _dft_tables(n):
    k = np.arange(n, dtype=np.int64)
    ang = (2.0 * np.pi / n) * ((k[:, None] * k[None, :]) % n).astype(np.float64)
    return np.cos(ang), np.sin(ang)


def _fourier_call(xf, csc, ch, sh, rev, *, batch, seq):
    scale = 1.0 / math.sqrt(seq * FOURIER_GW)
    rows = seq // 2 + FOURIER_PAD
    per_step = max(1, min(batch, FOURIER_STEP_ROWS // seq))
    assert batch % per_step == 0

    def const(shape):
        return pl.BlockSpec(shape, lambda b: (0, 0), pipeline_mode=pl.Buffered(1))

    return pl.pallas_call(
        functools.partial(_fourier_kernel, seq=seq, per_step=per_step, scale=scale),
        out_shape=jax.ShapeDtypeStruct((batch * seq, FOURIER_WIDTH), bf16),
        grid=(batch // per_step,),
        in_specs=[
            pl.BlockSpec((per_step * seq, FOURIER_WIDTH), lambda b: (b, 0)),
            const((FOURIER_GW, 2 * FOURIER_GW)),
            const((rows, seq)),
            const((rows, seq)),
            const((FOURIER_REV, FOURIER_REV)),
        ],
        out_specs=pl.BlockSpec((per_step * seq, FOURIER_WIDTH), lambda b: (b, 0)),
        scratch_shapes=[pltpu.VMEM((2 * seq, per_step * FOURIER_WIDTH), bf16)],
        compiler_params=_cparams(("parallel",)),
        name=f"fourier_{seq}",
    )(xf, csc, ch, sh, rev)


MERGE_CAST_COLS = 512


MERGE_ROW_GROUPS = 2


def _merge_kernel(x_ref, sh_ref, sc_ref, gt_ref, nw_ref, at_ref, rc_ref, fr_ref,
                  wbg32_ref, bbg_ref, wao32_ref, wlo32_ref, wfo32_ref, wo32_ref, o_ref,
                  wbg_ref, wao_ref, wlo_ref, wfo_ref, wo_ref):
    @pl.when(pl.program_id(0) == 0)
    def _():
        for src, dst in ((wbg32_ref, wbg_ref), (wao32_ref, wao_ref), (wlo32_ref, wlo_ref),
                         (wfo32_ref, wfo_ref), (wo32_ref, wo_ref)):
            for c0 in range(0, src.shape[1], MERGE_CAST_COLS):
                dst[:, c0:c0 + MERGE_CAST_COLS] = src[:, c0:c0 + MERGE_CAST_COLS].astype(bf16)

    tm = x_ref.shape[0]
    for r in range(MERGE_ROW_GROUPS):
        rows = slice(r * (tm // MERGE_ROW_GROUPS), (r + 1) * (tm // MERGE_ROW_GROUPS))
        x = x_ref[rows, :]
        h = _modnorm(x, nw_ref[...], sh_ref[...], sc_ref[...]).astype(bf16)
        merged = None
        for idx, (br_ref, w_ref) in enumerate(((at_ref, wao_ref), (rc_ref, wlo_ref), (fr_ref, wfo_ref))):
            cols = slice(idx * D_MODEL, (idx + 1) * D_MODEL)
            g = _sigmoid(_dot(h, wbg_ref[:, cols]) + bbg_ref[:, cols])
            y = g * _dot(br_ref[rows, :], w_ref[...])
            merged = y if merged is None else merged + y
        out = _dot(merged.astype(bf16), wo_ref[...])
        o_ref[rows, :] = x + gt_ref[...] * out


def _merge_call(x, mod5, norm_w4, attn, rec, four, wbg, bbg, wao, wlo, wfo, wo, *, layer, row_fn, tm, name):
    n_tok = x.shape[0]

    def const(shape):
        nd = len(shape)
        return pl.BlockSpec((None,) + shape, lambda i: (layer,) + (0,) * nd,
                            pipeline_mode=pl.Buffered(1))

    def branch():
        return pl.BlockSpec((tm, ATTN_WIDTH), lambda i: (i, 0))

    return pl.pallas_call(
        _merge_kernel,
        out_shape=jax.ShapeDtypeStruct((n_tok, D_MODEL), f32),
        grid=(n_tok // tm,),
        in_specs=[
            pl.BlockSpec((tm, D_MODEL), lambda i: (i, 0)),
            _mod_spec(layer, 3, row_fn),
            _mod_spec(layer, 4, row_fn),
            _mod_spec(layer, 5, row_fn),
            _vec_spec(layer, 1),
            branch(), branch(), branch(),
            const((D_MODEL, N_BRANCH * D_MODEL)),
            const((1, N_BRANCH * D_MODEL)),
            const((ATTN_WIDTH, D_MODEL)),
            const((LRU_WIDTH, D_MODEL)),
            const((FOURIER_WIDTH, D_MODEL)),
            const((D_MODEL, D_MODEL)),
        ],
        out_specs=pl.BlockSpec((tm, D_MODEL), lambda i: (i, 0)),
        scratch_shapes=[pltpu.VMEM((D_MODEL, N_BRANCH * D_MODEL), bf16),
                        pltpu.VMEM((ATTN_WIDTH, D_MODEL), bf16),
                        pltpu.VMEM((LRU_WIDTH, D_MODEL), bf16),
                        pltpu.VMEM((FOURIER_WIDTH, D_MODEL), bf16),
                        pltpu.VMEM((D_MODEL, D_MODEL), bf16)],
        compiler_params=_cparams(("arbitrary",)),
        name=name,
    )(x, mod5, mod5, mod5, norm_w4, attn, rec, four, wbg, bbg, wao, wlo, wfo, wo)


def _rope_lane_tables(n_tokens):
    rows = n_tokens // GRID_W
    row = np.repeat(np.arange(rows), GRID_W).astype(np.float64)
    col = np.tile(np.arange(GRID_W), rows).astype(np.float64)
    inv = ROPE_BASE ** (-np.arange(0, AXIS_ROT, 2, dtype=np.float64) / AXIS_ROT)
    ang = np.stack([row[:, None] * inv, col[:, None] * inv], axis=1)
    cos, sin = np.cos(ang), np.sin(ang)
    cos_h = np.stack([cos, cos], axis=2).reshape(n_tokens, HEAD_DIM)
    sin_h = np.stack([-sin, sin], axis=2).reshape(n_tokens, HEAD_DIM)
    reps = LANES // HEAD_DIM
    return (jnp.asarray(np.tile(cos_h, (1, reps)), f32), jnp.asarray(np.tile(sin_h, (1, reps)), f32))


def _lru_dense_weights(lru_wa, lru_wi, lru_ba, lru_bi):
    per = LRU_BLOCKS // 2
    hw = LRU_HALF
    eye = jnp.eye(per, dtype=bf16)

    def dense(w, c):
        wh = w[:, :, c * per:(c + 1) * per].astype(bf16)
        d = wh[:, :, :, :, None, :] * eye[None, None, :, None, :, None]
        return d.reshape(DEPTH, 2, hw, hw)

    halves, biases = [], []
    for c in range(2):
        sl = slice(c * hw, (c + 1) * hw)
        da, di = dense(lru_wa, c), dense(lru_wi, c)
        halves.append(jnp.concatenate([da[:, 0], di[:, 0], da[:, 1], di[:, 1]], axis=-1))
        biases.append(jnp.concatenate(
            [lru_ba[:, 0, sl], lru_bi[:, 0, sl], lru_ba[:, 1, sl], lru_bi[:, 1, sl]], axis=-1))
    wd = jnp.stack(halves, axis=1)
    bias = jnp.stack(biases, axis=1)[:, :, None, :]
    return wd, bias


def kernel(x_prompt, x_sample, c, cache_k, cache_v, state_lru, c_ctx, w_ada, b_ada, norm_w, final_norm_w,
           ffn1_wg, ffn1_wu, ffn1_wd, ffn2_wg, ffn2_wu, ffn2_wd, w_in, w_branch_gate, b_branch_gate,
           attn_sink, w_attn_out, conv_w, conv_b, lru_wa, lru_ba, lru_wi, lru_bi, lru_lambda,
           w_lru_out, w_fourier_out, w_o):
    batch, seq, _ = x_prompt.shape
    dec_batch, dec_seq, _ = x_sample.shape
    past = cache_k.shape[2]
    assert 1 + dec_batch <= COND_ROWS

    cond = jnp.concatenate([c_ctx[None, :], c, jnp.zeros((COND_ROWS - 1 - dec_batch, D_MODEL), f32)], axis=0)
    mod = _ada_call(cond, w_ada, b_ada)
    mod5 = mod.reshape(DEPTH, COND_ROWS, N_SUB * 3, D_MODEL).transpose(0, 2, 1, 3)[:, :, :, None, :]
    norm_w4 = norm_w[:, :, None, :]
    final_w = final_norm_w[None, :]

    bbg = b_branch_gate[:, None, :]
    merge_w = (w_branch_gate, bbg, w_attn_out, w_lru_out, w_fourier_out, w_o)
    lru_wd, lru_bias = _lru_dense_weights(lru_wa, lru_wi, lru_ba, lru_bi)
    conv_b3 = conv_b[:, None, :]
    rope_tabs = _rope_lane_tables(dec_seq)
    cc, sc_ = _dft_tables(FOURIER_GW)
    csc = jnp.asarray(np.concatenate([cc, sc_], axis=1), f32).astype(bf16)
    dft = {}
    for n in (seq, dec_seq):
        cl, sl = _dft_tables(n)
        rows = n // 2 + FOURIER_PAD
        dft[n] = (jnp.asarray(cl[:rows], f32).astype(bf16), jnp.asarray(sl[:rows], f32).astype(bf16))
    rev = jnp.asarray(np.eye(FOURIER_REV)[::-1], f32).astype(bf16)
    cache_k4 = cache_k.reshape(dec_batch, DEPTH, past, KV_WIDTH)
    cache_v4 = cache_v.reshape(dec_batch, DEPTH, past, KV_WIDTH)
    h0_ctx = jnp.zeros((batch, 2, LRU_WIDTH), f32)

    tm_p = tm_s = FFN_TOKEN_TILE
    tmi = INPROJ_TOKEN_TILE
    tmm = MERGE_TOKEN_TILE
    for tile in (tm_p, tmi, tmm):
        assert (batch * seq) % tile == 0 and dec_seq % tile == 0
    row_p = row_pi = row_pm = lambda i: 0
    row_s = lambda i: 1 + (i * tm_s) // dec_seq
    row_si = lambda i: 1 + (i * tmi) // dec_seq
    row_sm = lambda i: 1 + (i * tmm) // dec_seq

    xp = x_prompt.reshape(batch * seq, D_MODEL)
    xs = x_sample.reshape(dec_batch * dec_seq, D_MODEL)
    ks, vs, ss = [], [], []
    for l in range(DEPTH):
        last = l == DEPTH - 1
        sink = attn_sink[l]
        xp = _ffn_call(xp, mod5, norm_w4, ffn1_wg, ffn1_wu, ffn1_wd, None,
                       layer=l, sub=0, row_fn=row_p, tm=tm_p, name="ffn1_ctx")
        xs = _ffn_call(xs, mod5, norm_w4, ffn1_wg, ffn1_wu, ffn1_wd, None,
                       layer=l, sub=0, row_fn=row_s, tm=tm_s, name="ffn1_lat")
        q, k, v, xr, yr, xf = _inproj_call(xp, mod5, norm_w4, w_in, None,
                                           layer=l, row_fn=row_pi, tm=tmi, seq=seq, name="inproj_ctx")
        ks.append(k.reshape(batch, seq, KV_HEADS, HEAD_DIM))
        vs.append(v.reshape(batch, seq, KV_HEADS, HEAD_DIM))
        attn = _ctx_attn_call(sink, q, k, v, batch=batch, seq=seq)
        rec, st = _lru_call(xr, yr, conv_w, conv_b3, lru_wd, lru_bias, lru_lambda, h0_ctx,
                            layer=l, batch=batch, seq=seq)
        ss.append(st)
        four = _fourier_call(xf, csc, *dft[seq], rev, batch=batch, seq=seq)
        xp = _merge_call(xp, mod5, norm_w4, attn, rec, four, *merge_w,
                         layer=l, row_fn=row_pm, tm=tmm, name="merge_ctx")
        q, k, v, xr, yr, xf = _inproj_call(xs, mod5, norm_w4, w_in, rope_tabs,
                                           layer=l, row_fn=row_si, tm=tmi, seq=dec_seq, name="inproj_lat")
        attn = _lat_attn_call(sink, q, k, v, cache_k4, cache_v4, layer=l, batch=dec_batch, seq=dec_seq)
        rec, _ = _lru_call(xr, yr, conv_w, conv_b3, lru_wd, lru_bias, lru_lambda, state_lru[:, l],
                           layer=l, batch=dec_batch, seq=dec_seq)
        four = _fourier_call(xf, csc, *dft[dec_seq], rev, batch=dec_batch, seq=dec_seq)
        xs = _merge_call(xs, mod5, norm_w4, attn, rec, four, *merge_w,
                         layer=l, row_fn=row_sm, tm=tmm, name="merge_lat")
        xp = _ffn_call(xp, mod5, norm_w4, ffn2_wg, ffn2_wu, ffn2_wd, final_w if last else None,
                       layer=l, sub=2, row_fn=row_p, tm=tm_p, name="ffn2_ctx")
        xs = _ffn_call(xs, mod5, norm_w4, ffn2_wg, ffn2_wu, ffn2_wd, final_w if last else None,
                       layer=l, sub=2, row_fn=row_s, tm=tm_s, name="ffn2_lat")

    y_prompt = xp.reshape(batch, seq, D_MODEL)
    y_sample = xs.reshape(dec_batch, dec_seq, D_MODEL)
    return (y_prompt, y_sample, jnp.stack(ks, axis=1), jnp.stack(vs, axis=1), jnp.stack(ss, axis=1))
```

```python
import functools
import math

import numpy as np
import jax
import jax.numpy as jnp
from jax import lax
from jax.experimental import pallas as pl
from jax.experimental.pallas import tpu as pltpu

f32 = jnp.float32
bf16 = jnp.bfloat16

D_MODEL = 1024
DEPTH = 2
GRID_W = 64
N_HEADS = 8
KV_HEADS = 2
HEAD_DIM = 64
Q_GROUP = N_HEADS // KV_HEADS
ATTN_WIDTH = N_HEADS * HEAD_DIM
KV_WIDTH = KV_HEADS * HEAD_DIM
WINDOW = 128
BLOCK = 128
AXIS_ROT = HEAD_DIM // 2
ROPE_BASE = 10000.0
LRU_WIDTH = 512
LRU_BLOCKS = 8
LRU_BW = LRU_WIDTH // LRU_BLOCKS
LRU_C = 8.0
CONV_W = 4
CONV_LEFT = 2
FOURIER_WIDTH = 512
FOURIER_GROUPS = 4
FOURIER_GW = FOURIER_WIDTH // FOURIER_GROUPS
D_FF = 2816
N_BRANCH = 3
N_SUB = 3
EPS = 1e-6
NEG = -1e30
IN_WIDTH = ATTN_WIDTH + 2 * KV_WIDTH + 2 * LRU_WIDTH + FOURIER_WIDTH

V7X_VMEM_BYTES = 64 * 1024 * 1024
SUBLANES = 8
LANES = 128
MXU_N = 256
assert 2 * KV_WIDTH == MXU_N
COND_ROWS = 8
LRU_HALF = LRU_WIDTH // 2
LRU_SEGMENTS = SUBLANES
PAD_ROWS = SUBLANES

FFN_TOKEN_TILE = 2048
INPROJ_TOKEN_TILE = 1024
MERGE_TOKEN_TILE = 512


def _cparams(sem):
    return pltpu.CompilerParams(dimension_semantics=sem, vmem_limit_bytes=V7X_VMEM_BYTES)


def _dot(a, b):
    return jnp.dot(a, b, preferred_element_type=f32)


def _sigmoid(x):
    return jax.nn.sigmoid(x)


def _modnorm(x, nw, shift, scale):
    y = x * lax.rsqrt(jnp.mean(x * x, axis=-1, keepdims=True) + EPS)
    return y * (nw * (1.0 + scale)) + shift


def _ada_kernel(cond_ref, w_ref, b_ref, o_ref):
    c = cond_ref[...]
    s = (c * _sigmoid(c)).astype(bf16)
    o_ref[...] = _dot(s, w_ref[...].astype(bf16)) + b_ref[...]


def _ada_call(cond, w_ada, b_ada):
    n_out = w_ada.shape[-1]
    tn = 1024
    return pl.pallas_call(
        _ada_kernel,
        out_shape=jax.ShapeDtypeStruct((DEPTH, COND_ROWS, n_out), f32),
        grid=(DEPTH, n_out // tn),
        in_specs=[
            pl.BlockSpec((COND_ROWS, D_MODEL), lambda l, j: (0, 0)),
            pl.BlockSpec((None, D_MODEL, tn), lambda l, j: (l, 0, j)),
            pl.BlockSpec((None, 1, tn), lambda l, j: (l, 0, j)),
        ],
        out_specs=pl.BlockSpec((None, COND_ROWS, tn), lambda l, j: (l, 0, j)),
        compiler_params=_cparams(("parallel", "parallel")),
        name="adaln",
    )(cond, w_ada, b_ada.reshape(DEPTH, 1, n_out))


def _mod_spec(layer, slot, row_fn):
    return pl.BlockSpec((None, None, None, 1, D_MODEL),
                        lambda i, *_: (layer, slot, row_fn(i), 0, 0))


def _vec_spec(layer, sub):
    return pl.BlockSpec((None, None, 1, D_MODEL), lambda i, *_: (layer, sub, 0, 0))


FFN_ROW_GROUPS = 4


def _ffn_kernel(*refs, n_ff, final):
    if final:
        (x_ref, sh_ref, sc_ref, gt_ref, nw_ref, wg_ref, wu_ref, wd_ref, fw_ref,
         o_ref, h_ref) = refs
    else:
        (x_ref, sh_ref, sc_ref, gt_ref, nw_ref, wg_ref, wu_ref, wd_ref,
         o_ref, h_ref) = refs
    j = pl.program_id(1)

    tm = x_ref.shape[0]
    groups = [slice(r * (tm // FFN_ROW_GROUPS), (r + 1) * (tm // FFN_ROW_GROUPS))
              for r in range(FFN_ROW_GROUPS)]

    def down(h, w):
        wg, wu, wd = w
        g = _dot(h, wg)
        u = _dot(h, wu)
        a = ((g * _sigmoid(g)) * u).astype(bf16)
        return _dot(a, wd)

    def weights():
        return (wg_ref[...].astype(bf16), wu_ref[...].astype(bf16), wd_ref[...].astype(bf16))

    @pl.when(j == 0)
    def _():
        w = weights()
        for rows in groups:
            h = _modnorm(x_ref[rows, :], nw_ref[...], sh_ref[...], sc_ref[...]).astype(bf16)
            h_ref[rows, :] = h
            o_ref[rows, :] = down(h, w)

    @pl.when((j > 0) & (j < n_ff - 1))
    def _():
        w = weights()
        for rows in groups:
            o_ref[rows, :] += down(h_ref[rows, :], w)

    @pl.when(j == n_ff - 1)
    def _():
        w = weights()
        for rows in groups:
            y = x_ref[rows, :] + (0.5 * gt_ref[...]) * (o_ref[rows, :] + down(h_ref[rows, :], w))
            if final:
                y = y * lax.rsqrt(jnp.mean(y * y, axis=-1, keepdims=True) + EPS) * fw_ref[...]
            o_ref[rows, :] = y


def _ffn_call(x, mod5, norm_w4, wg, wu, wd, final_w, *, layer, sub, row_fn, tm, name):
    n_tok = x.shape[0]
    tf = 256
    n_ff = D_FF // tf
    final = final_w is not None
    in_specs = [
        pl.BlockSpec((tm, D_MODEL), lambda i, j: (i, 0)),
        _mod_spec(layer, sub * 3 + 0, row_fn),
        _mod_spec(layer, sub * 3 + 1, row_fn),
        _mod_spec(layer, sub * 3 + 2, row_fn),
        _vec_spec(layer, sub),
        pl.BlockSpec((None, D_MODEL, tf), lambda i, j: (layer, 0, j)),
        pl.BlockSpec((None, D_MODEL, tf), lambda i, j: (layer, 0, j)),
        pl.BlockSpec((None, tf, D_MODEL), lambda i, j: (layer, j, 0)),
    ]
    args = [x, mod5, mod5, mod5, norm_w4, wg, wu, wd]
    if final:
        in_specs.append(pl.BlockSpec((1, D_MODEL), lambda i, j: (0, 0)))
        args.append(final_w)
    return pl.pallas_call(
        functools.partial(_ffn_kernel, n_ff=n_ff, final=final),
        out_shape=jax.ShapeDtypeStruct((n_tok, D_MODEL), f32),
        grid=(n_tok // tm, n_ff),
        in_specs=in_specs,
        out_specs=pl.BlockSpec((tm, D_MODEL), lambda i, j: (i, 0)),
        scratch_shapes=[pltpu.VMEM((tm, D_MODEL), bf16)],
        compiler_params=_cparams(("parallel", "arbitrary")),
        name=name,
    )(*args)


def _swap16(x):
    n = x.shape[-1]
    lane = lax.broadcasted_iota(jnp.int32, x.shape, x.ndim - 1)
    first = (lane % AXIS_ROT) < (AXIS_ROT // 2)
    return jnp.where(first, pltpu.roll(x, n - AXIS_ROT // 2, x.ndim - 1),
                     pltpu.roll(x, AXIS_ROT // 2, x.ndim - 1))


INPROJ_ROW_GROUPS = 2


def _inproj_kernel(*refs, rope):
    if rope:
        (x_ref, sh_ref, sc_ref, nw_ref, w32_ref, cos_ref, sin_ref,
         q_ref, k_ref, v_ref, xr_ref, yr_ref, xf_ref, w_ref) = refs
    else:
        (x_ref, sh_ref, sc_ref, nw_ref, w32_ref,
         q_ref, k_ref, v_ref, xr_ref, yr_ref, xf_ref, w_ref) = refs

    @pl.when(pl.program_id(0) == 0)
    def _():
        for c0 in range(0, IN_WIDTH, MXU_N):
            w_ref[:, c0:c0 + MXU_N] = w32_ref[:, c0:c0 + MXU_N].astype(bf16)

    tm = x_ref.shape[0]
    for r in range(INPROJ_ROW_GROUPS):
        rows = slice(r * (tm // INPROJ_ROW_GROUPS), (r + 1) * (tm // INPROJ_ROW_GROUPS))
        h = _modnorm(x_ref[rows, :], nw_ref[...], sh_ref[...], sc_ref[...]).astype(bf16)
        if rope:
            cos = cos_ref[rows, :]
            sin = sin_ref[rows, :]

        def rot(t):
            return t * cos + _swap16(t) * sin if rope else t

        o = ATTN_WIDTH
        kv = _dot(h, w_ref[:, o:o + 2 * KV_WIDTH])
        k_ref[rows, :] = rot(kv[:, :KV_WIDTH])
        v_ref[rows, :] = kv[:, KV_WIDTH:]
        for c in range(ATTN_WIDTH // MXU_N):
            q = _dot(h, w_ref[:, c * MXU_N:(c + 1) * MXU_N])
            for t in range(MXU_N // LANES):
                c0 = c * MXU_N + t * LANES
                q_ref[rows, c0:c0 + LANES] = rot(q[:, t * LANES:(t + 1) * LANES]).astype(bf16)
        o += 2 * KV_WIDTH
        xr_ref[rows, :] = _dot(h, w_ref[:, o:o + LRU_WIDTH])
        o += LRU_WIDTH
        yr_ref[rows, :] = _dot(h, w_ref[:, o:o + LRU_WIDTH])
        o += LRU_WIDTH
        xf_ref[rows, :] = _dot(h, w_ref[:, o:o + FOURIER_WIDTH]).astype(bf16)


def _inproj_call(x, mod5, norm_w4, w_in, rope_tabs, *, layer, row_fn, tm, seq, name):
    n_tok = x.shape[0]
    rope = rope_tabs is not None
    in_specs = [
        pl.BlockSpec((tm, D_MODEL), lambda i: (i, 0)),
        _mod_spec(layer, 3, row_fn),
        _mod_spec(layer, 4, row_fn),
        _vec_spec(layer, 1),
        pl.BlockSpec((None, D_MODEL, IN_WIDTH), lambda i: (layer, 0, 0), pipeline_mode=pl.Buffered(1)),
    ]
    args = [x, mod5, mod5, norm_w4, w_in]
    if rope:
        per_seq = seq // tm
        in_specs += [pl.BlockSpec((tm, LANES), lambda i: (i % per_seq, 0))] * 2
        args += list(rope_tabs)
    widths = (ATTN_WIDTH, KV_WIDTH, KV_WIDTH, LRU_WIDTH, LRU_WIDTH, FOURIER_WIDTH)
    dtypes = (bf16, f32, f32, f32, f32, bf16)
    return pl.pallas_call(
        functools.partial(_inproj_kernel, rope=rope),
        out_shape=[jax.ShapeDtypeStruct((n_tok, w), d) for w, d in zip(widths, dtypes)],
        grid=(n_tok // tm,),
        in_specs=in_specs,
        out_specs=[pl.BlockSpec((tm, w), lambda i: (i, 0)) for w in widths],
        scratch_shapes=[pltpu.VMEM((D_MODEL, IN_WIDTH), bf16)],
        compiler_params=_cparams(("arbitrary",)),
        name=name,
    )(*args)


def _half_lane_variants(x):
    lane = lax.broadcasted_iota(jnp.int32, x.shape, 1)
    low = lane < HEAD_DIM
    sw = pltpu.roll(x, HEAD_DIM, 1)
    zero = jnp.zeros_like(x)
    head0 = (jnp.where(low, x, zero), jnp.where(low, zero, sw))
    head1 = (jnp.where(low, sw, zero), jnp.where(low, zero, x))
    return [tuple(v.astype(bf16) for v in h) for h in (head0, head1)]


def _sink_softmax(scores, sk):
    m = sk
    for s in scores:
        m = jnp.maximum(m, jnp.max(s, axis=-1, keepdims=True))
    denom = jnp.exp(sk - m)
    probs = []
    for s in scores:
        p = jnp.exp(s - m)
        denom = denom + jnp.sum(p, axis=-1, keepdims=True)
        probs.append(p.astype(bf16))
    return probs, denom


def _sink_heads(sink_ref, q_ref, kvar, vvar, bias_fns, o_ref):
    nq = q_ref.shape[0]
    top = lax.broadcasted_iota(jnp.int32, (2 * nq, 1), 0) < nq
    low = lax.broadcasted_iota(jnp.int32, (2 * nq, LANES), 1) < HEAD_DIM
    nt = (((1,), (1,)), ((), ()))
    for g in range(KV_HEADS):
        c0 = g * Q_GROUP * HEAD_DIM
        q2 = jnp.concatenate([q_ref[:, c0:c0 + LANES], q_ref[:, c0 + LANES:c0 + 2 * LANES]], axis=0)
        q2 = q2 * jnp.asarray(HEAD_DIM ** -0.5, bf16)
        parts = []
        for par in range(2):
            scores = []
            for piece, bias_fn in zip(kvar[g][par], bias_fns):
                s = lax.dot_general(q2, piece, nt, preferred_element_type=f32)
                scores.append(s if bias_fn is None else bias_fn(s))
            n_top = g * Q_GROUP + par
            sk = jnp.where(top, sink_ref[n_top], sink_ref[n_top + 2])
            parts.append(_sink_softmax(scores, sk))
        (p_e, d_e), (p_o, d_o) = parts
        o2 = None
        for probs, vals in ((p_e, vvar[g][0]), (p_o, vvar[g][1])):
            for p, v in zip(probs, vals):
                o2 = _dot(p, v) if o2 is None else o2 + _dot(p, v)
        o2 = o2 * jnp.where(low, 1.0 / d_e, 1.0 / d_o)
        o_ref[:, c0:c0 + LANES] = o2[:nq].astype(o_ref.dtype)
        o_ref[:, c0 + LANES:c0 + 2 * LANES] = o2[nq:].astype(o_ref.dtype)


CTX_ATTN_PER_STEP = 2


def _ctx_attn_kernel(sink_ref, q_ref, k_ref, v_ref, o_ref, *, seq):
    for s in range(CTX_ATTN_PER_STEP):
        rows = slice(s * seq, (s + 1) * seq)
        kvar = [[[v] for v in head] for head in _half_lane_variants(k_ref[rows, :])]
        vvar = [[[v] for v in head] for head in _half_lane_variants(v_ref[rows, :])]
        _sink_heads(sink_ref, q_ref.at[rows, :], kvar, vvar, [None], o_ref.at[rows, :])


def _ctx_attn_call(sink, q, k, v, *, batch, seq):
    per = CTX_ATTN_PER_STEP
    assert batch % per == 0
    return pl.pallas_call(
        functools.partial(_ctx_attn_kernel, seq=seq),
        out_shape=jax.ShapeDtypeStruct((batch * seq, ATTN_WIDTH), bf16),
        grid=(batch // per,),
        in_specs=[
            pl.BlockSpec(memory_space=pltpu.SMEM),
            pl.BlockSpec((per * seq, ATTN_WIDTH), lambda b: (b, 0)),
            pl.BlockSpec((per * seq, KV_WIDTH), lambda b: (b, 0)),
            pl.BlockSpec((per * seq, KV_WIDTH), lambda b: (b, 0)),
        ],
        out_specs=pl.BlockSpec((per * seq, ATTN_WIDTH), lambda b: (b, 0)),
        compiler_params=_cparams(("parallel",)),
        name="ctx_attn",
    )(sink, q, k, v)


ATTN_PREP_ROWS = 256


LAT_ATTN_PER_STEP = 4


def _lat_attn_kernel(sink_ref, q_ref, k_ref, v_ref, kx_ref, vx_ref, o_ref, kvar_ref, vvar_ref,
                     *, n_blocks, past):
    n = pl.program_id(1)
    seq = n_blocks * BLOCK
    ctx0 = seq + 2 * BLOCK

    @pl.when(n == 0)
    def _():
        zeros = jnp.zeros((BLOCK, KV_WIDTH), bf16)
        for src, ctx, dst in ((k_ref, kx_ref, kvar_ref), (v_ref, vx_ref, vvar_ref)):
            for i in range(2 * KV_HEADS):
                dst[i, 0:BLOCK, :] = zeros
                dst[i, BLOCK + seq:ctx0, :] = zeros
            for r0 in range(0, seq + past, ATTN_PREP_ROWS):
                rows = min(ATTN_PREP_ROWS, seq + past - r0)
                x = src[r0:r0 + rows, :] if r0 < seq else ctx[r0 - seq:r0 - seq + rows, :]
                d0 = BLOCK + r0 if r0 < seq else ctx0 + r0 - seq
                variants = _half_lane_variants(x)
                for g in range(KV_HEADS):
                    for par in range(2):
                        dst[2 * g + par, d0:d0 + rows, :] = variants[g][par]

    assert WINDOW == BLOCK
    a_idx = lax.broadcasted_iota(jnp.int32, (2 * BLOCK, BLOCK), 0) % BLOCK
    c_idx = lax.broadcasted_iota(jnp.int32, (2 * BLOCK, BLOCK), 1)
    for sub in range(LAT_ATTN_PER_STEP):
        blk = n * LAT_ATTN_PER_STEP + sub
        band0 = pl.multiple_of(blk * BLOCK, BLOCK)

        def pieces(ref, band0=band0):
            return [[[ref[2 * g + par, pl.ds(band0, 3 * BLOCK), :], ref[2 * g + par, ctx0:ctx0 + past, :]]
                     for par in range(2)] for g in range(KV_HEADS)]

        bias_p = jnp.where((c_idx >= a_idx) & (blk > 0), 0.0, NEG).astype(f32)
        bias_n = jnp.where((c_idx <= a_idx) & (blk < n_blocks - 1), 0.0, NEG).astype(f32)

        def band_bias(s, bias_p=bias_p, bias_n=bias_n):
            return jnp.concatenate([s[:, :BLOCK] + bias_p, s[:, BLOCK:2 * BLOCK],
                                    s[:, 2 * BLOCK:] + bias_n], axis=1)

        rows = slice(sub * BLOCK, (sub + 1) * BLOCK)
        _sink_heads(sink_ref, q_ref.at[rows, :], pieces(kvar_ref), pieces(vvar_ref),
                    [band_bias, None], o_ref.at[rows, :])


def _lat_attn_call(sink, q, k, v, cache_k4, cache_v4, *, layer, batch, seq):
    nb = seq // BLOCK
    past = cache_k4.shape[2]
    assert seq % ATTN_PREP_ROWS == 0
    per = LAT_ATTN_PER_STEP
    assert nb % per == 0
    steps = nb // per
    whole = pl.BlockSpec((seq, KV_WIDTH), lambda b, n: (b, 0))
    ctx = pl.BlockSpec((None, None, past, KV_WIDTH), lambda b, n: (b, layer, 0, 0))
    return pl.pallas_call(
        functools.partial(_lat_attn_kernel, n_blocks=nb, past=past),
        out_shape=jax.ShapeDtypeStruct((batch * seq, ATTN_WIDTH), bf16),
        grid=(batch, steps),
        in_specs=[
            pl.BlockSpec(memory_space=pltpu.SMEM),
            pl.BlockSpec((per * BLOCK, ATTN_WIDTH), lambda b, n: (b * steps + n, 0)),
            whole, whole, ctx, ctx,
        ],
        out_specs=pl.BlockSpec((per * BLOCK, ATTN_WIDTH), lambda b, n: (b * steps + n, 0)),
        scratch_shapes=[pltpu.VMEM((2 * KV_HEADS, seq + 2 * BLOCK + past, KV_WIDTH), bf16)] * 2,
        compiler_params=_cparams(("parallel", "arbitrary")),
        name="lat_attn",
    )(sink, q, k, v, cache_k4, cache_v4)


def _softplus(x):
    return jnp.maximum(x, 0.0) + jnp.log1p(jnp.exp(-jnp.abs(x)))


def _gelu_tanh(x):
    c = math.sqrt(2.0 / math.pi)
    t = jnp.tanh(x * (c + (0.044715 * c) * (x * x)))
    hx = 0.5 * x
    return hx + hx * t


SCAN_SKEW = SUBLANES // 2


def _scan_layout(seq):
    seg = seq // LRU_SEGMENTS
    assert seg % SUBLANES == 0 and LRU_SEGMENTS % 2 == 0
    pitch = seg + SCAN_SKEW
    pair = 2 * seg
    return pitch, pair, (LRU_SEGMENTS // 2) * (pair + SUBLANES)


def _scan_row(t, pair):
    return t + SUBLANES * (t // pair)


def _lru_kernel(xr_ref, yr_ref, cw_ref, cb_ref, wd_ref, bias_ref, lam_ref, h0_ref,
                rec_ref, st_ref, xs_ref, af_ref, bf_ref, ab_ref, bb_ref, *, seq):
    seg = seq // LRU_SEGMENTS
    chunk = min(seq, 256)
    hw = LRU_HALF
    n_lt = hw // LANES
    pitch, pair, _ = _scan_layout(seq)
    piece = min(pair, chunk)
    assert pair % piece == 0

    for m in range(LRU_SEGMENTS // 2):
        g0 = m * (pair + SUBLANES) + pair
        for t in range(n_lt):
            for a_ref, b_ref in ((af_ref, bf_ref), (ab_ref, bb_ref)):
                a_ref[t, g0:g0 + SUBLANES, :] = jnp.ones((SUBLANES, LANES), f32)
                b_ref[t, g0:g0 + SUBLANES, :] = jnp.zeros((SUBLANES, LANES), f32)

    zeros_pad = jnp.zeros((PAD_ROWS, hw), f32)
    xs_ref[0:PAD_ROWS, :] = zeros_pad
    xs_ref[PAD_ROWS + seq:2 * PAD_ROWS + seq, :] = zeros_pad
    for c in range(seq // chunk):
        xs_ref[PAD_ROWS + c * chunk:PAD_ROWS + (c + 1) * chunk, :] = xr_ref[c * chunk:(c + 1) * chunk, :]

    neg_c_sp = [-LRU_C * _softplus(-lam_ref[d:d + 1, :]) for d in range(2)]
    a_refs = (af_ref, ab_ref)
    b_refs = (bf_ref, bb_ref)

    for c in range(seq // chunk):
        r0 = c * chunk
        win = xs_ref[r0:r0 + chunk + 2 * PAD_ROWS, :]
        xc = cb_ref[...]
        for j in range(CONV_W):
            d = j - CONV_LEFT
            tap = win if d == 0 else pltpu.roll(win, (-d) % win.shape[0], 0)
            xc = xc + tap[PAD_ROWS:PAD_ROWS + chunk, :] * cw_ref[j:j + 1, :]
        y = _dot(xc.astype(bf16), wd_ref[...]) + bias_ref[...]
        for d in range(2):
            r = _sigmoid(y[:, (2 * d) * hw:(2 * d + 1) * hw])
            gi = _sigmoid(y[:, (2 * d + 1) * hw:(2 * d + 2) * hw])
            log_a = neg_c_sp[d] * r
            a = jnp.exp(log_a)
            z = jnp.tanh(log_a) * (-1.0 - a * a)
            b = jnp.where(z > 0.0, z * lax.rsqrt(z), 0.0) * (gi * xc)
            for p0 in range(0, chunk, piece):
                dst = _scan_row(r0 + p0, pair)
                for t in range(n_lt):
                    lanes = slice(t * LANES, (t + 1) * LANES)
                    a_refs[d][t, dst:dst + piece, :] = a[p0:p0 + piece, lanes]
                    b_refs[d][t, dst:dst + piece, :] = b[p0:p0 + piece, lanes]

    def step(j, carry):
        jf = pl.ds(j, LRU_SEGMENTS, stride=pitch)
        jb = pl.ds(pitch - 1 - j, LRU_SEGMENTS, stride=pitch)
        out = []
        for t in range(n_lt):
            hf, pf, hb, pb = carry[4 * t:4 * t + 4]
            a = af_ref[t, jf, :]
            hf = a * hf + bf_ref[t, jf, :]
            pf = a * pf
            bf_ref[t, jf, :] = hf
            af_ref[t, jf, :] = pf
            a = ab_ref[t, jb, :]
            hb = a * hb + bb_ref[t, jb, :]
            pb = a * pb
            bb_ref[t, jb, :] = hb
            ab_ref[t, jb, :] = pb
            out += [hf, pf, hb, pb]
        return tuple(out)

    z = jnp.zeros((LRU_SEGMENTS, LANES), f32)
    o = jnp.ones((LRU_SEGMENTS, LANES), f32)
    fin = lax.fori_loop(0, pitch, step, (z, o, z, o) * n_lt, unroll=2)

    sub = min(pair, 256)
    off = lax.broadcasted_iota(jnp.int32, (sub, LANES), 0)
    for t in range(n_lt):
        hf, pf, hb, pb = fin[4 * t:4 * t + 4]
        lanes = slice(t * LANES, (t + 1) * LANES)
        cf = h0_ref[0:1, lanes]
        carry_f = []
        for s in range(LRU_SEGMENTS):
            carry_f.append(cf)
            cf = hf[s:s + 1, :] + pf[s:s + 1, :] * cf
        cb = h0_ref[1:2, lanes]
        carry_b = [None] * LRU_SEGMENTS
        for s in reversed(range(LRU_SEGMENTS)):
            carry_b[s] = cb
            cb = hb[s:s + 1, :] + pb[s:s + 1, :] * cb
        st_ref[0:1, lanes] = cf
        st_ref[1:2, lanes] = cb
        for m in range(LRU_SEGMENTS // 2):
            for u in range(pair // sub):
                t0 = m * pair + u * sub
                rows = slice(t0, t0 + sub)
                src = slice(_scan_row(t0, pair), _scan_row(t0, pair) + sub)
                lo, hi = u * sub, (u + 1) * sub
                if hi <= pitch:
                    cfs, cbs = carry_f[2 * m], carry_b[2 * m]
                elif lo >= pitch:
                    cfs, cbs = carry_f[2 * m + 1], carry_b[2 * m + 1]
                else:
                    first = off < (pitch - lo)
                    cfs = jnp.where(first, carry_f[2 * m], carry_f[2 * m + 1])
                    cbs = jnp.where(first, carry_b[2 * m], carry_b[2 * m + 1])
                h = ((bf_ref[t, src, :] + af_ref[t, src, :] * cfs)
                     + (bb_ref[t, src, :] + ab_ref[t, src, :] * cbs))
                rec_ref[rows, lanes] = (h * _gelu_tanh(yr_ref[rows, lanes])).astype(rec_ref.dtype)


def _lru_call(xr, yr, conv_w, conv_b, wd, bias, lam, h0, *, layer, batch, seq):
    hw = LRU_HALF
    return pl.pallas_call(
        functools.partial(_lru_kernel, seq=seq),
        out_shape=[jax.ShapeDtypeStruct((batch * seq, LRU_WIDTH), bf16),
                   jax.ShapeDtypeStruct((batch, 2, LRU_WIDTH), f32)],
        grid=(batch, 2),
        in_specs=[
            pl.BlockSpec((seq, hw), lambda b, c: (b, c)),
            pl.BlockSpec((seq, hw), lambda b, c: (b, c)),
            pl.BlockSpec((None, CONV_W, hw), lambda b, c: (layer, 0, c)),
            pl.BlockSpec((None, 1, hw), lambda b, c: (layer, 0, c)),
            pl.BlockSpec((None, None, hw, 4 * hw), lambda b, c: (layer, c, 0, 0)),
            pl.BlockSpec((None, None, 1, 4 * hw), lambda b, c: (layer, c, 0, 0)),
            pl.BlockSpec((None, 2, hw), lambda b, c: (layer, 0, c)),
            pl.BlockSpec((None, 2, hw), lambda b, c: (b, 0, c)),
        ],
        out_specs=[pl.BlockSpec((seq, hw), lambda b, c: (b, c)),
                   pl.BlockSpec((None, 2, hw), lambda b, c: (b, 0, c))],
        scratch_shapes=([pltpu.VMEM((seq + 2 * PAD_ROWS, hw), f32)]
                        + [pltpu.VMEM((hw // LANES, _scan_layout(seq)[2], LANES), f32)] * 4),
        compiler_params=_cparams(("parallel", "parallel")),
        name=f"lru_{seq}",
    )(xr, yr, conv_w, conv_b, wd, bias, lam, h0)


FOURIER_REV = 128
FOURIER_PAD = 16
FOURIER_STEP_ROWS = 2048


def _fourier_kernel(x_ref, csc_ref, ch_ref, sh_ref, rev_ref, o_ref, u_ref, *, seq, per_step, scale):
    half = seq // 2
    chunk = min(seq, 512)
    w = FOURIER_WIDTH
    for s in range(per_step):
        for c in range(seq // chunk):
            rows = slice(s * seq + c * chunk, s * seq + (c + 1) * chunk)
            for g in range(FOURIER_GROUPS):
                cols = slice(g * FOURIER_GW, (g + 1) * FOURIER_GW)
                dst = slice(s * w + g * FOURIER_GW, s * w + (g + 1) * FOURIER_GW)
                u = _dot(x_ref[rows, cols], csc_ref[...])
                u_ref[c * chunk:(c + 1) * chunk, dst] = u[:, :FOURIER_GW].astype(bf16)
                u_ref[seq + c * chunk:seq + (c + 1) * chunk, dst] = u[:, FOURIER_GW:].astype(bf16)
    a = _dot(ch_ref[...], u_ref[0:seq, :])
    b = _dot(sh_ref[...], u_ref[seq:2 * seq, :])
    top = ((a[0:half] - b[0:half]) * scale).astype(o_ref.dtype)
    z = ((a[1:half + 1] + b[1:half + 1]) * scale).astype(bf16)
    nb = half // FOURIER_REV
    for s in range(per_step):
        cols = slice(s * w, (s + 1) * w)
        o_ref[s * seq:s * seq + half, :] = top[:, cols]
        for k in range(nb):
            blk = z[(nb - 1 - k) * FOURIER_REV:(nb - k) * FOURIER_REV, cols]
            r0 = s * seq + half + k * FOURIER_REV
            o_ref[r0:r0 + FOURIER_REV, :] = _dot(rev_ref[...], blk).astype(o_ref.dtype)


def _dft_tables(n):
    k = np.arange(n, dtype=np.int64)
    ang = (2.0 * np.pi / n) * ((k[:, None] * k[None, :]) % n).astype(np.float64)
    return np.cos(ang), np.sin(ang)


def _fourier_call(xf, csc, ch, sh, rev, *, batch, seq):
    scale = 1.0 / math.sqrt(seq * FOURIER_GW)
    rows = seq // 2 + FOURIER_PAD
    per_step = max(1, min(batch, FOURIER_STEP_ROWS // seq))
    assert batch % per_step == 0

    def const(shape):
        return pl.BlockSpec(shape, lambda b: (0, 0), pipeline_mode=pl.Buffered(1))

    return pl.pallas_call(
        functools.partial(_fourier_kernel, seq=seq, per_step=per_step, scale=scale),
        out_shape=jax.ShapeDtypeStruct((batch * seq, FOURIER_WIDTH), bf16),
        grid=(batch // per_step,),
        in_specs=[
            pl.BlockSpec((per_step * seq, FOURIER_WIDTH), lambda b: (b, 0)),
            const((FOURIER_GW, 2 * FOURIER_GW)),
            const((rows, seq)),
            const((rows, seq)),
            const((FOURIER_REV, FOURIER_REV)),
        ],
        out_specs=pl.BlockSpec((per_step * seq, FOURIER_WIDTH), lambda b: (b, 0)),
        scratch_shapes=[pltpu.VMEM((2 * seq, per_step * FOURIER_WIDTH), bf16)],
        compiler_params=_cparams(("parallel",)),
        name=f"fourier_{seq}",
    )(xf, csc, ch, sh, rev)


MERGE_CAST_COLS = 512


MERGE_ROW_GROUPS = 2


def _merge_kernel(x_ref, sh_ref, sc_ref, gt_ref, nw_ref, at_ref, rc_ref, fr_ref,
                  wbg32_ref, bbg_ref, wao32_ref, wlo32_ref, wfo32_ref, wo32_ref, o_ref,
                  wbg_ref, wao_ref, wlo_ref, wfo_ref, wo_ref):
    @pl.when(pl.program_id(0) == 0)
    def _():
        for src, dst in ((wbg32_ref, wbg_ref), (wao32_ref, wao_ref), (wlo32_ref, wlo_ref),
                         (wfo32_ref, wfo_ref), (wo32_ref, wo_ref)):
            for c0 in range(0, src.shape[1], MERGE_CAST_COLS):
                dst[:, c0:c0 + MERGE_CAST_COLS] = src[:, c0:c0 + MERGE_CAST_COLS].astype(bf16)

    tm = x_ref.shape[0]
    for r in range(MERGE_ROW_GROUPS):
        rows = slice(r * (tm // MERGE_ROW_GROUPS), (r + 1) * (tm // MERGE_ROW_GROUPS))
        x = x_ref[rows, :]
        h = _modnorm(x, nw_ref[...], sh_ref[...], sc_ref[...]).astype(bf16)
        merged = None
        for idx, (br_ref, w_ref) in enumerate(((at_ref, wao_ref), (rc_ref, wlo_ref), (fr_ref, wfo_ref))):
            cols = slice(idx * D_MODEL, (idx + 1) * D_MODEL)
            g = _sigmoid(_dot(h, wbg_ref[:, cols]) + bbg_ref[:, cols])
            y = g * _dot(br_ref[rows, :], w_ref[...])
            merged = y if merged is None else merged + y
        out = _dot(merged.astype(bf16), wo_ref[...])
        o_ref[rows, :] = x + gt_ref[...] * out


def _merge_call(x, mod5, norm_w4, attn, rec, four, wbg, bbg, wao, wlo, wfo, wo, *, layer, row_fn, tm, name):
    n_tok = x.shape[0]

    def const(shape):
        nd = len(shape)
        return pl.BlockSpec((None,) + shape, lambda i: (layer,) + (0,) * nd,
                            pipeline_mode=pl.Buffered(1))

    def branch():
        return pl.BlockSpec((tm, ATTN_WIDTH), lambda i: (i, 0))

    return pl.pallas_call(
        _merge_kernel,
        out_shape=jax.ShapeDtypeStruct((n_tok, D_MODEL), f32),
        grid=(n_tok // tm,),
        in_specs=[
            pl.BlockSpec((tm, D_MODEL), lambda i: (i, 0)),
            _mod_spec(layer, 3, row_fn),
            _mod_spec(layer, 4, row_fn),
            _mod_spec(layer, 5, row_fn),
            _vec_spec(layer, 1),
            branch(), branch(), branch(),
            const((D_MODEL, N_BRANCH * D_MODEL)),
            const((1, N_BRANCH * D_MODEL)),
            const((ATTN_WIDTH, D_MODEL)),
            const((LRU_WIDTH, D_MODEL)),
            const((FOURIER_WIDTH, D_MODEL)),
            const((D_MODEL, D_MODEL)),
        ],
        out_specs=pl.BlockSpec((tm, D_MODEL), lambda i: (i, 0)),
        scratch_shapes=[pltpu.VMEM((D_MODEL, N_BRANCH * D_MODEL), bf16),
                        pltpu.VMEM((ATTN_WIDTH, D_MODEL), bf16),
                        pltpu.VMEM((LRU_WIDTH, D_MODEL), bf16),
                        pltpu.VMEM((FOURIER_WIDTH, D_MODEL), bf16),
                        pltpu.VMEM((D_MODEL, D_MODEL), bf16)],
        compiler_params=_cparams(("arbitrary",)),
        name=name,
    )(x, mod5, mod5, mod5, norm_w4, attn, rec, four, wbg, bbg, wao, wlo, wfo, wo)


def _rope_lane_tables(n_tokens):
    rows = n_tokens // GRID_W
    row = np.repeat(np.arange(rows), GRID_W).astype(np.float64)
    col = np.tile(np.arange(GRID_W), rows).astype(np.float64)
    inv = ROPE_BASE ** (-np.arange(0, AXIS_ROT, 2, dtype=np.float64) / AXIS_ROT)
    ang = np.stack([row[:, None] * inv, col[:, None] * inv], axis=1)
    cos, sin = np.cos(ang), np.sin(ang)
    cos_h = np.stack([cos, cos], axis=2).reshape(n_tokens, HEAD_DIM)
    sin_h = np.stack([-sin, sin], axis=2).reshape(n_tokens, HEAD_DIM)
    reps = LANES // HEAD_DIM
    return (jnp.asarray(np.tile(cos_h, (1, reps)), f32), jnp.asarray(np.tile(sin_h, (1, reps)), f32))


def _lru_dense_weights(lru_wa, lru_wi, lru_ba, lru_bi):
    per = LRU_BLOCKS // 2
    hw = LRU_HALF
    eye = jnp.eye(per, dtype=bf16)

    def dense(w, c):
        wh = w[:, :, c * per:(c + 1) * per].astype(bf16)
        d = wh[:, :, :, :, None, :] * eye[None, None, :, None, :, None]
        return d.reshape(DEPTH, 2, hw, hw)

    halves, biases = [], []
    for c in range(2):
        sl = slice(c * hw, (c + 1) * hw)
        da, di = dense(lru_wa, c), dense(lru_wi, c)
        halves.append(jnp.concatenate([da[:, 0], di[:, 0], da[:, 1], di[:, 1]], axis=-1))
        biases.append(jnp.concatenate(
            [lru_ba[:, 0, sl], lru_bi[:, 0, sl], lru_ba[:, 1, sl], lru_bi[:, 1, sl]], axis=-1))
    wd = jnp.stack(halves, axis=1)
    bias = jnp.stack(biases, axis=1)[:, :, None, :]
    return wd, bias


def kernel(x_prompt, x_sample, c, cache_k, cache_v, state_lru, c_ctx, w_ada, b_ada, norm_w, final_norm_w,
           ffn1_wg, ffn1_wu, ffn1_wd, ffn2_wg, ffn2_wu, ffn2_wd, w_in, w_branch_gate, b_branch_gate,
           attn_sink, w_attn_out, conv_w, conv_b, lru_wa, lru_ba, lru_wi, lru_bi, lru_lambda,
           w_lru_out, w_fourier_out, w_o):
    batch, seq, _ = x_prompt.shape
    dec_batch, dec_seq, _ = x_sample.shape
    past = cache_k.shape[2]
    assert 1 + dec_batch <= COND_ROWS

    cond = jnp.concatenate([c_ctx[None, :], c, jnp.zeros((COND_ROWS - 1 - dec_batch, D_MODEL), f32)], axis=0)
    mod = _ada_call(cond, w_ada, b_ada)
    mod5 = mod.reshape(DEPTH, COND_ROWS, N_SUB * 3, D_MODEL).transpose(0, 2, 1, 3)[:, :, :, None, :]
    norm_w4 = norm_w[:, :, None, :]
    final_w = final_norm_w[None, :]

    bbg = b_branch_gate[:, None, :]
    merge_w = (w_branch_gate, bbg, w_attn_out, w_lru_out, w_fourier_out, w_o)
    lru_wd, lru_bias = _lru_dense_weights(lru_wa, lru_wi, lru_ba, lru_bi)
    conv_b3 = conv_b[:, None, :]
    rope_tabs = _rope_lane_tables(dec_seq)
    cc, sc_ = _dft_tables(FOURIER_GW)
    csc = jnp.asarray(np.concatenate([cc, sc_], axis=1), f32).astype(bf16)
    dft = {}
    for n in (seq, dec_seq):
        cl, sl = _dft_tables(n)
        rows = n // 2 + FOURIER_PAD
        dft[n] = (jnp.asarray(cl[:rows], f32).astype(bf16), jnp.asarray(sl[:rows], f32).astype(bf16))
    rev = jnp.asarray(np.eye(FOURIER_REV)[::-1], f32).astype(bf16)
    cache_k4 = cache_k.reshape(dec_batch, DEPTH, past, KV_WIDTH)
    cache_v4 = cache_v.reshape(dec_batch, DEPTH, past, KV_WIDTH)
    h0_ctx = jnp.zeros((batch, 2, LRU_WIDTH), f32)

    tm_p = tm_s = FFN_TOKEN_TILE
    tmi = INPROJ_TOKEN_TILE
    tmm = MERGE_TOKEN_TILE
    for tile in (tm_p, tmi, tmm):
        assert (batch * seq) % tile == 0 and dec_seq % tile == 0
    row_p = row_pi = row_pm = lambda i: 0
    row_s = lambda i: 1 + (i * tm_s) // dec_seq
    row_si = lambda i: 1 + (i * tmi) // dec_seq
    row_sm = lambda i: 1 + (i * tmm) // dec_seq

    xp = x_prompt.reshape(batch * seq, D_MODEL)
    xs = x_sample.reshape(dec_batch * dec_seq, D_MODEL)
    ks, vs, ss = [], [], []
    for l in range(DEPTH):
        last = l == DEPTH - 1
        sink = attn_sink[l]
        xp = _ffn_call(xp, mod5, norm_w4, ffn1_wg, ffn1_wu, ffn1_wd, None,
                       layer=l, sub=0, row_fn=row_p, tm=tm_p, name="ffn1_ctx")
        xs = _ffn_call(xs, mod5, norm_w4, ffn1_wg, ffn1_wu, ffn1_wd, None,
                       layer=l, sub=0, row_fn=row_s, tm=tm_s, name="ffn1_lat")
        q, k, v, xr, yr, xf = _inproj_call(xp, mod5, norm_w4, w_in, None,
                                           layer=l, row_fn=row_pi, tm=tmi, seq=seq, name="inproj_ctx")
        ks.append(k.reshape(batch, seq, KV_HEADS, HEAD_DIM))
        vs.append(v.reshape(batch, seq, KV_HEADS, HEAD_DIM))
        attn = _ctx_attn_call(sink, q, k, v, batch=batch, seq=seq)
        rec, st = _lru_call(xr, yr, conv_w, conv_b3, lru_wd, lru_bias, lru_lambda, h0_ctx,
                            layer=l, batch=batch, seq=seq)
        ss.append(st)
        four = _fourier_call(xf, csc, *dft[seq], rev, batch=batch, seq=seq)
        xp = _merge_call(xp, mod5, norm_w4, attn, rec, four, *merge_w,
                         layer=l, row_fn=row_pm, tm=tmm, name="merge_ctx")
        q, k, v, xr, yr, xf = _inproj_call(xs, mod5, norm_w4, w_in, rope_tabs,
                                           layer=l, row_fn=row_si, tm=tmi, seq=dec_seq, name="inproj_lat")
        attn = _lat_attn_call(sink, q, k, v, cache_k4, cache_v4, layer=l, batch=dec_batch, seq=dec_seq)
        rec, _ = _lru_call(xr, yr, conv_w, conv_b3, lru_wd, lru_bias, lru_lambda, state_lru[:, l],
                           layer=l, batch=dec_batch, seq=dec_seq)
        four = _fourier_call(xf, csc, *dft[dec_seq], rev, batch=dec_batch, seq=dec_seq)
        xs = _merge_call(xs, mod5, norm_w4, attn, rec, four, *merge_w,
                         layer=l, row_fn=row_sm, tm=tmm, name="merge_lat")
        xp = _ffn_call(xp, mod5, norm_w4, ffn2_wg, ffn2_wu, ffn2_wd, final_w if last else None,
                       layer=l, sub=2, row_fn=row_p, tm=tm_p, name="ffn2_ctx")
        xs = _ffn_call(xs, mod5, norm_w4, ffn2_wg, ffn2_wu, ffn2_wd, final_w if last else None,
                       layer=l, sub=2, row_fn=row_s, tm=tm_s, name="ffn2_lat")

    y_prompt = xp.reshape(batch, seq, D_MODEL)
    y_sample = xs.reshape(dec_batch, dec_seq, D_MODEL)
    return (y_prompt, y_sample, jnp.stack(ks, axis=1), jnp.stack(vs, axis=1), jnp.stack(ss, axis=1))
```

```python
import functools
import math

import numpy as np
import jax
import jax.numpy as jnp
from jax import lax
from jax.experimental import pallas as pl
from jax.experimental.pallas import tpu as pltpu

f32 = jnp.float32
bf16 = jnp.bfloat16

D_MODEL = 1024
DEPTH = 2
GRID_W = 64
N_HEADS = 8
KV_HEADS = 2
HEAD_DIM = 64
Q_GROUP = N_HEADS // KV_HEADS
ATTN_WIDTH = N_HEADS * HEAD_DIM
KV_WIDTH = KV_HEADS * HEAD_DIM
WINDOW = 128
BLOCK = 128
AXIS_ROT = HEAD_DIM // 2
ROPE_BASE = 10000.0
LRU_WIDTH = 512
LRU_BLOCKS = 8
LRU_BW = LRU_WIDTH // LRU_BLOCKS
LRU_C = 8.0
CONV_W = 4
CONV_LEFT = 2
FOURIER_WIDTH = 512
FOURIER_GROUPS = 4
FOURIER_GW = FOURIER_WIDTH // FOURIER_GROUPS
D_FF = 2816
N_BRANCH = 3
N_SUB = 3
EPS = 1e-6
NEG = -1e30
IN_WIDTH = ATTN_WIDTH + 2 * KV_WIDTH + 2 * LRU_WIDTH + FOURIER_WIDTH

V7X_VMEM_BYTES = 64 * 1024 * 1024
SUBLANES = 8
LANES = 128
MXU_N = 256
assert 2 * KV_WIDTH == MXU_N
COND_ROWS = 8
LRU_HALF = LRU_WIDTH // 2
LRU_SEGMENTS = SUBLANES
PAD_ROWS = SUBLANES

ADA_COL_TILE = 2304
FFN_TOKEN_TILE = 2048
INPROJ_TOKEN_TILE = 1024
MERGE_TOKEN_TILE = 512


def _cparams(sem):
    return pltpu.CompilerParams(dimension_semantics=sem, vmem_limit_bytes=V7X_VMEM_BYTES)


def _dot(a, b):
    return jnp.dot(a, b, preferred_element_type=f32)


def _sigmoid(x):
    return jax.nn.sigmoid(x)


def _modnorm(x, nw, shift, scale):
    y = x * lax.rsqrt(jnp.mean(x * x, axis=-1, keepdims=True) + EPS)
    return y * (nw * (1.0 + scale)) + shift


def _ada_kernel(cond_ref, w_ref, b_ref, o_ref):
    c = cond_ref[...]
    s = (c * _sigmoid(c)).astype(bf16)
    o_ref[...] = _dot(s, w_ref[...].astype(bf16)) + b_ref[...]


def _ada_call(cond, w_ada, b_ada):
    n_out = w_ada.shape[-1]
    tn = ADA_COL_TILE
    assert n_out % tn == 0
    return pl.pallas_call(
        _ada_kernel,
        out_shape=jax.ShapeDtypeStruct((DEPTH, COND_ROWS, n_out), f32),
        grid=(DEPTH, n_out // tn),
        in_specs=[
            pl.BlockSpec((COND_ROWS, D_MODEL), lambda l, j: (0, 0)),
            pl.BlockSpec((None, D_MODEL, tn), lambda l, j: (l, 0, j)),
            pl.BlockSpec((None, 1, tn), lambda l, j: (l, 0, j)),
        ],
        out_specs=pl.BlockSpec((None, COND_ROWS, tn), lambda l, j: (l, 0, j)),
        compiler_params=_cparams(("parallel", "parallel")),
        name="adaln",
    )(cond, w_ada, b_ada.reshape(DEPTH, 1, n_out))


def _mod_spec(layer, slot, row_fn):
    return pl.BlockSpec((None, None, None, 1, D_MODEL),
                        lambda i, *_: (layer, slot, row_fn(i), 0, 0))


def _vec_spec(layer, sub):
    return pl.BlockSpec((None, None, 1, D_MODEL), lambda i, *_: (layer, sub, 0, 0))


FFN_ROW_GROUPS = 4


def _ffn_kernel(*refs, n_ff, final):
    if final:
        (x_ref, sh_ref, sc_ref, gt_ref, nw_ref, wg_ref, wu_ref, wd_ref, fw_ref,
         o_ref, h_ref) = refs
    else:
        (x_ref, sh_ref, sc_ref, gt_ref, nw_ref, wg_ref, wu_ref, wd_ref,
         o_ref, h_ref) = refs
    j = pl.program_id(1)

    tm = x_ref.shape[0]
    groups = [slice(r * (tm // FFN_ROW_GROUPS), (r + 1) * (tm // FFN_ROW_GROUPS))
              for r in range(FFN_ROW_GROUPS)]

    def down(h, w):
        wg, wu, wd = w
        g = _dot(h, wg)
        u = _dot(h, wu)
        a = ((g * _sigmoid(g)) * u).astype(bf16)
        return _dot(a, wd)

    def weights():
        return (wg_ref[...].astype(bf16), wu_ref[...].astype(bf16), wd_ref[...].astype(bf16))

    @pl.when(j == 0)
    def _():
        w = weights()
        for rows in groups:
            h = _modnorm(x_ref[rows, :], nw_ref[...], sh_ref[...], sc_ref[...]).astype(bf16)
            h_ref[rows, :] = h
            o_ref[rows, :] = down(h, w)

    @pl.when((j > 0) & (j < n_ff - 1))
    def _():
        w = weights()
        for rows in groups:
            o_ref[rows, :] += down(h_ref[rows, :], w)

    @pl.when(j == n_ff - 1)
    def _():
        w = weights()
        for rows in groups:
            y = x_ref[rows, :] + (0.5 * gt_ref[...]) * (o_ref[rows, :] + down(h_ref[rows, :], w))
            if final:
                y = y * lax.rsqrt(jnp.mean(y * y, axis=-1, keepdims=True) + EPS) * fw_ref[...]
            o_ref[rows, :] = y


def _ffn_call(x, mod5, norm_w4, wg, wu, wd, final_w, *, layer, sub, row_fn, tm, name):
    n_tok = x.shape[0]
    tf = 256
    n_ff = D_FF // tf
    final = final_w is not None
    in_specs = [
        pl.BlockSpec((tm, D_MODEL), lambda i, j: (i, 0)),
        _mod_spec(layer, sub * 3 + 0, row_fn),
        _mod_spec(layer, sub * 3 + 1, row_fn),
        _mod_spec(layer, sub * 3 + 2, row_fn),
        _vec_spec(layer, sub),
        pl.BlockSpec((None, D_MODEL, tf), lambda i, j: (layer, 0, j)),
        pl.BlockSpec((None, D_MODEL, tf), lambda i, j: (layer, 0, j)),
        pl.BlockSpec((None, tf, D_MODEL), lambda i, j: (layer, j, 0)),
    ]
    args = [x, mod5, mod5, mod5, norm_w4, wg, wu, wd]
    if final:
        in_specs.append(pl.BlockSpec((1, D_MODEL), lambda i, j: (0, 0)))
        args.append(final_w)
    return pl.pallas_call(
        functools.partial(_ffn_kernel, n_ff=n_ff, final=final),
        out_shape=jax.ShapeDtypeStruct((n_tok, D_MODEL), f32),
        grid=(n_tok // tm, n_ff),
        in_specs=in_specs,
        out_specs=pl.BlockSpec((tm, D_MODEL), lambda i, j: (i, 0)),
        scratch_shapes=[pltpu.VMEM((tm, D_MODEL), bf16)],
        compiler_params=_cparams(("parallel", "arbitrary")),
        name=name,
    )(*args)


def _swap16(x):
    n = x.shape[-1]
    lane = lax.broadcasted_iota(jnp.int32, x.shape, x.ndim - 1)
    first = (lane % AXIS_ROT) < (AXIS_ROT // 2)
    return jnp.where(first, pltpu.roll(x, n - AXIS_ROT // 2, x.ndim - 1),
                     pltpu.roll(x, AXIS_ROT // 2, x.ndim - 1))


INPROJ_ROW_GROUPS = 2


def _inproj_kernel(*refs, rope):
    if rope:
        (x_ref, sh_ref, sc_ref, nw_ref, w32_ref, cos_ref, sin_ref,
         q_ref, k_ref, v_ref, xr_ref, yr_ref, xf_ref, w_ref) = refs
    else:
        (x_ref, sh_ref, sc_ref, nw_ref, w32_ref,
         q_ref, k_ref, v_ref, xr_ref, yr_ref, xf_ref, w_ref) = refs

    @pl.when(pl.program_id(0) == 0)
    def _():
        for c0 in range(0, IN_WIDTH, MXU_N):
            w_ref[:, c0:c0 + MXU_N] = w32_ref[:, c0:c0 + MXU_N].astype(bf16)

    tm = x_ref.shape[0]
    for r in range(INPROJ_ROW_GROUPS):
        rows = slice(r * (tm // INPROJ_ROW_GROUPS), (r + 1) * (tm // INPROJ_ROW_GROUPS))
        h = _modnorm(x_ref[rows, :], nw_ref[...], sh_ref[...], sc_ref[...]).astype(bf16)
        if rope:
            cos = cos_ref[rows, :]
            sin = sin_ref[rows, :]

        def rot(t):
            return t * cos + _swap16(t) * sin if rope else t

        o = ATTN_WIDTH
        kv = _dot(h, w_ref[:, o:o + 2 * KV_WIDTH])
        k_ref[rows, :] = rot(kv[:, :KV_WIDTH])
        v_ref[rows, :] = kv[:, KV_WIDTH:]
        for c in range(ATTN_WIDTH // MXU_N):
            q = _dot(h, w_ref[:, c * MXU_N:(c + 1) * MXU_N])
            for t in range(MXU_N // LANES):
                c0 = c * MXU_N + t * LANES
                q_ref[rows, c0:c0 + LANES] = rot(q[:, t * LANES:(t + 1) * LANES]).astype(bf16)
        o += 2 * KV_WIDTH
        xr_ref[rows, :] = _dot(h, w_ref[:, o:o + LRU_WIDTH])
        o += LRU_WIDTH
        yr_ref[rows, :] = _dot(h, w_ref[:, o:o + LRU_WIDTH])
        o += LRU_WIDTH
        xf_ref[rows, :] = _dot(h, w_ref[:, o:o + FOURIER_WIDTH]).astype(bf16)


def _inproj_call(x, mod5, norm_w4, w_in, rope_tabs, *, layer, row_fn, tm, seq, name):
    n_tok = x.shape[0]
    rope = rope_tabs is not None
    in_specs = [
        pl.BlockSpec((tm, D_MODEL), lambda i: (i, 0)),
        _mod_spec(layer, 3, row_fn),
        _mod_spec(layer, 4, row_fn),
        _vec_spec(layer, 1),
        pl.BlockSpec((None, D_MODEL, IN_WIDTH), lambda i: (layer, 0, 0), pipeline_mode=pl.Buffered(1)),
    ]
    args = [x, mod5, mod5, norm_w4, w_in]
    if rope:
        per_seq = seq // tm
        in_specs += [pl.BlockSpec((tm, LANES), lambda i: (i % per_seq, 0))] * 2
        args += list(rope_tabs)
    widths = (ATTN_WIDTH, KV_WIDTH, KV_WIDTH, LRU_WIDTH, LRU_WIDTH, FOURIER_WIDTH)
    dtypes = (bf16, f32, f32, f32, f32, bf16)
    return pl.pallas_call(
        functools.partial(_inproj_kernel, rope=rope),
        out_shape=[jax.ShapeDtypeStruct((n_tok, w), d) for w, d in zip(widths, dtypes)],
        grid=(n_tok // tm,),
        in_specs=in_specs,
        out_specs=[pl.BlockSpec((tm, w), lambda i: (i, 0)) for w in widths],
        scratch_shapes=[pltpu.VMEM((D_MODEL, IN_WIDTH), bf16)],
        compiler_params=_cparams(("arbitrary",)),
        name=name,
    )(*args)


def _half_lane_variants(x):
    lane = lax.broadcasted_iota(jnp.int32, x.shape, 1)
    low = lane < HEAD_DIM
    sw = pltpu.roll(x, HEAD_DIM, 1)
    zero = jnp.zeros_like(x)
    head0 = (jnp.where(low, x, zero), jnp.where(low, zero, sw))
    head1 = (jnp.where(low, sw, zero), jnp.where(low, zero, x))
    return [tuple(v.astype(bf16) for v in h) for h in (head0, head1)]


def _sink_softmax(scores, sk):
    m = sk
    for s in scores:
        m = jnp.maximum(m, jnp.max(s, axis=-1, keepdims=True))
    denom = jnp.exp(sk - m)
    probs = []
    for s in scores:
        p = jnp.exp(s - m)
        denom = denom + jnp.sum(p, axis=-1, keepdims=True)
        probs.append(p.astype(bf16))
    return probs, denom


def _sink_heads(sink_ref, q_ref, kvar, vvar, bias_fns, o_ref):
    nq = q_ref.shape[0]
    top = lax.broadcasted_iota(jnp.int32, (2 * nq, 1), 0) < nq
    low = lax.broadcasted_iota(jnp.int32, (2 * nq, LANES), 1) < HEAD_DIM
    nt = (((1,), (1,)), ((), ()))
    for g in range(KV_HEADS):
        c0 = g * Q_GROUP * HEAD_DIM
        q2 = jnp.concatenate([q_ref[:, c0:c0 + LANES], q_ref[:, c0 + LANES:c0 + 2 * LANES]], axis=0)
        q2 = q2 * jnp.asarray(HEAD_DIM ** -0.5, bf16)
        parts = []
        for par in range(2):
            scores = []
            for piece, bias_fn in zip(kvar[g][par], bias_fns):
                s = lax.dot_general(q2, piece, nt, preferred_element_type=f32)
                scores.append(s if bias_fn is None else bias_fn(s))
            n_top = g * Q_GROUP + par
            sk = jnp.where(top, sink_ref[n_top], sink_ref[n_top + 2])
            parts.append(_sink_softmax(scores, sk))
        (p_e, d_e), (p_o, d_o) = parts
        o2 = None
        for probs, vals in ((p_e, vvar[g][0]), (p_o, vvar[g][1])):
            for p, v in zip(probs, vals):
                o2 = _dot(p, v) if o2 is None else o2 + _dot(p, v)
        o2 = o2 * jnp.where(low, 1.0 / d_e, 1.0 / d_o)
        o_ref[:, c0:c0 + LANES] = o2[:nq].astype(o_ref.dtype)
        o_ref[:, c0 + LANES:c0 + 2 * LANES] = o2[nq:].astype(o_ref.dtype)


CTX_ATTN_PER_STEP = 2


def _ctx_attn_kernel(sink_ref, q_ref, k_ref, v_ref, o_ref, *, seq):
    for s in range(CTX_ATTN_PER_STEP):
        rows = slice(s * seq, (s + 1) * seq)
        kvar = [[[v] for v in head] for head in _half_lane_variants(k_ref[rows, :])]
        vvar = [[[v] for v in head] for head in _half_lane_variants(v_ref[rows, :])]
        _sink_heads(sink_ref, q_ref.at[rows, :], kvar, vvar, [None], o_ref.at[rows, :])


def _ctx_attn_call(sink, q, k, v, *, batch, seq):
    per = CTX_ATTN_PER_STEP
    assert batch % per == 0
    return pl.pallas_call(
        functools.partial(_ctx_attn_kernel, seq=seq),
        out_shape=jax.ShapeDtypeStruct((batch * seq, ATTN_WIDTH), bf16),
        grid=(batch // per,),
        in_specs=[
            pl.BlockSpec(memory_space=pltpu.SMEM),
            pl.BlockSpec((per * seq, ATTN_WIDTH), lambda b: (b, 0)),
            pl.BlockSpec((per * seq, KV_WIDTH), lambda b: (b, 0)),
            pl.BlockSpec((per * seq, KV_WIDTH), lambda b: (b, 0)),
        ],
        out_specs=pl.BlockSpec((per * seq, ATTN_WIDTH), lambda b: (b, 0)),
        compiler_params=_cparams(("parallel",)),
        name="ctx_attn",
    )(sink, q, k, v)


ATTN_PREP_ROWS = 256


LAT_ATTN_PER_STEP = 4


def _lat_attn_kernel(sink_ref, q_ref, k_ref, v_ref, kx_ref, vx_ref, o_ref, kvar_ref, vvar_ref,
                     *, n_blocks, past):
    n = pl.program_id(1)
    seq = n_blocks * BLOCK
    ctx0 = seq + 2 * BLOCK

    @pl.when(n == 0)
    def _():
        zeros = jnp.zeros((BLOCK, KV_WIDTH), bf16)
        for src, ctx, dst in ((k_ref, kx_ref, kvar_ref), (v_ref, vx_ref, vvar_ref)):
            for i in range(2 * KV_HEADS):
                dst[i, 0:BLOCK, :] = zeros
                dst[i, BLOCK + seq:ctx0, :] = zeros
            for r0 in range(0, seq + past, ATTN_PREP_ROWS):
                rows = min(ATTN_PREP_ROWS, seq + past - r0)
                x = src[r0:r0 + rows, :] if r0 < seq else ctx[r0 - seq:r0 - seq + rows, :]
                d0 = BLOCK + r0 if r0 < seq else ctx0 + r0 - seq
                variants = _half_lane_variants(x)
                for g in range(KV_HEADS):
                    for par in range(2):
                        dst[2 * g + par, d0:d0 + rows, :] = variants[g][par]

    assert WINDOW == BLOCK
    a_idx = lax.broadcasted_iota(jnp.int32, (2 * BLOCK, BLOCK), 0) % BLOCK
    c_idx = lax.broadcasted_iota(jnp.int32, (2 * BLOCK, BLOCK), 1)
    for sub in range(LAT_ATTN_PER_STEP):
        blk = n * LAT_ATTN_PER_STEP + sub
        band0 = pl.multiple_of(blk * BLOCK, BLOCK)

        def pieces(ref, band0=band0):
            return [[[ref[2 * g + par, pl.ds(band0, 3 * BLOCK), :], ref[2 * g + par, ctx0:ctx0 + past, :]]
                     for par in range(2)] for g in range(KV_HEADS)]

        bias_p = jnp.where((c_idx >= a_idx) & (blk > 0), 0.0, NEG).astype(f32)
        bias_n = jnp.where((c_idx <= a_idx) & (blk < n_blocks - 1), 0.0, NEG).astype(f32)

        def band_bias(s, bias_p=bias_p, bias_n=bias_n):
            return jnp.concatenate([s[:, :BLOCK] + bias_p, s[:, BLOCK:2 * BLOCK],
                                    s[:, 2 * BLOCK:] + bias_n], axis=1)

        rows = slice(sub * BLOCK, (sub + 1) * BLOCK)
        _sink_heads(sink_ref, q_ref.at[rows, :], pieces(kvar_ref), pieces(vvar_ref),
                    [band_bias, None], o_ref.at[rows, :])


def _lat_attn_call(sink, q, k, v, cache_k4, cache_v4, *, layer, batch, seq):
    nb = seq // BLOCK
    past = cache_k4.shape[2]
    assert seq % ATTN_PREP_ROWS == 0
    per = LAT_ATTN_PER_STEP
    assert nb % per == 0
    steps = nb // per
    whole = pl.BlockSpec((seq, KV_WIDTH), lambda b, n: (b, 0))
    ctx = pl.BlockSpec((None, None, past, KV_WIDTH), lambda b, n: (b, layer, 0, 0))
    return pl.pallas_call(
        functools.partial(_lat_attn_kernel, n_blocks=nb, past=past),
        out_shape=jax.ShapeDtypeStruct((batch * seq, ATTN_WIDTH), bf16),
        grid=(batch, steps),
        in_specs=[
            pl.BlockSpec(memory_space=pltpu.SMEM),
            pl.BlockSpec((per * BLOCK, ATTN_WIDTH), lambda b, n: (b * steps + n, 0)),
            whole, whole, ctx, ctx,
        ],
        out_specs=pl.BlockSpec((per * BLOCK, ATTN_WIDTH), lambda b, n: (b * steps + n, 0)),
        scratch_shapes=[pltpu.VMEM((2 * KV_HEADS, seq + 2 * BLOCK + past, KV_WIDTH), bf16)] * 2,
        compiler_params=_cparams(("parallel", "arbitrary")),
        name="lat_attn",
    )(sink, q, k, v, cache_k4, cache_v4)


def _softplus(x):
    return jnp.maximum(x, 0.0) + jnp.log1p(jnp.exp(-jnp.abs(x)))


def _gelu_tanh(x):
    c = math.sqrt(2.0 / math.pi)
    t = jnp.tanh(x * (c + (0.044715 * c) * (x * x)))
    hx = 0.5 * x
    return hx + hx * t


SCAN_SKEW = SUBLANES // 2


def _scan_layout(seq):
    seg = seq // LRU_SEGMENTS
    assert seg % SUBLANES == 0 and LRU_SEGMENTS % 2 == 0
    pitch = seg + SCAN_SKEW
    pair = 2 * seg
    return pitch, pair, (LRU_SEGMENTS // 2) * (pair + SUBLANES)


def _scan_row(t, pair):
    return t + SUBLANES * (t // pair)


def _lru_kernel(xr_ref, yr_ref, cw_ref, cb_ref, wd_ref, bias_ref, lam_ref, h0_ref,
                rec_ref, st_ref, xs_ref, af_ref, bf_ref, ab_ref, bb_ref, *, seq):
    seg = seq // LRU_SEGMENTS
    chunk = min(seq, 256)
    hw = LRU_HALF
    n_lt = hw // LANES
    pitch, pair, _ = _scan_layout(seq)
    piece = min(pair, chunk)
    assert pair % piece == 0

    for m in range(LRU_SEGMENTS // 2):
        g0 = m * (pair + SUBLANES) + pair
        for t in range(n_lt):
            for a_ref, b_ref in ((af_ref, bf_ref), (ab_ref, bb_ref)):
                a_ref[t, g0:g0 + SUBLANES, :] = jnp.ones((SUBLANES, LANES), f32)
                b_ref[t, g0:g0 + SUBLANES, :] = jnp.zeros((SUBLANES, LANES), f32)

    zeros_pad = jnp.zeros((PAD_ROWS, hw), f32)
    xs_ref[0:PAD_ROWS, :] = zeros_pad
    xs_ref[PAD_ROWS + seq:2 * PAD_ROWS + seq, :] = zeros_pad
    for c in range(seq // chunk):
        xs_ref[PAD_ROWS + c * chunk:PAD_ROWS + (c + 1) * chunk, :] = xr_ref[c * chunk:(c + 1) * chunk, :]

    neg_c_sp = [-LRU_C * _softplus(-lam_ref[d:d + 1, :]) for d in range(2)]
    a_refs = (af_ref, ab_ref)
    b_refs = (bf_ref, bb_ref)

    for c in range(seq // chunk):
        r0 = c * chunk
        win = xs_ref[r0:r0 + chunk + 2 * PAD_ROWS, :]
        xc = cb_ref[...]
        for j in range(CONV_W):
            d = j - CONV_LEFT
            tap = win if d == 0 else pltpu.roll(win, (-d) % win.shape[0], 0)
            xc = xc + tap[PAD_ROWS:PAD_ROWS + chunk, :] * cw_ref[j:j + 1, :]
        y = _dot(xc.astype(bf16), wd_ref[...]) + bias_ref[...]
        for d in range(2):
            r = _sigmoid(y[:, (2 * d) * hw:(2 * d + 1) * hw])
            gi = _sigmoid(y[:, (2 * d + 1) * hw:(2 * d + 2) * hw])
            log_a = neg_c_sp[d] * r
            a = jnp.exp(log_a)
            z = jnp.tanh(log_a) * (-1.0 - a * a)
            b = jnp.where(z > 0.0, z * lax.rsqrt(z), 0.0) * (gi * xc)
            for p0 in range(0, chunk, piece):
                dst = _scan_row(r0 + p0, pair)
                for t in range(n_lt):
                    lanes = slice(t * LANES, (t + 1) * LANES)
                    a_refs[d][t, dst:dst + piece, :] = a[p0:p0 + piece, lanes]
                    b_refs[d][t, dst:dst + piece, :] = b[p0:p0 + piece, lanes]

    def step(j, carry):
        jf = pl.ds(j, LRU_SEGMENTS, stride=pitch)
        jb = pl.ds(pitch - 1 - j, LRU_SEGMENTS, stride=pitch)
        out = []
        for t in range(n_lt):
            hf, pf, hb, pb = carry[4 * t:4 * t + 4]
            a = af_ref[t, jf, :]
            hf = a * hf + bf_ref[t, jf, :]
            pf = a * pf
            bf_ref[t, jf, :] = hf
            af_ref[t, jf, :] = pf
            a = ab_ref[t, jb, :]
            hb = a * hb + bb_ref[t, jb, :]
            pb = a * pb
            bb_ref[t, jb, :] = hb
            ab_ref[t, jb, :] = pb
            out += [hf, pf, hb, pb]
        return tuple(out)

    z = jnp.zeros((LRU_SEGMENTS, LANES), f32)
    o = jnp.ones((LRU_SEGMENTS, LANES), f32)
    fin = lax.fori_loop(0, pitch, step, (z, o, z, o) * n_lt, unroll=2)

    sub = min(pair, 256)
    off = lax.broadcasted_iota(jnp.int32, (sub, LANES), 0)
    for t in range(n_lt):
        hf, pf, hb, pb = fin[4 * t:4 * t + 4]
        lanes = slice(t * LANES, (t + 1) * LANES)
        cf = h0_ref[0:1, lanes]
        carry_f = []
        for s in range(LRU_SEGMENTS):
            carry_f.append(cf)
            cf = hf[s:s + 1, :] + pf[s:s + 1, :] * cf
        cb = h0_ref[1:2, lanes]
        carry_b = [None] * LRU_SEGMENTS
        for s in reversed(range(LRU_SEGMENTS)):
            carry_b[s] = cb
            cb = hb[s:s + 1, :] + pb[s:s + 1, :] * cb
        st_ref[0:1, lanes] = cf
        st_ref[1:2, lanes] = cb
        for m in range(LRU_SEGMENTS // 2):
            for u in range(pair // sub):
                t0 = m * pair + u * sub
                rows = slice(t0, t0 + sub)
                src = slice(_scan_row(t0, pair), _scan_row(t0, pair) + sub)
                lo, hi = u * sub, (u + 1) * sub
                if hi <= pitch:
                    cfs, cbs = carry_f[2 * m], carry_b[2 * m]
                elif lo >= pitch:
                    cfs, cbs = carry_f[2 * m + 1], carry_b[2 * m + 1]
                else:
                    first = off < (pitch - lo)
                    cfs = jnp.where(first, carry_f[2 * m], carry_f[2 * m + 1])
                    cbs = jnp.where(first, carry_b[2 * m], carry_b[2 * m + 1])
                h = ((bf_ref[t, src, :] + af_ref[t, src, :] * cfs)
                     + (bb_ref[t, src, :] + ab_ref[t, src, :] * cbs))
                rec_ref[rows, lanes] = (h * _gelu_tanh(yr_ref[rows, lanes])).astype(rec_ref.dtype)


def _lru_call(xr, yr, conv_w, conv_b, wd, bias, lam, h0, *, layer, batch, seq):
    hw = LRU_HALF
    return pl.pallas_call(
        functools.partial(_lru_kernel, seq=seq),
        out_shape=[jax.ShapeDtypeStruct((batch * seq, LRU_WIDTH), bf16),
                   jax.ShapeDtypeStruct((batch, 2, LRU_WIDTH), f32)],
        grid=(batch, 2),
        in_specs=[
            pl.BlockSpec((seq, hw), lambda b, c: (b, c)),
            pl.BlockSpec((seq, hw), lambda b, c: (b, c)),
            pl.BlockSpec((None, CONV_W, hw), lambda b, c: (layer, 0, c)),
            pl.BlockSpec((None, 1, hw), lambda b, c: (layer, 0, c)),
            pl.BlockSpec((None, None, hw, 4 * hw), lambda b, c: (layer, c, 0, 0)),
            pl.BlockSpec((None, None, 1, 4 * hw), lambda b, c: (layer, c, 0, 0)),
            pl.BlockSpec((None, 2, hw), lambda b, c: (layer, 0, c)),
            pl.BlockSpec((None, 2, hw), lambda b, c: (b, 0, c)),
        ],
        out_specs=[pl.BlockSpec((seq, hw), lambda b, c: (b, c)),
                   pl.BlockSpec((None, 2, hw), lambda b, c: (b, 0, c))],
        scratch_shapes=([pltpu.VMEM((seq + 2 * PAD_ROWS, hw), f32)]
                        + [pltpu.VMEM((hw // LANES, _scan_layout(seq)[2], LANES), f32)] * 4),
        compiler_params=_cparams(("parallel", "parallel")),
        name=f"lru_{seq}",
    )(xr, yr, conv_w, conv_b, wd, bias, lam, h0)


FOURIER_REV = 128
FOURIER_PAD = 16
FOURIER_STEP_ROWS = 2048


def _fourier_kernel(x_ref, csc_ref, ch_ref, sh_ref, rev_ref, o_ref, u_ref, *, seq, per_step, scale):
    half = seq // 2
    chunk = min(seq, 512)
    w = FOURIER_WIDTH
    for s in range(per_step):
        for c in range(seq // chunk):
            rows = slice(s * seq + c * chunk, s * seq + (c + 1) * chunk)
            for g in range(FOURIER_GROUPS):
                cols = slice(g * FOURIER_GW, (g + 1) * FOURIER_GW)
                dst = slice(s * w + g * FOURIER_GW, s * w + (g + 1) * FOURIER_GW)
                u = _dot(x_ref[rows, cols], csc_ref[...])
                u_ref[c * chunk:(c + 1) * chunk, dst] = u[:, :FOURIER_GW].astype(bf16)
                u_ref[seq + c * chunk:seq + (c + 1) * chunk, dst] = u[:, FOURIER_GW:].astype(bf16)
    a = _dot(ch_ref[...], u_ref[0:seq, :])
    b = _dot(sh_ref[...], u_ref[seq:2 * seq, :])
    top = ((a[0:half] - b[0:half]) * scale).astype(o_ref.dtype)
    z = ((a[1:half + 1] + b[1:half + 1]) * scale).astype(bf16)
    nb = half // FOURIER_REV
    for s in range(per_step):
        cols = slice(s * w, (s + 1) * w)
        o_ref[s * seq:s * seq + half, :] = top[:, cols]
        for k in range(nb):
            blk = z[(nb - 1 - k) * FOURIER_REV:(nb - k) * FOURIER_REV, cols]
            r0 = s * seq + half + k * FOURIER_REV
            o_ref[r0:r0 + FOURIER_REV, :] = _dot(rev_ref[...], blk).astype(o_ref.dtype)


def _dft_tables(n):
    k = np.arange(n, dtype=np.int64)
    ang = (2.0 * np.pi / n) * ((k[:, None] * k[None, :]) % n).astype(np.float64)
    return np.cos(ang), np.sin(ang)


def _fourier_call(xf, csc, ch, sh, rev, *, batch, seq):
    scale = 1.0 / math.sqrt(seq * FOURIER_GW)
    rows = seq // 2 + FOURIER_PAD
    per_step = max(1, min(batch, FOURIER_STEP_ROWS // seq))
    assert batch % per_step == 0

    def const(shape):
        return pl.BlockSpec(shape, lambda b: (0, 0), pipeline_mode=pl.Buffered(1))

    return pl.pallas_call(
        functools.partial(_fourier_kernel, seq=seq, per_step=per_step, scale=scale),
        out_shape=jax.ShapeDtypeStruct((batch * seq, FOURIER_WIDTH), bf16),
        grid=(batch // per_step,),
        in_specs=[
            pl.BlockSpec((per_step * seq, FOURIER_WIDTH), lambda b: (b, 0)),
            const((FOURIER_GW, 2 * FOURIER_GW)),
            const((rows, seq)),
            const((rows, seq)),
            const((FOURIER_REV, FOURIER_REV)),
        ],
        out_specs=pl.BlockSpec((per_step * seq, FOURIER_WIDTH), lambda b: (b, 0)),
        scratch_shapes=[pltpu.VMEM((2 * seq, per_step * FOURIER_WIDTH), bf16)],
        compiler_params=_cparams(("parallel",)),
        name=f"fourier_{seq}",
    )(xf, csc, ch, sh, rev)


MERGE_CAST_COLS = 512


MERGE_ROW_GROUPS = 2


def _merge_kernel(x_ref, sh_ref, sc_ref, gt_ref, nw_ref, at_ref, rc_ref, fr_ref,
                  wbg32_ref, bbg_ref, wao32_ref, wlo32_ref, wfo32_ref, wo32_ref, o_ref,
                  wbg_ref, wao_ref, wlo_ref, wfo_ref, wo_ref):
    @pl.when(pl.program_id(0) == 0)
    def _():
        for src, dst in ((wbg32_ref, wbg_ref), (wao32_ref, wao_ref), (wlo32_ref, wlo_ref),
                         (wfo32_ref, wfo_ref), (wo32_ref, wo_ref)):
            for c0 in range(0, src.shape[1], MERGE_CAST_COLS):
                dst[:, c0:c0 + MERGE_CAST_COLS] = src[:, c0:c0 + MERGE_CAST_COLS].astype(bf16)

    tm = x_ref.shape[0]
    for r in range(MERGE_ROW_GROUPS):
        rows = slice(r * (tm // MERGE_ROW_GROUPS), (r + 1) * (tm // MERGE_ROW_GROUPS))
        x = x_ref[rows, :]
        h = _modnorm(x, nw_ref[...], sh_ref[...], sc_ref[...]).astype(bf16)
        merged = None
        for idx, (br_ref, w_ref) in enumerate(((at_ref, wao_ref), (rc_ref, wlo_ref), (fr_ref, wfo_ref))):
            cols = slice(idx * D_MODEL, (idx + 1) * D_MODEL)
            g = _sigmoid(_dot(h, wbg_ref[:, cols]) + bbg_ref[:, cols])
            y = g * _dot(br_ref[rows, :], w_ref[...])
            merged = y if merged is None else merged + y
        out = _dot(merged.astype(bf16), wo_ref[...])
        o_ref[rows, :] = x + gt_ref[...] * out


def _merge_call(x, mod5, norm_w4, attn, rec, four, wbg, bbg, wao, wlo, wfo, wo, *, layer, row_fn, tm, name):
    n_tok = x.shape[0]

    def const(shape):
        nd = len(shape)
        return pl.BlockSpec((None,) + shape, lambda i: (layer,) + (0,) * nd,
                            pipeline_mode=pl.Buffered(1))

    def branch():
        return pl.BlockSpec((tm, ATTN_WIDTH), lambda i: (i, 0))

    return pl.pallas_call(
        _merge_kernel,
        out_shape=jax.ShapeDtypeStruct((n_tok, D_MODEL), f32),
        grid=(n_tok // tm,),
        in_specs=[
            pl.BlockSpec((tm, D_MODEL), lambda i: (i, 0)),
            _mod_spec(layer, 3, row_fn),
            _mod_spec(layer, 4, row_fn),
            _mod_spec(layer, 5, row_fn),
            _vec_spec(layer, 1),
            branch(), branch(), branch(),
            const((D_MODEL, N_BRANCH * D_MODEL)),
            const((1, N_BRANCH * D_MODEL)),
            const((ATTN_WIDTH, D_MODEL)),
            const((LRU_WIDTH, D_MODEL)),
            const((FOURIER_WIDTH, D_MODEL)),
            const((D_MODEL, D_MODEL)),
        ],
        out_specs=pl.BlockSpec((tm, D_MODEL), lambda i: (i, 0)),
        scratch_shapes=[pltpu.VMEM((D_MODEL, N_BRANCH * D_MODEL), bf16),
                        pltpu.VMEM((ATTN_WIDTH, D_MODEL), bf16),
                        pltpu.VMEM((LRU_WIDTH, D_MODEL), bf16),
                        pltpu.VMEM((FOURIER_WIDTH, D_MODEL), bf16),
                        pltpu.VMEM((D_MODEL, D_MODEL), bf16)],
        compiler_params=_cparams(("arbitrary",)),
        name=name,
    )(x, mod5, mod5, mod5, norm_w4, attn, rec, four, wbg, bbg, wao, wlo, wfo, wo)


def _rope_lane_tables(n_tokens):
    rows = n_tokens // GRID_W
    row = np.repeat(np.arange(rows), GRID_W).astype(np.float64)
    col = np.tile(np.arange(GRID_W), rows).astype(np.float64)
    inv = ROPE_BASE ** (-np.arange(0, AXIS_ROT, 2, dtype=np.float64) / AXIS_ROT)
    ang = np.stack([row[:, None] * inv, col[:, None] * inv], axis=1)
    cos, sin = np.cos(ang), np.sin(ang)
    cos_h = np.stack([cos, cos], axis=2).reshape(n_tokens, HEAD_DIM)
    sin_h = np.stack([-sin, sin], axis=2).reshape(n_tokens, HEAD_DIM)
    reps = LANES // HEAD_DIM
    return (jnp.asarray(np.tile(cos_h, (1, reps)), f32), jnp.asarray(np.tile(sin_h, (1, reps)), f32))


def _lru_dense_weights(lru_wa, lru_wi, lru_ba, lru_bi):
    per = LRU_BLOCKS // 2
    hw = LRU_HALF
    eye = jnp.eye(per, dtype=bf16)

    def dense(w, c):
        wh = w[:, :, c * per:(c + 1) * per].astype(bf16)
        d = wh[:, :, :, :, None, :] * eye[None, None, :, None, :, None]
        return d.reshape(DEPTH, 2, hw, hw)

    halves, biases = [], []
    for c in range(2):
        sl = slice(c * hw, (c + 1) * hw)
        da, di = dense(lru_wa, c), dense(lru_wi, c)
        halves.append(jnp.concatenate([da[:, 0], di[:, 0], da[:, 1], di[:, 1]], axis=-1))
        biases.append(jnp.concatenate(
            [lru_ba[:, 0, sl], lru_bi[:, 0, sl], lru_ba[:, 1, sl], lru_bi[:, 1, sl]], axis=-1))
    wd = jnp.stack(halves, axis=1)
    bias = jnp.stack(biases, axis=1)[:, :, None, :]
    return wd, bias


def kernel(x_prompt, x_sample, c, cache_k, cache_v, state_lru, c_ctx, w_ada, b_ada, norm_w, final_norm_w,
           ffn1_wg, ffn1_wu, ffn1_wd, ffn2_wg, ffn2_wu, ffn2_wd, w_in, w_branch_gate, b_branch_gate,
           attn_sink, w_attn_out, conv_w, conv_b, lru_wa, lru_ba, lru_wi, lru_bi, lru_lambda,
           w_lru_out, w_fourier_out, w_o):
    batch, seq, _ = x_prompt.shape
    dec_batch, dec_seq, _ = x_sample.shape
    past = cache_k.shape[2]
    assert 1 + dec_batch <= COND_ROWS

    cond = jnp.concatenate([c_ctx[None, :], c, jnp.zeros((COND_ROWS - 1 - dec_batch, D_MODEL), f32)], axis=0)
    mod = _ada_call(cond, w_ada, b_ada)
    mod5 = mod.reshape(DEPTH, COND_ROWS, N_SUB * 3, D_MODEL).transpose(0, 2, 1, 3)[:, :, :, None, :]
    norm_w4 = norm_w[:, :, None, :]
    final_w = final_norm_w[None, :]

    bbg = b_branch_gate[:, None, :]
    merge_w = (w_branch_gate, bbg, w_attn_out, w_lru_out, w_fourier_out, w_o)
    lru_wd, lru_bias = _lru_dense_weights(lru_wa, lru_wi, lru_ba, lru_bi)
    conv_b3 = conv_b[:, None, :]
    rope_tabs = _rope_lane_tables(dec_seq)
    cc, sc_ = _dft_tables(FOURIER_GW)
    csc = jnp.asarray(np.concatenate([cc, sc_], axis=1), f32).astype(bf16)
    dft = {}
    for n in (seq, dec_seq):
        cl, sl = _dft_tables(n)
        rows = n // 2 + FOURIER_PAD
        dft[n] = (jnp.asarray(cl[:rows], f32).astype(bf16), jnp.asarray(sl[:rows], f32).astype(bf16))
    rev = jnp.asarray(np.eye(FOURIER_REV)[::-1], f32).astype(bf16)
    cache_k4 = cache_k.reshape(dec_batch, DEPTH, past, KV_WIDTH)
    cache_v4 = cache_v.reshape(dec_batch, DEPTH, past, KV_WIDTH)
    h0_ctx = jnp.zeros((batch, 2, LRU_WIDTH), f32)

    tm_p = tm_s = FFN_TOKEN_TILE
    tmi = INPROJ_TOKEN_TILE
    tmm = MERGE_TOKEN_TILE
    for tile in (tm_p, tmi, tmm):
        assert (batch * seq) % tile == 0 and dec_seq % tile == 0
    row_p = row_pi = row_pm = lambda i: 0
    row_s = lambda i: 1 + (i * tm_s) // dec_seq
    row_si = lambda i: 1 + (i * tmi) // dec_seq
    row_sm = lambda i: 1 + (i * tmm) // dec_seq

    xp = x_prompt.reshape(batch * seq, D_MODEL)
    xs = x_sample.reshape(dec_batch * dec_seq, D_MODEL)
    ks, vs, ss = [], [], []
    for l in range(DEPTH):
        last = l == DEPTH - 1
        sink = attn_sink[l]
        xp = _ffn_call(xp, mod5, norm_w4, ffn1_wg, ffn1_wu, ffn1_wd, None,
                       layer=l, sub=0, row_fn=row_p, tm=tm_p, name="ffn1_ctx")
        xs = _ffn_call(xs, mod5, norm_w4, ffn1_wg, ffn1_wu, ffn1_wd, None,
                       layer=l, sub=0, row_fn=row_s, tm=tm_s, name="ffn1_lat")
        q, k, v, xr, yr, xf = _inproj_call(xp, mod5, norm_w4, w_in, None,
                                           layer=l, row_fn=row_pi, tm=tmi, seq=seq, name="inproj_ctx")
        ks.append(k.reshape(batch, seq, KV_HEADS, HEAD_DIM))
        vs.append(v.reshape(batch, seq, KV_HEADS, HEAD_DIM))
        attn = _ctx_attn_call(sink, q, k, v, batch=batch, seq=seq)
        rec, st = _lru_call(xr, yr, conv_w, conv_b3, lru_wd, lru_bias, lru_lambda, h0_ctx,
                            layer=l, batch=batch, seq=seq)
        ss.append(st)
        four = _fourier_call(xf, csc, *dft[seq], rev, batch=batch, seq=seq)
        xp = _merge_call(xp, mod5, norm_w4, attn, rec, four, *merge_w,
                         layer=l, row_fn=row_pm, tm=tmm, name="merge_ctx")
        q, k, v, xr, yr, xf = _inproj_call(xs, mod5, norm_w4, w_in, rope_tabs,
                                           layer=l, row_fn=row_si, tm=tmi, seq=dec_seq, name="inproj_lat")
        attn = _lat_attn_call(sink, q, k, v, cache_k4, cache_v4, layer=l, batch=dec_batch, seq=dec_seq)
        rec, _ = _lru_call(xr, yr, conv_w, conv_b3, lru_wd, lru_bias, lru_lambda, state_lru[:, l],
                           layer=l, batch=dec_batch, seq=dec_seq)
        four = _fourier_call(xf, csc, *dft[dec_seq], rev, batch=dec_batch, seq=dec_seq)
        xs = _merge_call(xs, mod5, norm_w4, attn, rec, four, *merge_w,
                         layer=l, row_fn=row_sm, tm=tmm, name="merge_lat")
        xp = _ffn_call(xp, mod5, norm_w4, ffn2_wg, ffn2_wu, ffn2_wd, final_w if last else None,
                       layer=l, sub=2, row_fn=row_p, tm=tm_p, name="ffn2_ctx")
        xs = _ffn_call(xs, mod5, norm_w4, ffn2_wg, ffn2_wu, ffn2_wd, final_w if last else None,
                       layer=l, sub=2, row_fn=row_s, tm=tm_s, name="ffn2_lat")

    y_prompt = xp.reshape(batch, seq, D_MODEL)
    y_sample = xs.reshape(dec_batch, dec_seq, D_MODEL)
    return (y_prompt, y_sample, jnp.stack(ks, axis=1), jnp.stack(vs, axis=1), jnp.stack(ss, axis=1))
```
